```python
import jax, jax.numpy as jnp
from jax import lax
import numpy as np

D_MODEL = 2048
BATCH = 8
SEQ = 4096
DEPTH = 1
DEC_BATCH = 32
DEC_SEQ = 32
PAST_LEN = 4096

CHUNK = 64
QBLOCK = 64
N_HEADS = 8
N_KV_HEADS = 4
HEAD_DIM = 128
GQA_GROUP = N_HEADS // N_KV_HEADS
ATT_WIDTH = N_HEADS * HEAD_DIM
KV_WIDTH = N_KV_HEADS * HEAD_DIM
IDX_HEADS = 16
IDX_DIM = 64
TOPK_MAX = 256
M_HEADS = 4
M_HEAD_DIM = 256
M_WIDTH = M_HEADS * M_HEAD_DIM
CONV_W = 4
N_GROUPS = 4
EXP_PER_GROUP = 8
N_EXPERTS = N_GROUPS * EXP_PER_GROUP
TOP_K_INNER = 2
D_EXPERT = 512
PLE_DIM = 256
EPS = 1e-6

PROJ_SIZES = (ATT_WIDTH, KV_WIDTH, KV_WIDTH, IDX_HEADS * IDX_DIM, IDX_DIM, IDX_HEADS,
              M_WIDTH, M_WIDTH, M_WIDTH, M_HEADS, M_HEADS, M_WIDTH, D_MODEL, D_MODEL)
D_PROJ = sum(PROJ_SIZES)

kernel_name = "hybrid_dsa_mlstm_hmoe_stream_step"


def rmsnorm(x, g):
    xf = x.astype(jnp.float32)
    y = xf * lax.rsqrt(jnp.mean(xf * xf, axis=-1, keepdims=True) + EPS)
    return (y * g.astype(jnp.float32)).astype(x.dtype)


def split_proj(z):
    outs = []
    off = 0
    for s in PROJ_SIZES:
        outs.append(z[..., off:off + s])
        off += s
    return outs


def causal_conv(u, prev, w, b):
    T = u.shape[1]
    up = jnp.concatenate([prev.astype(u.dtype), u], axis=1)
    y = b.astype(u.dtype) + up[:, 0:T] * w[0]
    for j in range(1, CONV_W):
        y = y + up[:, j:j + T] * w[j]
    return jax.nn.silu(y), up[:, -(CONV_W - 1):]


def dsa_attention(q, qi, wi, k_all, v_all, ki_all, q_pos, k_pos):
    B, T = q.shape[0], q.shape[1]
    S = k_all.shape[1]
    topk = min(TOPK_MAX, S // 4)
    qb = QBLOCK if T % QBLOCK == 0 else T
    nb = T // qb
    k_chunk = k_pos // CHUNK

    def blocks(a):
        return jnp.moveaxis(a.reshape((B, nb, qb) + a.shape[2:]), 1, 0)

    gather_rows = jax.vmap(lambda a, i: a[i])

    def one_block(args):
        q_b, qi_b, wi_b, qpos_b = args
        dots = jnp.einsum('bqhd,bsd->bqhs', qi_b, ki_all, preferred_element_type=jnp.float32) * (IDX_DIM ** -0.5)
        iscore = jnp.einsum('bqh,bqhs->bqs', wi_b.astype(jnp.float32) * (IDX_HEADS ** -0.5), jax.nn.relu(dots))
        admiss = k_chunk[None, :] <= (qpos_b // CHUNK)[:, None]
        iscore = jnp.where(admiss[None], iscore, -jnp.inf)
        sel_score, sel = lax.top_k(iscore, topk)
        valid = jnp.isfinite(sel_score)
        flat = sel.reshape(B, qb * topk)
        k_sel = gather_rows(k_all, flat).reshape(B, qb, topk, N_KV_HEADS, HEAD_DIM)
        v_sel = gather_rows(v_all, flat).reshape(B, qb, topk, N_KV_HEADS, HEAD_DIM)
        qg = q_b.reshape(B, qb, N_KV_HEADS, GQA_GROUP, HEAD_DIM)
        s = jnp.einsum('bqhgd,bqkhd->bqhgk', qg, k_sel, preferred_element_type=jnp.float32) * (HEAD_DIM ** -0.5)
        s = jnp.where(valid[:, :, None, None, :], s, -jnp.inf)
        p = jax.nn.softmax(s, axis=-1).astype(v_sel.dtype)
        o = jnp.einsum('bqhgk,bqkhd->bqhgd', p, v_sel)
        return o.reshape(B, qb, ATT_WIDTH)

    out = lax.map(one_block, (blocks(q), blocks(qi), blocks(wi), q_pos.reshape(nb, qb)))
    return jnp.moveaxis(out, 0, 1).reshape(B, T, ATT_WIDTH)


def mlstm_chunkwise(q, k, v, ig, lf, C0, n0, m0):
    B, T = q.shape[0], q.shape[1]
    L = CHUNK if T % CHUNK == 0 else T
    nc = T // L

    def chunks(a):
        a = jnp.moveaxis(a.reshape((B, nc, L) + a.shape[2:]), 1, 0)
        return jnp.swapaxes(a, 2, 3)

    causal = jnp.tril(jnp.ones((L, L), dtype=bool))

    def step(carry, inp):
        C, n, m = carry
        qc, kc, vc, ic, fc = inp
        b = jnp.cumsum(fc, axis=-1)
        dmat = jnp.where(causal, b[..., :, None] - b[..., None, :] + ic[..., None, :], -jnp.inf)
        m_inter = b + m[..., None]
        m_i = jnp.maximum(m_inter, jnp.max(dmat, axis=-1))
        s = jnp.einsum('bhid,bhjd->bhij', qc, kc) * jnp.exp(dmat - m_i[..., None])
        scale = jnp.exp(m_inter - m_i)
        num = jnp.einsum('bhij,bhje->bhie', s, vc) + scale[..., None] * jnp.einsum('bhid,bhde->bhie', qc, C)
        den = jnp.sum(s, axis=-1) + scale * jnp.einsum('bhid,bhd->bhi', qc, n)
        h = num / jnp.maximum(jnp.abs(den), jnp.exp(-m_i))[..., None]
        m_new = m_i[..., -1]
        decay = jnp.exp(b[..., -1] + m - m_new)
        w = jnp.exp(b[..., -1:] - b + ic - m_new[..., None])
        C_new = decay[..., None, None] * C + jnp.einsum('bhj,bhjd,bhje->bhde', w, kc, vc)
        n_new = decay[..., None] * n + jnp.einsum('bhj,bhjd->bhd', w, kc)
        return (C_new, n_new, m_new), h

    (C, n, m), h = lax.scan(step, (C0, n0, m0), (chunks(q), chunks(k), chunks(v), chunks(ig), chunks(lf)))
    h = jnp.moveaxis(jnp.swapaxes(h, 2, 3), 0, 1).reshape(B, T, M_HEADS, M_HEAD_DIM)
    return h, C, n, m


def hier_moe(h, router_gw, router_gb, router_ew, router_eb, w1, w3, w2):
    B, T, D = h.shape
    hf = h.reshape(B * T, D)
    N = B * T
    pg = jax.nn.softmax(jnp.matmul(hf, router_gw, preferred_element_type=jnp.float32) + router_gb.astype(jnp.float32), axis=-1)
    g_star = jnp.argmax(pg, axis=-1)
    pg_star = jnp.max(pg, axis=-1)
    le = (jnp.matmul(hf, router_ew, preferred_element_type=jnp.float32) + router_eb.astype(jnp.float32))
    le = le.reshape(N, N_GROUPS, EXP_PER_GROUP)
    le_g = le[jnp.arange(N), g_star]
    top_v, top_i = lax.top_k(le_g, TOP_K_INNER)
    wts = pg_star[:, None] * jax.nn.softmax(top_v, axis=-1)
    eid = g_star[:, None] * EXP_PER_GROUP + top_i
    gates = jnp.sum(jax.nn.one_hot(eid, N_EXPERTS, dtype=jnp.float32) * wts[..., None], axis=1)
    y = jnp.zeros((N, D), jnp.float32)
    for e in range(N_EXPERTS):
        a = jax.nn.silu(hf @ w1[e]) * (hf @ w3[e])
        y = y + gates[:, e:e + 1] * (a @ w2[e]).astype(jnp.float32)
    return y.astype(h.dtype).reshape(B, T, D)


def hybrid_layer(x, pe, past_k, past_v, past_ki, conv_prev, C0, n0, m0,
                 g_mix, w_in, conv_w, conv_b, b_igate, b_fgate, mh_gain, w_att_out, w_mlstm_out, w_out,
                 g_ffn, router_gw, router_gb, router_ew, router_eb, w1, w3, w2,
                 g_ple, w_ple, w_ple_gate):
    f32 = jnp.float32
    B, T, _ = x.shape
    P = past_k.shape[1]
    h = rmsnorm(x, g_mix)
    z = h @ w_in
    qa, ka, va, qi, ki, wi, qm, km, vm, ig, fg, og, ga, gb = split_proj(z)
    k_new = ka.reshape(B, T, N_KV_HEADS, HEAD_DIM)
    v_new = va.reshape(B, T, N_KV_HEADS, HEAD_DIM)
    k_all = jnp.concatenate([past_k.astype(x.dtype), k_new], axis=1)
    v_all = jnp.concatenate([past_v.astype(x.dtype), v_new], axis=1)
    ki_all = jnp.concatenate([past_ki.astype(x.dtype), ki], axis=1)
    q_pos = P + jnp.arange(T)
    k_pos = jnp.arange(P + T)
    att = dsa_attention(qa.reshape(B, T, N_HEADS, HEAD_DIM), qi.reshape(B, T, IDX_HEADS, IDX_DIM), wi,
                        k_all, v_all, ki_all, q_pos, k_pos)
    y_a = att @ w_att_out
    qk, conv_new = causal_conv(jnp.concatenate([qm, km], axis=-1), conv_prev, conv_w, conv_b)
    q_m = qk[..., :M_WIDTH].reshape(B, T, M_HEADS, M_HEAD_DIM).astype(f32)
    k_m = qk[..., M_WIDTH:].reshape(B, T, M_HEADS, M_HEAD_DIM).astype(f32) * (M_HEAD_DIM ** -0.5)
    v_m = vm.reshape(B, T, M_HEADS, M_HEAD_DIM).astype(f32)
    i_pre = ig.astype(f32) + b_igate.astype(f32)
    log_f = jax.nn.log_sigmoid(fg.astype(f32) + b_fgate.astype(f32))
    hm, C, n, m = mlstm_chunkwise(q_m, k_m, v_m, i_pre, log_f, C0.astype(f32), n0.astype(f32), m0.astype(f32))
    hm = hm * lax.rsqrt(jnp.mean(hm * hm, axis=-1, keepdims=True) + EPS) * mh_gain.astype(f32).reshape(M_HEADS, M_HEAD_DIM)
    hm = (hm.reshape(B, T, M_WIDTH) * jax.nn.sigmoid(og.astype(f32))).astype(x.dtype)
    y_b = hm @ w_mlstm_out
    mixed = jax.nn.sigmoid(ga) * y_a + jax.nn.sigmoid(gb) * y_b
    x = x + mixed @ w_out
    x = x + hier_moe(rmsnorm(x, g_ffn), router_gw, router_gb, router_ew, router_eb, w1, w3, w2)
    gate = jax.nn.sigmoid(rmsnorm(x, g_ple) @ w_ple_gate)
    x = x + gate * (pe.astype(x.dtype) @ w_ple)
    sd = C0.dtype
    return x, (k_new, v_new, ki, conv_new, C.astype(sd), n.astype(sd), m.astype(sd))


def setup_inputs(seed: int = 0) -> dict:
    key = jax.random.key(seed)
    ks = list(jax.random.split(key, 40))
    cnt = [0]

    def nrm(shape, scale):
        k = ks[cnt[0]]
        cnt[0] += 1
        return jax.random.normal(k, shape, jnp.float32) * scale

    def gain(shape):
        return 1.0 + nrm(shape, 0.02)

    d = {}
    d['x_prompt'] = nrm((BATCH, SEQ, D_MODEL), 1.0)
    d['x_sample'] = nrm((DEC_BATCH, DEC_SEQ, D_MODEL), 1.0)
    d['cache_k'] = nrm((DEPTH, DEC_BATCH, PAST_LEN, N_KV_HEADS, HEAD_DIM), 1.0)
    d['cache_v'] = nrm((DEPTH, DEC_BATCH, PAST_LEN, N_KV_HEADS, HEAD_DIM), 1.0)
    d['cache_kidx'] = nrm((DEPTH, DEC_BATCH, PAST_LEN, IDX_DIM), 1.0)
    d['state_conv'] = nrm((DEPTH, DEC_BATCH, CONV_W - 1, 2 * M_WIDTH), 1.0)
    d['state_C'] = nrm((DEPTH, DEC_BATCH, M_HEADS, M_HEAD_DIM, M_HEAD_DIM), 0.05)
    d['state_n'] = nrm((DEPTH, DEC_BATCH, M_HEADS, M_HEAD_DIM), 0.1)
    d['state_m'] = nrm((DEPTH, DEC_BATCH, M_HEADS), 0.5)
    d['p_prompt'] = nrm((DEPTH, BATCH, SEQ, PLE_DIM), 1.0)
    d['p_sample'] = nrm((DEPTH, DEC_BATCH, DEC_SEQ, PLE_DIM), 1.0)
    d['g_mix'] = gain((DEPTH, D_MODEL))
    d['w_in'] = nrm((DEPTH, D_MODEL, D_PROJ), D_MODEL ** -0.5)
    d['conv_w'] = nrm((DEPTH, CONV_W, 2 * M_WIDTH), CONV_W ** -0.5)
    d['conv_b'] = nrm((DEPTH, 2 * M_WIDTH), 0.01)
    d['b_igate'] = nrm((DEPTH, M_HEADS), 0.1)
    d['b_fgate'] = jnp.linspace(3.0, 6.0, M_HEADS, dtype=jnp.float32)[None, :] + nrm((DEPTH, M_HEADS), 0.1)
    d['mh_gain'] = gain((DEPTH, M_WIDTH))
    d['w_att_out'] = nrm((DEPTH, ATT_WIDTH, D_MODEL), ATT_WIDTH ** -0.5)
    d['w_mlstm_out'] = nrm((DEPTH, M_WIDTH, D_MODEL), M_WIDTH ** -0.5)
    d['w_out'] = nrm((DEPTH, D_MODEL, D_MODEL), D_MODEL ** -0.5)
    d['g_ffn'] = gain((DEPTH, D_MODEL))
    d['router_gw'] = nrm((DEPTH, D_MODEL, N_GROUPS), D_MODEL ** -0.5)
    d['router_gb'] = nrm((DEPTH, N_GROUPS), 0.01)
    d['router_ew'] = nrm((DEPTH, D_MODEL, N_EXPERTS), D_MODEL ** -0.5)
    d['router_eb'] = nrm((DEPTH, N_EXPERTS), 0.01)
    d['w1'] = nrm((DEPTH, N_EXPERTS, D_MODEL, D_EXPERT), D_MODEL ** -0.5)
    d['w3'] = nrm((DEPTH, N_EXPERTS, D_MODEL, D_EXPERT), D_MODEL ** -0.5)
    d['w2'] = nrm((DEPTH, N_EXPERTS, D_EXPERT, D_MODEL), D_EXPERT ** -0.5)
    d['g_ple'] = gain((DEPTH, D_MODEL))
    d['w_ple'] = nrm((DEPTH, PLE_DIM, D_MODEL), PLE_DIM ** -0.5)
    d['w_ple_gate'] = nrm((DEPTH, D_MODEL, D_MODEL), D_MODEL ** -0.5)
    d['g_final'] = gain((D_MODEL,))
    return d


def reference(x_prompt, x_sample, cache_k, cache_v, cache_kidx, state_conv, state_C, state_n, state_m,
              p_prompt, p_sample, g_mix, w_in, conv_w, conv_b, b_igate, b_fgate, mh_gain,
              w_att_out, w_mlstm_out, w_out, g_ffn, router_gw, router_gb, router_ew, router_eb,
              w1, w3, w2, g_ple, w_ple, w_ple_gate, g_final):
    yp = x_prompt
    ys = x_sample
    Bp = x_prompt.shape[0]
    sdt = state_C.dtype
    new_p = [[] for _ in range(7)]
    new_s = [[] for _ in range(7)]
    for i in range(DEPTH):
        lw = (g_mix[i], w_in[i], conv_w[i], conv_b[i], b_igate[i], b_fgate[i], mh_gain[i],
              w_att_out[i], w_mlstm_out[i], w_out[i], g_ffn[i], router_gw[i], router_gb[i],
              router_ew[i], router_eb[i], w1[i], w3[i], w2[i], g_ple[i], w_ple[i], w_ple_gate[i])
        yp, st_p = hybrid_layer(
            yp, p_prompt[i],
            jnp.zeros((Bp, 0, N_KV_HEADS, HEAD_DIM), yp.dtype),
            jnp.zeros((Bp, 0, N_KV_HEADS, HEAD_DIM), yp.dtype),
            jnp.zeros((Bp, 0, IDX_DIM), yp.dtype),
            jnp.zeros((Bp, CONV_W - 1, 2 * M_WIDTH), yp.dtype),
            jnp.zeros((Bp, M_HEADS, M_HEAD_DIM, M_HEAD_DIM), sdt),
            jnp.zeros((Bp, M_HEADS, M_HEAD_DIM), sdt),
            jnp.zeros((Bp, M_HEADS), sdt),
            *lw)
        ys, st_s = hybrid_layer(
            ys, p_sample[i], cache_k[i], cache_v[i], cache_kidx[i], state_conv[i],
            state_C[i], state_n[i], state_m[i], *lw)
        for lst, s in zip(new_p, st_p):
            lst.append(s)
        for lst, s in zip(new_s, st_s):
            lst.append(s)
    y_prompt = rmsnorm(yp, g_final)
    y_sample = rmsnorm(ys, g_final)
    pk, pv, pki, pconv, pC, pn, pm = [jnp.stack(l, axis=0) for l in new_p]
    sk, sv, ski, sconv, sC, sn, sm = [jnp.stack(l, axis=0) for l in new_s]
    return (y_prompt, y_sample, pk, pv, pki, pconv, pC, pn, pm, sk, sv, ski, sconv, sC, sn, sm)
```

```python
import functools

import numpy as np
import jax
import jax.numpy as jnp
from jax import lax
from jax.experimental import pallas as pl
from jax.experimental.pallas import tpu as pltpu

F32 = jnp.float32
BF16 = jnp.bfloat16
I32 = jnp.int32

D_MODEL = 2048
CHUNK = 64
CHUNK_SHIFT = 6
assert 1 << CHUNK_SHIFT == CHUNK
N_HEADS = 8
N_KV_HEADS = 4
HEAD_DIM = 128
ATT_WIDTH = N_HEADS * HEAD_DIM
KV_WIDTH = N_KV_HEADS * HEAD_DIM
IDX_HEADS = 16
IDX_DIM = 64
TOPK_MAX = 256
M_HEADS = 4
M_HEAD_DIM = 256
M_WIDTH = M_HEADS * M_HEAD_DIM
CONV_W = 4
N_GROUPS = 4
EXP_PER_GROUP = 8
N_EXPERTS = N_GROUPS * EXP_PER_GROUP
D_EXPERT = 512
PLE_DIM = 256
EPS = 1e-6
PROJ_SIZES = (ATT_WIDTH, KV_WIDTH, KV_WIDTH, IDX_HEADS * IDX_DIM, IDX_DIM, IDX_HEADS,
              M_WIDTH, M_WIDTH, M_WIDTH, M_HEADS, M_HEADS, M_WIDTH, D_MODEL, D_MODEL)

LANES = 128
SUBLANES = 8
VMEM_LIMIT = 56 * 1024 * 1024

OFF_QA, OFF_QI, OFF_QM, OFF_KM, OFF_VM, OFF_OG = 0, 1024, 2048, 3072, 4096, 5120
OFF_GA, OFF_GB, OFF_KA, OFF_VA, OFF_SM = 6144, 8192, 10240, 10752, 11264
SM_WIDTH = 512
D_PACK = OFF_SM + SM_WIDTH
SM_KI, SM_WI, SM_IG, SM_FG = 0, 64, 80, 84
PROJ_TN = 512
KV_BLK0 = OFF_KA // PROJ_TN
KV_PACK = D_PACK - OFF_KA

INT_MIN = np.int32(-2 ** 31)
NEG_BIG = -1e30

DSA_KT = 512
MOE_TM = 256
TOK_TM = 256


def _cparams(n_axes):
    return pltpu.CompilerParams(dimension_semantics=("arbitrary",) * n_axes,
                                vmem_limit_bytes=VMEM_LIMIT)


def _dot(a, b):
    return jnp.dot(a, b, preferred_element_type=F32)


def _dot_nt(a, b):
    return lax.dot_general(a, b, (((1,), (1,)), ((), ())), preferred_element_type=F32)


def _dot_tn(a, b):
    return lax.dot_general(a, b, (((0,), (0,)), ((), ())), preferred_element_type=F32)


def _tile_rows(t, size):
    start = t * size
    return pl.ds(start if isinstance(start, int) else pl.multiple_of(start, size), size)


def _sigmoid(x):
    return 1.0 / (1.0 + jnp.exp(-x))


def _split3(x):
    hi = x.astype(BF16)
    r1 = x - hi.astype(F32)
    mid = r1.astype(BF16)
    lo = (r1 - mid.astype(F32)).astype(BF16)
    return hi, mid, lo


def _inproj_kernel(x_ref, g_ref, w_ref, z_ref, kv_ref, h_ref):
    j = pl.program_id(1)

    @pl.when(j == 0)
    def _():
        x = x_ref[...]
        r = lax.rsqrt(jnp.mean(x * x, axis=-1, keepdims=True) + EPS)
        h_ref[...] = (x * r * g_ref[...]).astype(BF16)

    acc = _dot(h_ref[...], w_ref[...])
    z_ref[...] = acc

    @pl.when(j >= KV_BLK0)
    def _():
        kv_ref[...] = acc.astype(BF16)


def _inproj(x2d, g, w_pack):
    n = x2d.shape[0]
    tm = min(1024, n)
    grid = (n // tm, D_PACK // PROJ_TN)
    return pl.pallas_call(
        _inproj_kernel,
        grid=grid,
        in_specs=[
            pl.BlockSpec((tm, D_MODEL), lambda i, j: (i, 0)),
            pl.BlockSpec((1, D_MODEL), lambda i, j: (0, 0)),
            pl.BlockSpec((D_MODEL, PROJ_TN), lambda i, j: (0, j)),
        ],
        out_specs=[
            pl.BlockSpec((tm, PROJ_TN), lambda i, j: (i, j)),
            pl.BlockSpec((tm, PROJ_TN), lambda i, j: (i, jnp.maximum(j - KV_BLK0, 0))),
        ],
        out_shape=[jax.ShapeDtypeStruct((n, D_PACK), F32),
                   jax.ShapeDtypeStruct((n, KV_PACK), BF16)],
        scratch_shapes=[pltpu.VMEM((tm, D_MODEL), BF16)],
        compiler_params=_cparams(2),
        name="inproj",
    )(x2d, g, w_pack)


def _dsa_kernel(*refs, has_past, qb, n_past_tiles, n_new_tiles, t_valid, past_len, topk):
    if has_past:
        (qa_ref, qi_ref, sm_ref, kip_ref, kp_ref, vp_ref, kin_ref, kn_ref, vn_ref,
         o_ref, keys_ref, m_ref, l_ref, acc_ref) = refs
    else:
        (qa_ref, qi_ref, sm_ref, kin_ref, kn_ref, vn_ref,
         o_ref, keys_ref, m_ref, l_ref, acc_ref) = refs
    kt = DSA_KT
    i = pl.program_id(1)
    q0 = past_len + i * qb
    j_end = ((q0 + qb - 1) // CHUNK + 1) * CHUNK - past_len
    nk_new = jnp.minimum((j_end + kt - 1) // kt, n_new_tiles)
    n_tiles = n_past_tiles + nk_new
    qchunk = (q0 + lax.broadcasted_iota(I32, (qb, 1), 0)) >> CHUNK_SHIFT
    lane = lax.broadcasted_iota(I32, (qb, kt), 1)

    wsc = sm_ref[0][:, SM_WI:SM_WI + IDX_HEADS] * (IDX_HEADS ** -0.5 * IDX_DIM ** -0.5)
    wcols = [wsc[:, h:h + 1] for h in range(IDX_HEADS)]

    def score_tile(ki_t, kpos0, jvalid0, col):
        acc = jnp.zeros((qb, kt), F32)
        for h in range(IDX_HEADS):
            d = _dot_nt(qi_ref[0, 0, h], ki_t)
            acc = acc + wcols[h] * jnp.maximum(d, 0.0)
        bits = lax.bitcast_convert_type(acc + 0.0, I32)
        key = bits ^ ((bits >> 31) & np.int32(0x7FFFFFFF))
        adm = (((kpos0 + lane) >> CHUNK_SHIFT) <= qchunk) & (jvalid0 + lane < t_valid)
        keys_ref[col] = jnp.where(adm, key, INT_MIN)

    if has_past:
        def p1_past(t, c):
            ki_t = kip_ref[0, _tile_rows(t, kt), :].astype(BF16)
            score_tile(ki_t, t * kt, -(2 ** 30), t)
            return c
        lax.fori_loop(0, n_past_tiles, p1_past, 0)

    def p1_new(t, c):
        ki_t = kin_ref[0, _tile_rows(t, kt), 0:IDX_DIM]
        score_tile(ki_t, past_len + t * kt, t * kt, n_past_tiles + t)
        return c
    lax.fori_loop(0, nk_new, p1_new, 0)

    def count(fn):
        def body(t, part):
            c = jnp.where(fn(keys_ref[t], t), 1.0, 0.0)
            for s in range(kt // LANES):
                part = part + c[:, s * LANES:(s + 1) * LANES]
            return part
        part = lax.fori_loop(0, n_tiles, body, jnp.zeros((qb, LANES), F32))
        return jnp.sum(part, axis=1, keepdims=True)

    kf = float(topk)

    def sbody(it, tu):
        cand = tu | lax.shift_left(np.int32(1), jnp.asarray(31 - it, I32))
        cs = cand ^ INT_MIN
        cnt = count(lambda k, t: k >= cs)
        return jnp.where(cnt >= kf, cand, tu)
    tu = lax.fori_loop(0, 32, sbody, jnp.zeros((qb, 1), I32))
    thr = tu ^ INT_MIN
    cnt_ge = count(lambda k, t: k >= thr)
    cnt_gt = count(lambda k, t: k > thr)
    tie = (cnt_ge > kf) & (thr > INT_MIN)
    need = kf - cnt_gt

    @pl.when(jnp.max(jnp.where(tie, 1.0, 0.0)) > 0.0)
    def _():
        def gidx(t):
            return t * kt + lane

        def jbody(it, a):
            cand = a | lax.shift_left(np.int32(1), jnp.asarray(15 - it, I32))
            cnt = count(lambda k, t: (k == thr) & (gidx(t) < cand))
            return jnp.where(cnt < need, cand, a)
        a = lax.fori_loop(0, 16, jbody, jnp.zeros((qb, 1), I32))

        def drop(t, c):
            k = keys_ref[t]
            keys_ref[t] = jnp.where(tie & (k == thr) & (gidx(t) > a), INT_MIN, k)
            return c
        lax.fori_loop(0, n_tiles, drop, 0)

    thr_eff = jnp.maximum(thr, INT_MIN + 1)

    q = qa_ref[0]
    q2 = []
    for g in range(N_KV_HEADS):
        a = q[:, (2 * g) * HEAD_DIM:(2 * g + 1) * HEAD_DIM]
        b = q[:, (2 * g + 1) * HEAD_DIM:(2 * g + 2) * HEAD_DIM]
        q2.append(jnp.concatenate([a, b], axis=0).astype(BF16))
    m_ref[...] = jnp.full(m_ref.shape, -jnp.inf, F32)
    l_ref[...] = jnp.zeros(l_ref.shape, F32)
    acc_ref[...] = jnp.zeros(acc_ref.shape, F32)
    scale = HEAD_DIM ** -0.5

    def attend(col, k_fn, v_fn):
        bias = jnp.where(keys_ref[col] >= thr_eff, 0.0, NEG_BIG)
        bias2 = jnp.concatenate([bias, bias], axis=0)
        for g in range(N_KV_HEADS):
            s = _dot_nt(q2[g], k_fn(g)) * scale + bias2
            m_old = m_ref[g]
            m_new = jnp.maximum(m_old, jnp.max(s, axis=1, keepdims=True))
            alpha = jnp.exp(m_old - m_new)
            p = jnp.exp(s - m_new)
            l_ref[g] = alpha * l_ref[g] + jnp.sum(p, axis=1, keepdims=True)
            acc_ref[g] = alpha * acc_ref[g] + _dot(p.astype(BF16), v_fn(g))
            m_ref[g] = m_new

    def hs(g):
        return slice(g * HEAD_DIM, (g + 1) * HEAD_DIM)

    if has_past:
        def p3_past(t, c):
            rows = _tile_rows(t, kt)
            attend(t, lambda g: kp_ref[0, rows, hs(g)].astype(BF16),
                   lambda g: vp_ref[0, rows, hs(g)].astype(BF16))
            return c
        lax.fori_loop(0, n_past_tiles, p3_past, 0)

    def p3_new(t, c):
        rows = _tile_rows(t, kt)
        attend(n_past_tiles + t, lambda g: kn_ref[0, rows, hs(g)], lambda g: vn_ref[0, rows, hs(g)])
        return c
    lax.fori_loop(0, nk_new, p3_new, 0)

    for g in range(N_KV_HEADS):
        o = acc_ref[g] / l_ref[g]
        o_ref[0, :, (2 * g) * HEAD_DIM:(2 * g + 1) * HEAD_DIM] = o[0:qb].astype(BF16)
        o_ref[0, :, (2 * g + 1) * HEAD_DIM:(2 * g + 2) * HEAD_DIM] = o[qb:2 * qb].astype(BF16)


def _dsa(z3, kv3, past, qb):
    bsz, t, _ = z3.shape
    kt = DSA_KT
    nb = t // qb
    has_past = past is not None
    past_len = past[0].shape[1] if has_past else 0
    topk = min(TOPK_MAX, (past_len + t) // 4)
    t_pad = -(-t // kt) * kt
    if t_pad != t:
        kv3 = jnp.pad(kv3, ((0, 0), (0, t_pad - t), (0, 0)))
    n_new_tiles = t_pad // kt
    n_past_tiles = past_len // kt
    qi = z3[:, :, OFF_QI:OFF_QI + IDX_HEADS * IDX_DIM].astype(BF16)
    qi = qi.reshape(bsz, nb, qb, IDX_HEADS, IDX_DIM).transpose(0, 1, 3, 2, 4)

    in_specs = [
        pl.BlockSpec((1, qb, ATT_WIDTH), lambda b, i: (b, i, OFF_QA // ATT_WIDTH)),
        pl.BlockSpec((1, 1, IDX_HEADS, qb, IDX_DIM), lambda b, i: (b, i, 0, 0, 0)),
        pl.BlockSpec((1, qb, LANES), lambda b, i: (b, i, OFF_SM // LANES)),
    ]
    args = [z3, qi, z3]
    if has_past:
        pk, pv, pki = past
        in_specs += [
            pl.BlockSpec((1, past_len, IDX_DIM), lambda b, i: (b, 0, 0)),
            pl.BlockSpec((1, past_len, KV_WIDTH), lambda b, i: (b, 0, 0)),
            pl.BlockSpec((1, past_len, KV_WIDTH), lambda b, i: (b, 0, 0)),
        ]
        args += [pki, pk, pv]
    in_specs += [
        pl.BlockSpec((1, t_pad, LANES), lambda b, i: (b, 0, 2 * KV_WIDTH // LANES)),
        pl.BlockSpec((1, t_pad, KV_WIDTH), lambda b, i: (b, 0, 0)),
        pl.BlockSpec((1, t_pad, KV_WIDTH), lambda b, i: (b, 0, 1)),
    ]
    args += [kv3, kv3, kv3]
    kern = functools.partial(_dsa_kernel, has_past=has_past, qb=qb, n_past_tiles=n_past_tiles,
                             n_new_tiles=n_new_tiles, t_valid=t, past_len=past_len, topk=topk)
    return pl.pallas_call(
        kern,
        grid=(bsz, nb),
        in_specs=in_specs,
        out_specs=pl.BlockSpec((1, qb, ATT_WIDTH), lambda b, i: (b, i, 0)),
        out_shape=jax.ShapeDtypeStruct((bsz, t, ATT_WIDTH), BF16),
        scratch_shapes=[
            pltpu.VMEM((n_past_tiles + n_new_tiles, qb, kt), I32),
            pltpu.VMEM((N_KV_HEADS, 2 * qb, 1), F32),
            pltpu.VMEM((N_KV_HEADS, 2 * qb, 1), F32),
            pltpu.VMEM((N_KV_HEADS, 2 * qb, HEAD_DIM), F32),
        ],
        compiler_params=_cparams(2),
        name="dsa_past" if has_past else "dsa",
    )(*args)


def _mlstm_kernel(qk_ref, vm_ref, og_ref, sm_ref, gt_ref, cprev_ref, c0_ref, n0_ref, m0_ref,
                  cw_ref, cb_ref, gbrow_ref, gbcol_ref, gain_ref,
                  hm_ref, cnew_ref, cout_ref, nout_ref, mout_ref,
                  ubuf, c_s, n_s, m_s, *, ln):
    c = pl.program_id(1)
    nc = pl.num_programs(1)
    pad = SUBLANES

    @pl.when(c == 0)
    def _():
        ubuf[0:pad] = cprev_ref[0]
        c_s[...] = c0_ref[0]
        n_s[...] = n0_ref[0]
        m_s[...] = m0_ref[0]

    @pl.when(c > 0)
    def _():
        ubuf[0:pad] = ubuf[ln:ln + pad]

    ubuf[pad:pad + ln] = qk_ref[0]
    y = cb_ref[...] + ubuf[pad - 3:pad - 3 + ln] * cw_ref[0:1]
    for j in range(1, CONV_W):
        y = y + ubuf[pad - 3 + j:pad - 3 + j + ln] * cw_ref[j:j + 1]
    qkc = y * _sigmoid(y)
    cnew_ref[0] = ubuf[ln:ln + pad]

    pre_c = sm_ref[0] + gbrow_ref[...]
    pre_r = gt_ref[0, 0] + gbcol_ref[...][:, 0:1]

    def log_sigmoid(x):
        return jnp.minimum(x, 0.0) - jnp.log(1.0 + jnp.exp(-jnp.abs(x)))

    lf_c = log_sigmoid(pre_c)
    lf_r = log_sigmoid(pre_r)
    ri = lax.broadcasted_iota(I32, (ln, ln), 0)
    ci = lax.broadcasted_iota(I32, (ln, ln), 1)
    causal = ci <= ri
    tril = jnp.where(causal, 1.0, 0.0).astype(BF16)
    triu = jnp.where(ri <= ci, 1.0, 0.0).astype(BF16)
    b_c = sum(_dot(tril, p) for p in _split3(lf_c))
    b_r = sum(_dot(p, triu) for p in _split3(lf_r))

    vm = vm_ref[0]
    og = og_ref[0]
    for h in range(M_HEADS):
        hs = slice(h * M_HEAD_DIM, (h + 1) * M_HEAD_DIM)
        qf = qkc[:, hs]
        kf = qkc[:, M_WIDTH + h * M_HEAD_DIM:M_WIDTH + (h + 1) * M_HEAD_DIM] * (M_HEAD_DIM ** -0.5)
        qb16 = qf.astype(BF16)
        kb16 = kf.astype(BF16)
        vb16 = vm[:, hs].astype(BF16)
        bcol = b_c[:, SM_FG + h:SM_FG + h + 1]
        igcol = pre_c[:, SM_IG + h:SM_IG + h + 1]
        brow = b_r[M_HEADS + h:M_HEADS + h + 1, :]
        igrow = pre_r[h:h + 1, :]
        blast = bcol[ln - 1:ln, :]
        m_prev = m_s[h:h + 1, 0:1]
        dmat = jnp.where(causal, bcol - brow + igrow, -jnp.inf)
        m_inter = bcol + m_prev
        m_i = jnp.maximum(m_inter, jnp.max(dmat, axis=1, keepdims=True))
        s = _dot_nt(qb16, kb16) * jnp.exp(dmat - m_i)
        scale = jnp.exp(m_inter - m_i)
        c_prev = c_s[h]
        n_prev = n_s[h:h + 1, :]
        num = _dot(s.astype(BF16), vb16) + scale * _dot(qb16, c_prev.astype(BF16))
        den = jnp.sum(s, axis=1, keepdims=True) + scale * jnp.sum(qf * n_prev, axis=1, keepdims=True)
        hh = num / jnp.maximum(jnp.abs(den), jnp.exp(-m_i))
        m_new = m_i[ln - 1:ln, :]
        decay = jnp.exp(blast + m_prev - m_new)
        wcol = jnp.exp(blast - bcol + igcol - m_new)
        kw = kf * wcol
        c_s[h] = decay * c_prev + _dot_tn(kw.astype(BF16), vb16)
        n_s[h:h + 1, :] = decay * n_prev + jnp.sum(kw, axis=0, keepdims=True)
        m_s[h:h + 1, :] = jnp.broadcast_to(m_new, (1, LANES))
        hn = hh * lax.rsqrt(jnp.mean(hh * hh, axis=1, keepdims=True) + EPS) * gain_ref[:, hs]
        hm_ref[0, :, hs] = (hn * _sigmoid(og[:, hs])).astype(BF16)

    @pl.when(c == nc - 1)
    def _():
        cout_ref[0] = c_s[...]
        nout_ref[0] = n_s[...]
        mout_ref[0] = m_s[...]


def _mlstm(z3, conv_prev, c0, n0, m0, conv_w, conv_b, b_igate, b_fgate, mh_gain, ln):
    bsz, t, _ = z3.shape
    nc = t // ln
    pad = SUBLANES
    gt = z3[:, :, OFF_SM + SM_IG:OFF_SM + SM_IG + 2 * M_HEADS]
    gt = gt.reshape(bsz, nc, ln, 2 * M_HEADS).transpose(0, 1, 3, 2)
    cprev = jnp.pad(conv_prev.astype(F32), ((0, 0), (pad - (CONV_W - 1), 0), (0, 0)))
    gbias = jnp.concatenate([b_igate, b_fgate]).astype(F32)
    gbrow = jnp.zeros((1, LANES), F32).at[0, SM_IG:SM_IG + 2 * M_HEADS].set(gbias)
    gbcol = jnp.broadcast_to(gbias[:, None], (2 * M_HEADS, LANES))
    m0b = jnp.broadcast_to(m0.astype(F32)[:, :, None], (bsz, M_HEADS, LANES))
    kern = functools.partial(_mlstm_kernel, ln=ln)
    full = lambda shape: pl.BlockSpec(shape, lambda b, c: (0,) * len(shape))
    hm, cnew, cout, nout, mout = pl.pallas_call(
        kern,
        grid=(bsz, nc),
        in_specs=[
            pl.BlockSpec((1, ln, 2 * M_WIDTH), lambda b, c: (b, c, OFF_QM // (2 * M_WIDTH))),
            pl.BlockSpec((1, ln, M_WIDTH), lambda b, c: (b, c, OFF_VM // M_WIDTH)),
            pl.BlockSpec((1, ln, M_WIDTH), lambda b, c: (b, c, OFF_OG // M_WIDTH)),
            pl.BlockSpec((1, ln, LANES), lambda b, c: (b, c, OFF_SM // LANES)),
            pl.BlockSpec((1, 1, 2 * M_HEADS, ln), lambda b, c: (b, c, 0, 0)),
            pl.BlockSpec((1, pad, 2 * M_WIDTH), lambda b, c: (b, 0, 0)),
            pl.BlockSpec((1, M_HEADS, M_HEAD_DIM, M_HEAD_DIM), lambda b, c: (b, 0, 0, 0)),
            pl.BlockSpec((1, M_HEADS, M_HEAD_DIM), lambda b, c: (b, 0, 0)),
            pl.BlockSpec((1, M_HEADS, LANES), lambda b, c: (b, 0, 0)),
            full((CONV_W, 2 * M_WIDTH)),
            full((1, 2 * M_WIDTH)),
            full((1, LANES)),
            full((2 * M_HEADS, LANES)),
            full((1, M_WIDTH)),
        ],
        out_specs=[
            pl.BlockSpec((1, ln, M_WIDTH), lambda b, c: (b, c, 0)),
            pl.BlockSpec((1, pad, 2 * M_WIDTH), lambda b, c: (b, 0, 0)),
            pl.BlockSpec((1, M_HEADS, M_HEAD_DIM, M_HEAD_DIM), lambda b, c: (b, 0, 0, 0)),
            pl.BlockSpec((1, M_HEADS, M_HEAD_DIM), lambda b, c: (b, 0, 0)),
            pl.BlockSpec((1, M_HEADS, LANES), lambda b, c: (b, 0, 0)),
        ],
        out_shape=[
            jax.ShapeDtypeStruct((bsz, t, M_WIDTH), BF16),
            jax.ShapeDtypeStruct((bsz, pad, 2 * M_WIDTH), F32),
            jax.ShapeDtypeStruct((bsz, M_HEADS, M_HEAD_DIM, M_HEAD_DIM), F32),
            jax.ShapeDtypeStruct((bsz, M_HEADS, M_HEAD_DIM), F32),
            jax.ShapeDtypeStruct((bsz, M_HEADS, LANES), F32),
        ],
        scratch_shapes=[
            pltpu.VMEM((ln + pad, 2 * M_WIDTH), F32),
            pltpu.VMEM((M_HEADS, M_HEAD_DIM, M_HEAD_DIM), F32),
            pltpu.VMEM((M_HEADS, M_HEAD_DIM), F32),
            pltpu.VMEM((M_HEADS, LANES), F32),
        ],
        compiler_params=_cparams(2),
        name="mlstm",
    )(z3, z3, z3, z3, gt, cprev, c0.astype(F32), n0.astype(F32), m0b,
      conv_w.astype(F32), conv_b.astype(F32)[None, :], gbrow, gbcol, mh_gain.astype(F32)[None, :])
    return hm, cnew[:, pad - (CONV_W - 1):, :], cout, nout, mout[:, :, 0]


def _post_kernel(att_ref, hm_ref, ga_ref, gb_ref, x_ref, wa_ref, wb_ref, wo_ref, gf_ref, wr_ref, br_ref,
                 x1_ref, h2_ref, ri_ref):
    ya = _dot(att_ref[...], wa_ref[...])
    yb = _dot(hm_ref[...], wb_ref[...])
    mixed = _sigmoid(ga_ref[...]) * ya + _sigmoid(gb_ref[...]) * yb
    x1 = x_ref[...] + _dot(mixed.astype(BF16), wo_ref[...])
    x1_ref[...] = x1
    h2 = x1 * lax.rsqrt(jnp.mean(x1 * x1, axis=-1, keepdims=True) + EPS) * gf_ref[...]
    h2_ref[...] = h2
    lg = jnp.dot(h2, wr_ref[...], preferred_element_type=F32, precision=lax.Precision.HIGHEST) + br_ref[...]
    lane = lax.broadcasted_iota(I32, lg.shape, 1)
    glog = jnp.where(lane < N_GROUPS, lg, -jnp.inf)
    gmax = jnp.max(glog, axis=1, keepdims=True)
    gstar = jnp.min(jnp.where(glog == gmax, lane, LANES), axis=1, keepdims=True)
    pg = 1.0 / jnp.sum(jnp.exp(glog - gmax), axis=1, keepdims=True)
    lo = N_GROUPS + EXP_PER_GROUP * gstar
    elog = jnp.where((lane >= lo) & (lane < lo + EXP_PER_GROUP), lg, -jnp.inf)
    v0 = jnp.max(elog, axis=1, keepdims=True)
    i0 = jnp.min(jnp.where(elog == v0, lane, LANES), axis=1, keepdims=True)
    elog2 = jnp.where(lane == i0, -jnp.inf, elog)
    v1 = jnp.max(elog2, axis=1, keepdims=True)
    i1 = jnp.min(jnp.where(elog2 == v1, lane, LANES), axis=1, keepdims=True)
    e1 = jnp.exp(v1 - v0)
    w0 = pg / (1.0 + e1)
    w1 = pg * e1 / (1.0 + e1)
    out = jnp.where(lane == 0, (i0 - N_GROUPS).astype(F32), 0.0)
    out = jnp.where(lane == 1, (i1 - N_GROUPS).astype(F32), out)
    out = jnp.where(lane == 2, w0, out)
    out = jnp.where(lane == 3, w1, out)
    ri_ref[...] = out


def _post(att2, hm2, z2, x2, wa, wb, wo, g_ffn, wr, br):
    n = x2.shape[0]
    tm = TOK_TM
    row = lambda w: pl.BlockSpec((tm, w), lambda i: (i, 0))
    full = lambda shape: pl.BlockSpec(shape, lambda i: (0,) * len(shape))
    return pl.pallas_call(
        _post_kernel,
        grid=(n // tm,),
        in_specs=[
            row(ATT_WIDTH), row(M_WIDTH),
            pl.BlockSpec((tm, D_MODEL), lambda i: (i, OFF_GA // D_MODEL)),
            pl.BlockSpec((tm, D_MODEL), lambda i: (i, OFF_GB // D_MODEL)),
            row(D_MODEL),
            full((ATT_WIDTH, D_MODEL)), full((M_WIDTH, D_MODEL)), full((D_MODEL, D_MODEL)),
            full((1, D_MODEL)), full((D_MODEL, LANES)), full((1, LANES)),
        ],
        out_specs=[row(D_MODEL), row(D_MODEL), row(LANES)],
        out_shape=[jax.ShapeDtypeStruct((n, D_MODEL), F32),
                   jax.ShapeDtypeStruct((n, D_MODEL), F32),
                   jax.ShapeDtypeStruct((n, LANES), F32)],
        compiler_params=_cparams(1),
        name="post",
    )(att2, hm2, z2, z2, x2, wa, wb, wo, g_ffn, wr, br)


def _row_gather_start(idx_ref, n_rows, src_hbm, dst_buf, sem):
    def body(r, c):
        tok = idx_ref[0, 0, r]
        pltpu.make_async_copy(src_hbm.at[pl.ds(tok, 1)], dst_buf.at[pl.ds(r, 1)], sem).start()
        return c
    lax.fori_loop(0, n_rows, body, 0)


def _row_gather_wait(n_rows, src_hbm, dst_buf, sem):
    pltpu.make_async_copy(src_hbm.at[pl.ds(0, n_rows)], dst_buf, sem).wait()


def _moe_kernel(te_ref, nu_ref, tok_ref, tokn_ref, h2_hbm, w1_ref, w3_ref, w2_ref, y_ref, xbuf, sem):
    t = pl.program_id(0)
    n_used = nu_ref[0]
    slot = t % 2
    tm = MOE_TM

    @pl.when(t == 0)
    def _():
        _row_gather_start(tok_ref, tm, h2_hbm, xbuf.at[0], sem.at[0])

    @pl.when(t + 1 < n_used)
    def _():
        _row_gather_start(tokn_ref, tm, h2_hbm, xbuf.at[1 - slot], sem.at[1 - slot])

    @pl.when(t < n_used)
    def _():
        _row_gather_wait(tm, h2_hbm, xbuf.at[slot], sem.at[slot])
        xb = xbuf[slot].astype(BF16)
        u = _dot(xb, w1_ref[0])
        a = (u * _sigmoid(u)) * _dot(xb, w3_ref[0])
        y_ref[...] = _dot(a.astype(BF16), w2_ref[0])

    @pl.when(t >= n_used)
    def _():
        y_ref[...] = jnp.zeros(y_ref.shape, F32)


def _moe(h2, tile_expert, n_used, row_token, w1b, w3b, w2b):
    n_tiles = tile_expert.shape[0]
    tm = MOE_TM
    tok3 = row_token.reshape(n_tiles, 1, tm)
    grid_spec = pltpu.PrefetchScalarGridSpec(
        num_scalar_prefetch=2,
        grid=(n_tiles,),
        in_specs=[
            pl.BlockSpec((1, 1, tm), lambda t, te, nu: (t, 0, 0), memory_space=pltpu.SMEM),
            pl.BlockSpec((1, 1, tm), lambda t, te, nu: (jnp.minimum(t + 1, n_tiles - 1), 0, 0),
                         memory_space=pltpu.SMEM),
            pl.BlockSpec(memory_space=pl.ANY),
            pl.BlockSpec((1, D_MODEL, D_EXPERT), lambda t, te, nu: (te[t], 0, 0)),
            pl.BlockSpec((1, D_MODEL, D_EXPERT), lambda t, te, nu: (te[t], 0, 0)),
            pl.BlockSpec((1, D_EXPERT, D_MODEL), lambda t, te, nu: (te[t], 0, 0)),
        ],
        out_specs=pl.BlockSpec((tm, D_MODEL), lambda t, te, nu: (t, 0)),
        scratch_shapes=[pltpu.VMEM((2, tm, D_MODEL), F32), pltpu.SemaphoreType.DMA((2,))],
    )
    return pl.pallas_call(
        _moe_kernel,
        grid_spec=grid_spec,
        out_shape=jax.ShapeDtypeStruct((n_tiles * tm, D_MODEL), F32),
        compiler_params=_cparams(1),
        name="moe",
    )(tile_expert, n_used, tok3, tok3, h2, w1b, w3b, w2b)


def _route(rinfo, n_tiles):
    n = rinfo.shape[0]
    tm = MOE_TM
    eid = rinfo[:, 0:2].astype(I32).reshape(-1)
    order = jnp.argsort(eid, stable=True)
    counts = jnp.sum(jax.nn.one_hot(eid, N_EXPERTS, dtype=I32), axis=0)
    tiles_e = (counts + tm - 1) // tm
    tile_end = jnp.cumsum(tiles_e)
    row_start = (tile_end - tiles_e) * tm
    grp_start = jnp.cumsum(counts) - counts
    e_sorted = eid[order]
    dest_sorted = row_start[e_sorted] + jnp.arange(2 * n, dtype=I32) - grp_start[e_sorted]
    dest = jnp.zeros((2 * n,), I32).at[order].set(dest_sorted)
    row_token = jnp.zeros((n_tiles * tm,), I32).at[dest].set(jnp.arange(2 * n, dtype=I32) // 2)
    n_used = tile_end[-1]
    tile_expert = jnp.searchsorted(tile_end, jnp.arange(n_tiles, dtype=I32), side="right").astype(I32)
    last_e = jnp.searchsorted(tile_end, n_used - 1, side="right").astype(I32)
    tile_expert = jnp.minimum(tile_expert, last_e)
    pos = dest.reshape(n, 2)
    return tile_expert, n_used.reshape(1).astype(I32), row_token, pos


def _final_kernel(pos_ref, posn_ref, y_hbm, x1_ref, ri_ref, pe_ref, gp_ref, wpg_ref, wp_ref, gfin_ref,
                  o_ref, ybuf, sem):
    t = pl.program_id(0)
    nt = pl.num_programs(0)
    slot = t % 2
    tm = TOK_TM

    @pl.when(t == 0)
    def _():
        _row_gather_start(pos_ref, 2 * tm, y_hbm, ybuf.at[0], sem.at[0])

    @pl.when(t + 1 < nt)
    def _():
        _row_gather_start(posn_ref, 2 * tm, y_hbm, ybuf.at[1 - slot], sem.at[1 - slot])

    _row_gather_wait(2 * tm, y_hbm, ybuf.at[slot], sem.at[slot])
    ri = ri_ref[...]
    moe = ri[:, 2:3] * ybuf[slot, 0:tm] + ri[:, 3:4] * ybuf[slot, tm:2 * tm]
    x2 = x1_ref[...] + moe
    hp = x2 * lax.rsqrt(jnp.mean(x2 * x2, axis=-1, keepdims=True) + EPS) * gp_ref[...]
    gate = _sigmoid(_dot(hp.astype(BF16), wpg_ref[...]))
    x3 = x2 + gate * _dot(pe_ref[...].astype(BF16), wp_ref[...])
    o_ref[...] = x3 * lax.rsqrt(jnp.mean(x3 * x3, axis=-1, keepdims=True) + EPS) * gfin_ref[...]


def _final(x1, rinfo, pos, y_sorted, pe2, g_ple, wpg, wp, g_final):
    n = x1.shape[0]
    tm = TOK_TM
    nt = n // tm
    pos3 = pos.reshape(nt, tm, 2).transpose(0, 2, 1).reshape(nt, 1, 2 * tm)
    row = lambda w: pl.BlockSpec((tm, w), lambda i: (i, 0))
    full = lambda shape: pl.BlockSpec(shape, lambda i: (0,) * len(shape))
    return pl.pallas_call(
        _final_kernel,
        grid=(nt,),
        in_specs=[
            pl.BlockSpec((1, 1, 2 * tm), lambda i: (i, 0, 0), memory_space=pltpu.SMEM),
            pl.BlockSpec((1, 1, 2 * tm), lambda i: (jnp.minimum(i + 1, nt - 1), 0, 0),
                         memory_space=pltpu.SMEM),
            pl.BlockSpec(memory_space=pl.ANY),
            row(D_MODEL), row(LANES), row(PLE_DIM),
            full((1, D_MODEL)), full((D_MODEL, D_MODEL)), full((PLE_DIM, D_MODEL)), full((1, D_MODEL)),
        ],
        out_specs=row(D_MODEL),
        out_shape=jax.ShapeDtypeStruct((n, D_MODEL), F32),
        scratch_shapes=[pltpu.VMEM((2, 2 * tm, D_MODEL), F32), pltpu.SemaphoreType.DMA((2,))],
        compiler_params=_cparams(1),
        name="final",
    )(pos3, pos3, y_sorted, x1, rinfo, pe2, g_ple, wpg, wp, g_final)


def _pack_w_in(w):
    offs = np.concatenate([[0], np.cumsum(PROJ_SIZES)])
    qa, ka, va, qi, ki, wi, qm, km, vm, ig, fg, og, ga, gb = [w[:, offs[k]:offs[k + 1]] for k in range(14)]
    small = jnp.concatenate([ki, wi, ig, fg], axis=1)
    small = jnp.pad(small, ((0, 0), (0, SM_WIDTH - small.shape[1])))
    return jnp.concatenate([qa, qi, qm, km, vm, og, ga, gb, ka, va, small], axis=1).astype(BF16)


def _prep_weights(g_mix, w_in, w_att_out, w_mlstm_out, w_out, g_ffn, router_gw, router_gb, router_ew,
                  router_eb, w1, w3, w2, g_ple, w_ple, w_ple_gate, g_final):
    wr = jnp.concatenate([router_gw, router_ew], axis=1).astype(F32)
    wr = jnp.pad(wr, ((0, 0), (0, LANES - wr.shape[1])))
    br = jnp.concatenate([router_gb, router_eb]).astype(F32)
    br = jnp.pad(br, (0, LANES - br.shape[0]))[None, :]
    return dict(
        g_mix=g_mix.astype(F32)[None, :], w_pack=_pack_w_in(w_in),
        wa=w_att_out.astype(BF16), wb=w_mlstm_out.astype(BF16), wo=w_out.astype(BF16),
        g_ffn=g_ffn.astype(F32)[None, :], wr=wr, br=br,
        w1=w1.astype(BF16), w3=w3.astype(BF16), w2=w2.astype(BF16),
        g_ple=g_ple.astype(F32)[None, :], wp=w_ple.astype(BF16), wpg=w_ple_gate.astype(BF16),
        g_final=g_final.astype(F32)[None, :])


def _layer(x, pe, past, conv_prev, c0, n0, m0, wts, conv_w, conv_b, b_igate, b_fgate, mh_gain, qb, ln):
    bsz, t, _ = x.shape
    n = bsz * t
    x2 = x.reshape(n, D_MODEL)
    z, kvb = _inproj(x2, wts["g_mix"], wts["w_pack"])
    z3 = z.reshape(bsz, t, D_PACK)
    kv3 = kvb.reshape(bsz, t, KV_PACK)
    att = _dsa(z3, kv3, past, qb)
    hm, conv_new, c_new, n_new, m_new = _mlstm(z3, conv_prev, c0, n0, m0, conv_w, conv_b,
                                               b_igate, b_fgate, mh_gain, ln)
    x1, h2, rinfo = _post(att.reshape(n, ATT_WIDTH), hm.reshape(n, M_WIDTH), z, x2,
                          wts["wa"], wts["wb"], wts["wo"], wts["g_ffn"], wts["wr"], wts["br"])
    n_tiles = (2 * n + N_EXPERTS * (MOE_TM - 1) + MOE_TM - 1) // MOE_TM
    tile_expert, n_used, row_token, pos = _route(rinfo, n_tiles)
    y_sorted = _moe(h2, tile_expert, n_used, row_token, wts["w1"], wts["w3"], wts["w2"])
    y = _final(x1, rinfo, pos, y_sorted, pe.reshape(n, PLE_DIM).astype(F32),
               wts["g_ple"], wts["wpg"], wts["wp"], wts["g_final"])
    k_new = z3[:, :, OFF_KA:OFF_KA + KV_WIDTH].reshape(bsz, t, N_KV_HEADS, HEAD_DIM)
    v_new = z3[:, :, OFF_VA:OFF_VA + KV_WIDTH].reshape(bsz, t, N_KV_HEADS, HEAD_DIM)
    ki_new = z3[:, :, OFF_SM + SM_KI:OFF_SM + SM_KI + IDX_DIM]
    return y.reshape(bsz, t, D_MODEL), (k_new, v_new, ki_new, conv_new, c_new, n_new, m_new)


def kernel(x_prompt, x_sample, cache_k, cache_v, cache_kidx, state_conv, state_C, state_n, state_m,
           p_prompt, p_sample, g_mix, w_in, conv_w, conv_b, b_igate, b_fgate, mh_gain,
           w_att_out, w_mlstm_out, w_out, g_ffn, router_gw, router_gb, router_ew, router_eb,
           w1, w3, w2, g_ple, w_ple, w_ple_gate, g_final):
    assert g_mix.shape[0] == 1, "single-layer step"
    bp, tp, _ = x_prompt.shape
    bs, ts, _ = x_sample.shape
    sdt = state_C.dtype
    wts = _prep_weights(g_mix[0], w_in[0], w_att_out[0], w_mlstm_out[0], w_out[0], g_ffn[0],
                        router_gw[0], router_gb[0], router_ew[0], router_eb[0], w1[0], w3[0], w2[0],
                        g_ple[0], w_ple[0], w_ple_gate[0], g_final)
    mix = (conv_w[0], conv_b[0], b_igate[0], b_fgate[0], mh_gain[0])
    yp, st_p = _layer(
        x_prompt, p_prompt[0], None,
        jnp.zeros((bp, CONV_W - 1, 2 * M_WIDTH), F32),
        jnp.zeros((bp, M_HEADS, M_HEAD_DIM, M_HEAD_DIM), F32),
        jnp.zeros((bp, M_HEADS, M_HEAD_DIM), F32),
        jnp.zeros((bp, M_HEADS), F32),
        wts, *mix, qb=min(128, tp), ln=min(256, tp))
    plen = cache_k.shape[2]
    past = (cache_k[0].reshape(bs, plen, KV_WIDTH), cache_v[0].reshape(bs, plen, KV_WIDTH), cache_kidx[0])
    ys, st_s = _layer(
        x_sample, p_sample[0], past, state_conv[0], state_C[0], state_n[0], state_m[0],
        wts, *mix, qb=ts, ln=ts)
    outs_p = [s[None] for s in st_p]
    outs_s = [s[None] for s in st_s]
    for lst in (outs_p, outs_s):
        for k in (4, 5, 6):
            lst[k] = lst[k].astype(sdt)
    return (yp, ys, *outs_p, *outs_s)
```

```python
import functools

import numpy as np
import jax
import jax.numpy as jnp
from jax import lax
from jax.experimental import pallas as pl
from jax.experimental.pallas import tpu as pltpu

F32 = jnp.float32
BF16 = jnp.bfloat16
I32 = jnp.int32

D_MODEL = 2048
CHUNK = 64
CHUNK_SHIFT = 6
assert 1 << CHUNK_SHIFT == CHUNK
N_HEADS = 8
N_KV_HEADS = 4
HEAD_DIM = 128
ATT_WIDTH = N_HEADS * HEAD_DIM
KV_WIDTH = N_KV_HEADS * HEAD_DIM
IDX_HEADS = 16
IDX_DIM = 64
TOPK_MAX = 256
M_HEADS = 4
M_HEAD_DIM = 256
M_WIDTH = M_HEADS * M_HEAD_DIM
CONV_W = 4
N_GROUPS = 4
EXP_PER_GROUP = 8
N_EXPERTS = N_GROUPS * EXP_PER_GROUP
D_EXPERT = 512
PLE_DIM = 256
EPS = 1e-6
PROJ_SIZES = (ATT_WIDTH, KV_WIDTH, KV_WIDTH, IDX_HEADS * IDX_DIM, IDX_DIM, IDX_HEADS,
              M_WIDTH, M_WIDTH, M_WIDTH, M_HEADS, M_HEADS, M_WIDTH, D_MODEL, D_MODEL)

LANES = 128
SUBLANES = 8
VMEM_LIMIT = 56 * 1024 * 1024

OFF_QA, OFF_QI, OFF_QM, OFF_KM, OFF_VM, OFF_OG = 0, 1024, 2048, 3072, 4096, 5120
OFF_GA, OFF_GB, OFF_KA, OFF_VA, OFF_SM = 6144, 8192, 10240, 10752, 11264
SM_WIDTH = 512
D_PACK = OFF_SM + SM_WIDTH
SM_KI, SM_WI, SM_IG, SM_FG = 0, 64, 80, 84
PROJ_TN = 512
KV_BLK0 = OFF_KA // PROJ_TN
KV_PACK = D_PACK - OFF_KA

LOG2E = 1.4426950408889634
INT_MIN = np.int32(-2 ** 31)
NEG_BIG = -1e30

DSA_KT = 512
MOE_TM = 256
TOK_TM = 256


def _cparams(n_axes):
    return pltpu.CompilerParams(dimension_semantics=("arbitrary",) * n_axes,
                                vmem_limit_bytes=VMEM_LIMIT)


def _dot(a, b):
    return jnp.dot(a, b, preferred_element_type=F32)


def _dot_nt(a, b):
    return lax.dot_general(a, b, (((1,), (1,)), ((), ())), preferred_element_type=F32)


def _dot_tn(a, b):
    return lax.dot_general(a, b, (((0,), (0,)), ((), ())), preferred_element_type=F32)


def _tile_rows(t, size):
    start = t * size
    return pl.ds(start if isinstance(start, int) else pl.multiple_of(start, size), size)


def _sigmoid(x):
    return 1.0 / (1.0 + jnp.exp(-x))


def _split3(x):
    hi = x.astype(BF16)
    r1 = x - hi.astype(F32)
    mid = r1.astype(BF16)
    lo = (r1 - mid.astype(F32)).astype(BF16)
    return hi, mid, lo


def _inproj_kernel(x_ref, g_ref, w_ref, z_ref, kv_ref, k_ref, v_ref, ki_ref, h_ref):
    j = pl.program_id(1)

    @pl.when(j == 0)
    def _():
        x = x_ref[...]
        r = lax.rsqrt(jnp.mean(x * x, axis=-1, keepdims=True) + EPS)
        h_ref[...] = (x * r * g_ref[...]).astype(BF16)

    acc = _dot(h_ref[...], w_ref[...])
    z_ref[...] = acc

    @pl.when(j >= KV_BLK0)
    def _():
        kv_ref[...] = acc.astype(BF16)

    @pl.when(j == KV_BLK0)
    def _():
        k_ref[...] = acc

    @pl.when(j == KV_BLK0 + 1)
    def _():
        v_ref[...] = acc

    @pl.when(j == KV_BLK0 + 2)
    def _():
        ki_ref[...] = acc[:, SM_KI:SM_KI + IDX_DIM]


def _inproj(x2d, g, w_pack):
    n = x2d.shape[0]
    tm = min(1024, n)
    grid = (n // tm, D_PACK // PROJ_TN)
    return pl.pallas_call(
        _inproj_kernel,
        grid=grid,
        in_specs=[
            pl.BlockSpec((tm, D_MODEL), lambda i, j: (i, 0)),
            pl.BlockSpec((1, D_MODEL), lambda i, j: (0, 0)),
            pl.BlockSpec((D_MODEL, PROJ_TN), lambda i, j: (0, j)),
        ],
        out_specs=[
            pl.BlockSpec((tm, PROJ_TN), lambda i, j: (i, j)),
            pl.BlockSpec((tm, PROJ_TN), lambda i, j: (i, jnp.maximum(j - KV_BLK0, 0))),
            pl.BlockSpec((tm, KV_WIDTH), lambda i, j: (i, 0)),
            pl.BlockSpec((tm, KV_WIDTH), lambda i, j: (i, 0)),
            pl.BlockSpec((tm, IDX_DIM), lambda i, j: (i, 0)),
        ],
        out_shape=[jax.ShapeDtypeStruct((n, D_PACK), F32),
                   jax.ShapeDtypeStruct((n, KV_PACK), BF16),
                   jax.ShapeDtypeStruct((n, KV_WIDTH), F32),
                   jax.ShapeDtypeStruct((n, KV_WIDTH), F32),
                   jax.ShapeDtypeStruct((n, IDX_DIM), F32)],
        scratch_shapes=[pltpu.VMEM((tm, D_MODEL), BF16)],
        compiler_params=_cparams(2),
        name="inproj",
    )(x2d, g, w_pack)


def _dsa_kernel(*refs, has_past, qb, n_past_tiles, n_new_tiles, t_valid, past_len, topk):
    n_in = 9 if has_past else 6
    if has_past:
        qa_ref, qi_ref, sm_ref, kip_ref, kp_ref, vp_ref, kin_ref, kn_ref, vn_ref = refs[:n_in]
    else:
        qa_ref, qi_ref, sm_ref, kin_ref, kn_ref, vn_ref = refs[:n_in]
    o_ref, keys_ref, d_ref, s_ref = refs[n_in:n_in + 4]
    state = refs[n_in + 4:]
    mx_refs, acc_refs = state[0::2], state[1::2]
    kt = DSA_KT
    i = pl.program_id(1)
    q0 = past_len + i * qb
    j_end = ((q0 + qb - 1) // CHUNK + 1) * CHUNK - past_len
    nk_new = jnp.minimum((j_end + kt - 1) // kt, n_new_tiles)
    n_tiles = n_past_tiles + nk_new
    qchunk = (q0 + lax.broadcasted_iota(I32, (qb, 1), 0)) >> CHUNK_SHIFT
    lane = lax.broadcasted_iota(I32, (qb, kt), 1)

    wsc = sm_ref[0][:, SM_WI:SM_WI + IDX_HEADS] * (IDX_HEADS ** -0.5 * IDX_DIM ** -0.5)
    wcols = [wsc[:, h:h + 1] for h in range(IDX_HEADS)]

    qi_all = qi_ref[0, 0]

    def score_tile(ki_t, kpos0, jvalid0, col):
        d_ref[...] = _dot_nt(qi_all, ki_t)
        acc = wcols[0] * jnp.maximum(d_ref[0:qb], 0.0)
        for h in range(1, IDX_HEADS):
            acc = acc + wcols[h] * jnp.maximum(d_ref[h * qb:(h + 1) * qb], 0.0)
        bits = lax.bitcast_convert_type(acc + 0.0, I32)
        key = bits ^ ((bits >> 31) & np.int32(0x7FFFFFFF))
        adm = (((kpos0 + lane) >> CHUNK_SHIFT) <= qchunk) & (jvalid0 + lane < t_valid)
        keys_ref[col] = jnp.where(adm, key, INT_MIN)

    if has_past:
        def p1_past(t, c):
            ki_t = kip_ref[0, _tile_rows(t, kt), :].astype(BF16)
            score_tile(ki_t, t * kt, -(2 ** 30), t)
            return c
        lax.fori_loop(0, n_past_tiles, p1_past, 0)

    def p1_new(t, c):
        ki_t = kin_ref[0, _tile_rows(t, kt), 0:IDX_DIM]
        score_tile(ki_t, past_len + t * kt, t * kt, n_past_tiles + t)
        return c
    lax.fori_loop(0, nk_new, p1_new, 0)

    def count(fn):
        def body(t, part):
            kall = keys_ref[t]
            for s in range(kt // LANES):
                ks = kall[:, s * LANES:(s + 1) * LANES]
                part = part + jnp.where(fn(ks, t, s), 1.0, 0.0)
            return part
        part = lax.fori_loop(0, n_tiles, body, jnp.zeros((qb, LANES), F32))
        return jnp.sum(part, axis=1, keepdims=True)

    kf = float(topk)

    def sbody(it, tu):
        cand = tu | lax.shift_left(np.int32(1), jnp.asarray(31 - it, I32))
        cs = cand ^ INT_MIN
        cnt = count(lambda k, t, s: k >= cs)
        return jnp.where(cnt >= kf, cand, tu)
    tu = lax.fori_loop(0, 32, sbody, jnp.zeros((qb, 1), I32))
    thr = tu ^ INT_MIN
    cnt_ge = count(lambda k, t, s: k >= thr)
    cnt_gt = count(lambda k, t, s: k > thr)
    tie = (cnt_ge > kf) & (thr > INT_MIN)
    need = kf - cnt_gt

    @pl.when(jnp.max(jnp.where(tie, 1.0, 0.0)) > 0.0)
    def _():
        lane1 = lax.broadcasted_iota(I32, (qb, LANES), 1)

        def jbody(it, a):
            cand = a | lax.shift_left(np.int32(1), jnp.asarray(15 - it, I32))
            cnt = count(lambda k, t, s: (k == thr) & (t * kt + s * LANES + lane1 < cand))
            return jnp.where(cnt < need, cand, a)
        a = lax.fori_loop(0, 16, jbody, jnp.zeros((qb, 1), I32))

        def drop(t, c):
            k = keys_ref[t]
            keys_ref[t] = jnp.where(tie & (k == thr) & (t * kt + lane > a), INT_MIN, k)
            return c
        lax.fori_loop(0, n_tiles, drop, 0)

    thr_eff = jnp.maximum(thr, INT_MIN + 1)

    q = qa_ref[0] * (HEAD_DIM ** -0.5 * LOG2E)
    q2 = []
    for g in range(N_KV_HEADS):
        a = q[:, (2 * g) * HEAD_DIM:(2 * g + 1) * HEAD_DIM]
        b = q[:, (2 * g + 1) * HEAD_DIM:(2 * g + 2) * HEAD_DIM]
        q2.append(jnp.concatenate([a, b], axis=0).astype(BF16))
    for g in range(N_KV_HEADS):
        mx_refs[g][...] = jnp.full(mx_refs[g].shape, -jnp.inf, F32)
        acc_refs[g][...] = jnp.zeros(acc_refs[g].shape, F32)

    def hs(g):
        return slice(g * HEAD_DIM, (g + 1) * HEAD_DIM)

    def logits(col, k_fn):
        bias = jnp.where(keys_ref[col] >= thr_eff, 0.0, NEG_BIG)
        bias2 = jnp.concatenate([bias, bias], axis=0)
        for g in range(N_KV_HEADS):
            s = _dot_nt(q2[g], k_fn(g)) + bias2
            s_ref[g, col] = s
            mx = mx_refs[g][...]
            for c in range(kt // LANES):
                mx = jnp.maximum(mx, s[:, c * LANES:(c + 1) * LANES])
            mx_refs[g][...] = mx

    ones_blk = jnp.ones((kt, HEAD_DIM), BF16)

    def weighted(col, v_fn, m_rows):
        for g in range(N_KV_HEADS):
            p = jnp.exp2(s_ref[g, col] - m_rows[g]).astype(BF16)
            v_aug = jnp.concatenate([v_fn(g), ones_blk], axis=1)
            acc_refs[g][...] = acc_refs[g][...] + _dot(p, v_aug)

    if has_past:
        def pa_past(t, c):
            rows = _tile_rows(t, kt)
            logits(t, lambda g: kp_ref[0, rows, hs(g)].astype(BF16))
            return c
        lax.fori_loop(0, n_past_tiles, pa_past, 0)

    def pa_new(t, c):
        rows = _tile_rows(t, kt)
        logits(n_past_tiles + t, lambda g: kn_ref[0, rows, hs(g)])
        return c
    lax.fori_loop(0, nk_new, pa_new, 0)

    m_rows = [jnp.max(mx_refs[g][...], axis=1, keepdims=True) for g in range(N_KV_HEADS)]

    if has_past:
        def pb_past(t, c):
            rows = _tile_rows(t, kt)
            weighted(t, lambda g: vp_ref[0, rows, hs(g)].astype(BF16), m_rows)
            return c
        lax.fori_loop(0, n_past_tiles, pb_past, 0)

    def pb_new(t, c):
        rows = _tile_rows(t, kt)
        weighted(n_past_tiles + t, lambda g: vn_ref[0, rows, hs(g)], m_rows)
        return c
    lax.fori_loop(0, nk_new, pb_new, 0)

    for g in range(N_KV_HEADS):
        acc = acc_refs[g][...]
        o = acc[:, 0:HEAD_DIM] / acc[:, HEAD_DIM:2 * HEAD_DIM]
        o_ref[0, :, (2 * g) * HEAD_DIM:(2 * g + 1) * HEAD_DIM] = o[0:qb].astype(BF16)
        o_ref[0, :, (2 * g + 1) * HEAD_DIM:(2 * g + 2) * HEAD_DIM] = o[qb:2 * qb].astype(BF16)


def _dsa(z3, kv3, past, qb):
    bsz, t, _ = z3.shape
    kt = DSA_KT
    nb = t // qb
    has_past = past is not None
    past_len = past[0].shape[1] if has_past else 0
    topk = min(TOPK_MAX, (past_len + t) // 4)
    t_pad = -(-t // kt) * kt
    if t_pad != t:
        kv3 = jnp.pad(kv3, ((0, 0), (0, t_pad - t), (0, 0)))
    n_new_tiles = t_pad // kt
    n_past_tiles = past_len // kt
    qi = z3[:, :, OFF_QI:OFF_QI + IDX_HEADS * IDX_DIM].astype(BF16)
    qi = qi.reshape(bsz, nb, qb, IDX_HEADS, IDX_DIM).transpose(0, 1, 3, 2, 4)
    qi = qi.reshape(bsz, nb, IDX_HEADS * qb, IDX_DIM)

    in_specs = [
        pl.BlockSpec((1, qb, ATT_WIDTH), lambda b, i: (b, i, OFF_QA // ATT_WIDTH)),
        pl.BlockSpec((1, 1, IDX_HEADS * qb, IDX_DIM), lambda b, i: (b, i, 0, 0)),
        pl.BlockSpec((1, qb, LANES), lambda b, i: (b, i, OFF_SM // LANES)),
    ]
    args = [z3, qi, z3]
    if has_past:
        pk, pv, pki = past
        in_specs += [
            pl.BlockSpec((1, past_len, IDX_DIM), lambda b, i: (b, 0, 0)),
            pl.BlockSpec((1, past_len, KV_WIDTH), lambda b, i: (b, 0, 0)),
            pl.BlockSpec((1, past_len, KV_WIDTH), lambda b, i: (b, 0, 0)),
        ]
        args += [pki, pk, pv]
    in_specs += [
        pl.BlockSpec((1, t_pad, LANES), lambda b, i: (b, 0, 2 * KV_WIDTH // LANES)),
        pl.BlockSpec((1, t_pad, KV_WIDTH), lambda b, i: (b, 0, 0)),
        pl.BlockSpec((1, t_pad, KV_WIDTH), lambda b, i: (b, 0, 1)),
    ]
    args += [kv3, kv3, kv3]
    kern = functools.partial(_dsa_kernel, has_past=has_past, qb=qb, n_past_tiles=n_past_tiles,
                             n_new_tiles=n_new_tiles, t_valid=t, past_len=past_len, topk=topk)
    return pl.pallas_call(
        kern,
        grid=(bsz, nb),
        in_specs=in_specs,
        out_specs=pl.BlockSpec((1, qb, ATT_WIDTH), lambda b, i: (b, i, 0)),
        out_shape=jax.ShapeDtypeStruct((bsz, t, ATT_WIDTH), BF16),
        scratch_shapes=[pltpu.VMEM((n_past_tiles + n_new_tiles, qb, kt), I32),
                        pltpu.VMEM((IDX_HEADS * qb, kt), F32),
                        pltpu.VMEM((N_KV_HEADS, n_past_tiles + n_new_tiles, 2 * qb, kt), F32)] + [
            pltpu.VMEM((2 * qb, LANES), F32), pltpu.VMEM((2 * qb, 2 * HEAD_DIM), F32)] * N_KV_HEADS,
        compiler_params=_cparams(2),
        name="dsa_past" if has_past else "dsa",
    )(*args)


def _mlstm_kernel(qk_ref, vm_ref, og_ref, sm_ref, gt_ref, cprev_ref, c0_ref, n0_ref, m0_ref,
                  cw_ref, cb_ref, gbrow_ref, gbcol_ref, gain_ref,
                  hm_ref, cnew_ref, cout_ref, nout_ref, mout_ref,
                  ubuf, c_s, n_s, m_s, *, ln):
    c = pl.program_id(1)
    nc = pl.num_programs(1)
    pad = SUBLANES

    @pl.when(c == 0)
    def _():
        ubuf[0:pad] = cprev_ref[0]
        c_s[...] = c0_ref[0]
        n_s[...] = n0_ref[0]
        m_s[...] = m0_ref[0]

    @pl.when(c > 0)
    def _():
        ubuf[0:pad] = ubuf[ln:ln + pad]

    ubuf[pad:pad + ln] = qk_ref[0]
    y = cb_ref[...] + ubuf[pad - 3:pad - 3 + ln] * cw_ref[0:1]
    for j in range(1, CONV_W):
        y = y + ubuf[pad - 3 + j:pad - 3 + j + ln] * cw_ref[j:j + 1]
    qkc = y * _sigmoid(y)
    cnew_ref[0] = ubuf[ln:ln + pad]

    pre_c = sm_ref[0] + gbrow_ref[...]
    pre_r = gt_ref[0, 0] + gbcol_ref[...][:, 0:1]

    def log_sigmoid(x):
        return jnp.minimum(x, 0.0) - jnp.log(1.0 + jnp.exp(-jnp.abs(x)))

    lf_c = log_sigmoid(pre_c)
    lf_r = log_sigmoid(pre_r)
    ri = lax.broadcasted_iota(I32, (ln, ln), 0)
    ci = lax.broadcasted_iota(I32, (ln, ln), 1)
    causal = ci <= ri
    tril = jnp.where(causal, 1.0, 0.0).astype(BF16)
    triu = jnp.where(ri <= ci, 1.0, 0.0).astype(BF16)
    b_c = sum(_dot(tril, p) for p in _split3(lf_c))
    b_r = sum(_dot(p, triu) for p in _split3(lf_r))

    vm = vm_ref[0]
    og = og_ref[0]
    for h in range(M_HEADS):
        hs = slice(h * M_HEAD_DIM, (h + 1) * M_HEAD_DIM)
        qf = qkc[:, hs]
        kf = qkc[:, M_WIDTH + h * M_HEAD_DIM:M_WIDTH + (h + 1) * M_HEAD_DIM] * (M_HEAD_DIM ** -0.5)
        qb16 = qf.astype(BF16)
        kb16 = kf.astype(BF16)
        vb16 = vm[:, hs].astype(BF16)
        bcol = b_c[:, SM_FG + h:SM_FG + h + 1]
        igcol = pre_c[:, SM_IG + h:SM_IG + h + 1]
        brow = b_r[M_HEADS + h:M_HEADS + h + 1, :]
        igrow = pre_r[h:h + 1, :]
        blast = bcol[ln - 1:ln, :]
        m_prev = m_s[h:h + 1, 0:1]
        dmat = jnp.where(causal, bcol - brow + igrow, -jnp.inf)
        m_inter = bcol + m_prev
        m_i = jnp.maximum(m_inter, jnp.max(dmat, axis=1, keepdims=True))
        s = _dot_nt(qb16, kb16) * jnp.exp(dmat - m_i)
        scale = jnp.exp(m_inter - m_i)
        c_prev = c_s[h]
        n_prev = n_s[h:h + 1, :]
        num = _dot(s.astype(BF16), vb16) + scale * _dot(qb16, c_prev.astype(BF16))
        den = jnp.sum(s, axis=1, keepdims=True) + scale * jnp.sum(qf * n_prev, axis=1, keepdims=True)
        hh = num / jnp.maximum(jnp.abs(den), jnp.exp(-m_i))
        m_new = m_i[ln - 1:ln, :]
        decay = jnp.exp(blast + m_prev - m_new)
        wcol = jnp.exp(blast - bcol + igcol - m_new)
        kw = kf * wcol
        c_s[h] = decay * c_prev + _dot_tn(kw.astype(BF16), vb16)
        n_s[h:h + 1, :] = decay * n_prev + jnp.sum(kw, axis=0, keepdims=True)
        m_s[h:h + 1, :] = jnp.broadcast_to(m_new, (1, LANES))
        hn = hh * lax.rsqrt(jnp.mean(hh * hh, axis=1, keepdims=True) + EPS) * gain_ref[:, hs]
        hm_ref[0, :, hs] = (hn * _sigmoid(og[:, hs])).astype(BF16)

    @pl.when(c == nc - 1)
    def _():
        cout_ref[0] = c_s[...]
        nout_ref[0] = n_s[...]
        mout_ref[0] = m_s[...]


def _mlstm(z3, conv_prev, c0, n0, m0, conv_w, conv_b, b_igate, b_fgate, mh_gain, ln):
    bsz, t, _ = z3.shape
    nc = t // ln
    pad = SUBLANES
    gt = z3[:, :, OFF_SM + SM_IG:OFF_SM + SM_IG + 2 * M_HEADS]
    gt = gt.reshape(bsz, nc, ln, 2 * M_HEADS).transpose(0, 1, 3, 2)
    cprev = jnp.pad(conv_prev.astype(F32), ((0, 0), (pad - (CONV_W - 1), 0), (0, 0)))
    gbias = jnp.concatenate([b_igate, b_fgate]).astype(F32)
    gbrow = jnp.zeros((1, LANES), F32).at[0, SM_IG:SM_IG + 2 * M_HEADS].set(gbias)
    gbcol = jnp.broadcast_to(gbias[:, None], (2 * M_HEADS, LANES))
    m0b = jnp.broadcast_to(m0.astype(F32)[:, :, None], (bsz, M_HEADS, LANES))
    kern = functools.partial(_mlstm_kernel, ln=ln)
    full = lambda shape: pl.BlockSpec(shape, lambda b, c: (0,) * len(shape))
    hm, cnew, cout, nout, mout = pl.pallas_call(
        kern,
        grid=(bsz, nc),
        in_specs=[
            pl.BlockSpec((1, ln, 2 * M_WIDTH), lambda b, c: (b, c, OFF_QM // (2 * M_WIDTH))),
            pl.BlockSpec((1, ln, M_WIDTH), lambda b, c: (b, c, OFF_VM // M_WIDTH)),
            pl.BlockSpec((1, ln, M_WIDTH), lambda b, c: (b, c, OFF_OG // M_WIDTH)),
            pl.BlockSpec((1, ln, LANES), lambda b, c: (b, c, OFF_SM // LANES)),
            pl.BlockSpec((1, 1, 2 * M_HEADS, ln), lambda b, c: (b, c, 0, 0)),
            pl.BlockSpec((1, pad, 2 * M_WIDTH), lambda b, c: (b, 0, 0)),
            pl.BlockSpec((1, M_HEADS, M_HEAD_DIM, M_HEAD_DIM), lambda b, c: (b, 0, 0, 0)),
            pl.BlockSpec((1, M_HEADS, M_HEAD_DIM), lambda b, c: (b, 0, 0)),
            pl.BlockSpec((1, M_HEADS, LANES), lambda b, c: (b, 0, 0)),
            full((CONV_W, 2 * M_WIDTH)),
            full((1, 2 * M_WIDTH)),
            full((1, LANES)),
            full((2 * M_HEADS, LANES)),
            full((1, M_WIDTH)),
        ],
        out_specs=[
            pl.BlockSpec((1, ln, M_WIDTH), lambda b, c: (b, c, 0)),
            pl.BlockSpec((1, pad, 2 * M_WIDTH), lambda b, c: (b, 0, 0)),
            pl.BlockSpec((1, M_HEADS, M_HEAD_DIM, M_HEAD_DIM), lambda b, c: (b, 0, 0, 0)),
            pl.BlockSpec((1, M_HEADS, M_HEAD_DIM), lambda b, c: (b, 0, 0)),
            pl.BlockSpec((1, M_HEADS, LANES), lambda b, c: (b, 0, 0)),
        ],
        out_shape=[
            jax.ShapeDtypeStruct((bsz, t, M_WIDTH), BF16),
            jax.ShapeDtypeStruct((bsz, pad, 2 * M_WIDTH), F32),
            jax.ShapeDtypeStruct((bsz, M_HEADS, M_HEAD_DIM, M_HEAD_DIM), F32),
            jax.ShapeDtypeStruct((bsz, M_HEADS, M_HEAD_DIM), F32),
            jax.ShapeDtypeStruct((bsz, M_HEADS, LANES), F32),
        ],
        scratch_shapes=[
            pltpu.VMEM((ln + pad, 2 * M_WIDTH), F32),
            pltpu.VMEM((M_HEADS, M_HEAD_DIM, M_HEAD_DIM), F32),
            pltpu.VMEM((M_HEADS, M_HEAD_DIM), F32),
            pltpu.VMEM((M_HEADS, LANES), F32),
        ],
        compiler_params=_cparams(2),
        name="mlstm",
    )(z3, z3, z3, z3, gt, cprev, c0.astype(F32), n0.astype(F32), m0b,
      conv_w.astype(F32), conv_b.astype(F32)[None, :], gbrow, gbcol, mh_gain.astype(F32)[None, :])
    return hm, cnew[:, pad - (CONV_W - 1):, :], cout, nout, mout[:, :, 0]


def _post_kernel(att_ref, hm_ref, ga_ref, gb_ref, x_ref, wa_ref, wb_ref, wo_ref, gf_ref, wr_ref, br_ref,
                 x1_ref, h2_ref, ri_ref):
    ya = _dot(att_ref[...], wa_ref[...])
    yb = _dot(hm_ref[...], wb_ref[...])
    mixed = _sigmoid(ga_ref[...]) * ya + _sigmoid(gb_ref[...]) * yb
    x1 = x_ref[...] + _dot(mixed.astype(BF16), wo_ref[...])
    x1_ref[...] = x1
    h2 = x1 * lax.rsqrt(jnp.mean(x1 * x1, axis=-1, keepdims=True) + EPS) * gf_ref[...]
    h2_ref[...] = h2
    lg = jnp.dot(h2, wr_ref[...], preferred_element_type=F32, precision=lax.Precision.HIGHEST) + br_ref[...]
    lane = lax.broadcasted_iota(I32, lg.shape, 1)
    glog = jnp.where(lane < N_GROUPS, lg, -jnp.inf)
    gmax = jnp.max(glog, axis=1, keepdims=True)
    gstar = jnp.min(jnp.where(glog == gmax, lane, LANES), axis=1, keepdims=True)
    pg = 1.0 / jnp.sum(jnp.exp(glog - gmax), axis=1, keepdims=True)
    lo = N_GROUPS + EXP_PER_GROUP * gstar
    elog = jnp.where((lane >= lo) & (lane < lo + EXP_PER_GROUP), lg, -jnp.inf)
    v0 = jnp.max(elog, axis=1, keepdims=True)
    i0 = jnp.min(jnp.where(elog == v0, lane, LANES), axis=1, keepdims=True)
    elog2 = jnp.where(lane == i0, -jnp.inf, elog)
    v1 = jnp.max(elog2, axis=1, keepdims=True)
    i1 = jnp.min(jnp.where(elog2 == v1, lane, LANES), axis=1, keepdims=True)
    e1 = jnp.exp(v1 - v0)
    w0 = pg / (1.0 + e1)
    w1 = pg * e1 / (1.0 + e1)
    out = jnp.where(lane == 0, (i0 - N_GROUPS).astype(F32), 0.0)
    out = jnp.where(lane == 1, (i1 - N_GROUPS).astype(F32), out)
    out = jnp.where(lane == 2, w0, out)
    out = jnp.where(lane == 3, w1, out)
    ri_ref[...] = out


def _post(att2, hm2, z2, x2, wa, wb, wo, g_ffn, wr, br):
    n = x2.shape[0]
    tm = TOK_TM
    row = lambda w: pl.BlockSpec((tm, w), lambda i: (i, 0))
    full = lambda shape: pl.BlockSpec(shape, lambda i: (0,) * len(shape))
    return pl.pallas_call(
        _post_kernel,
        grid=(n // tm,),
        in_specs=[
            row(ATT_WIDTH), row(M_WIDTH),
            pl.BlockSpec((tm, D_MODEL), lambda i: (i, OFF_GA // D_MODEL)),
            pl.BlockSpec((tm, D_MODEL), lambda i: (i, OFF_GB // D_MODEL)),
            row(D_MODEL),
            full((ATT_WIDTH, D_MODEL)), full((M_WIDTH, D_MODEL)), full((D_MODEL, D_MODEL)),
            full((1, D_MODEL)), full((D_MODEL, LANES)), full((1, LANES)),
        ],
        out_specs=[row(D_MODEL), row(D_MODEL), row(LANES)],
        out_shape=[jax.ShapeDtypeStruct((n, D_MODEL), F32),
                   jax.ShapeDtypeStruct((n, D_MODEL), F32),
                   jax.ShapeDtypeStruct((n, LANES), F32)],
        compiler_params=_cparams(1),
        name="post",
    )(att2, hm2, z2, z2, x2, wa, wb, wo, g_ffn, wr, br)


def _row_gather_start(idx_ref, n_rows, src_hbm, dst_buf, sem):
    def body(r, c):
        tok = idx_ref[0, 0, r]
        pltpu.make_async_copy(src_hbm.at[pl.ds(tok, 1)], dst_buf.at[pl.ds(r, 1)], sem).start()
        return c
    lax.fori_loop(0, n_rows, body, 0)


def _row_gather_wait(n_rows, src_hbm, dst_buf, sem):
    pltpu.make_async_copy(src_hbm.at[pl.ds(0, n_rows)], dst_buf, sem).wait()


def _moe_kernel(te_ref, nu_ref, tok_ref, tokn_ref, h2_hbm, w1_ref, w3_ref, w2_ref, y_ref,
                xbuf, sem, w1b, w3b, w2b):
    t = pl.program_id(0)
    n_used = nu_ref[0]
    slot = t % 2
    tm = MOE_TM

    @pl.when((t < n_used) & ((t == 0) | (te_ref[t] != te_ref[jnp.maximum(t - 1, 0)])))
    def _():
        w1b[...] = w1_ref[0, 0].astype(BF16)
        w3b[...] = w3_ref[0, 0].astype(BF16)
        w2b[...] = w2_ref[0, 0].astype(BF16)

    @pl.when(t == 0)
    def _():
        _row_gather_start(tok_ref, tm, h2_hbm, xbuf.at[0], sem.at[0])

    @pl.when(t + 1 < n_used)
    def _():
        _row_gather_start(tokn_ref, tm, h2_hbm, xbuf.at[1 - slot], sem.at[1 - slot])

    @pl.when(t < n_used)
    def _():
        _row_gather_wait(tm, h2_hbm, xbuf.at[slot], sem.at[slot])
        xb = xbuf[slot].astype(BF16)
        u = _dot(xb, w1b[...])
        a = (u * _sigmoid(u)) * _dot(xb, w3b[...])
        y_ref[...] = _dot(a.astype(BF16), w2b[...])

    @pl.when(t >= n_used)
    def _():
        y_ref[...] = jnp.zeros(y_ref.shape, F32)


def _moe(h2, tile_expert, n_used, row_token, w1, w3, w2):
    n_tiles = tile_expert.shape[0]
    tm = MOE_TM
    tok3 = row_token.reshape(n_tiles, 1, tm)
    grid_spec = pltpu.PrefetchScalarGridSpec(
        num_scalar_prefetch=2,
        grid=(n_tiles,),
        in_specs=[
            pl.BlockSpec((1, 1, tm), lambda t, te, nu: (t, 0, 0), memory_space=pltpu.SMEM),
            pl.BlockSpec((1, 1, tm), lambda t, te, nu: (jnp.minimum(t + 1, n_tiles - 1), 0, 0),
                         memory_space=pltpu.SMEM),
            pl.BlockSpec(memory_space=pl.ANY),
            pl.BlockSpec((1, 1, D_MODEL, D_EXPERT), lambda t, te, nu: (0, te[t], 0, 0)),
            pl.BlockSpec((1, 1, D_MODEL, D_EXPERT), lambda t, te, nu: (0, te[t], 0, 0)),
            pl.BlockSpec((1, 1, D_EXPERT, D_MODEL), lambda t, te, nu: (0, te[t], 0, 0)),
        ],
        out_specs=pl.BlockSpec((tm, D_MODEL), lambda t, te, nu: (t, 0)),
        scratch_shapes=[pltpu.VMEM((2, tm, D_MODEL), F32), pltpu.SemaphoreType.DMA((2,)),
                        pltpu.VMEM((D_MODEL, D_EXPERT), BF16), pltpu.VMEM((D_MODEL, D_EXPERT), BF16),
                        pltpu.VMEM((D_EXPERT, D_MODEL), BF16)],
    )
    return pl.pallas_call(
        _moe_kernel,
        grid_spec=grid_spec,
        out_shape=jax.ShapeDtypeStruct((n_tiles * tm, D_MODEL), F32),
        compiler_params=_cparams(1),
        name="moe",
    )(tile_expert, n_used, tok3, tok3, h2, w1, w3, w2)


def _route(rinfo, n_tiles):
    n = rinfo.shape[0]
    tm = MOE_TM
    eid = rinfo[:, 0:2].astype(I32).reshape(-1)
    order = jnp.argsort(eid, stable=True).astype(I32)
    inv = jnp.argsort(order).astype(I32)
    onehot = eid[:, None] == jnp.arange(N_EXPERTS, dtype=I32)[None, :]
    counts = jnp.sum(onehot.astype(I32), axis=0)
    tiles_e = (counts + tm - 1) // tm
    tile_end = jnp.cumsum(tiles_e)
    row_start = (tile_end - tiles_e) * tm
    grp_start = jnp.cumsum(counts) - counts
    shift = row_start - grp_start
    pos = inv + jnp.sum(jnp.where(onehot, shift[None, :], 0), axis=1)
    n_used = tile_end[-1]
    tile_ids = jnp.arange(n_tiles, dtype=I32)
    te_raw = jnp.sum((tile_end[None, :] <= tile_ids[:, None]).astype(I32), axis=1)
    last_e = jnp.sum((tile_end <= n_used - 1).astype(I32))
    tile_expert = jnp.minimum(te_raw, last_e)
    te_c = jnp.minimum(te_raw, N_EXPERTS - 1)
    src = jnp.arange(n_tiles * tm, dtype=I32) - jnp.repeat(shift[te_c], tm)
    lo = jnp.repeat(grp_start[te_c], tm)
    hi = lo + jnp.repeat(counts[te_c], tm)
    valid = (src >= lo) & (src < hi)
    row_token = jnp.where(valid, order[jnp.clip(src, 0, 2 * n - 1)] // 2, 0)
    return tile_expert, n_used.reshape(1).astype(I32), row_token, pos.reshape(n, 2)


def _final_kernel(pos_ref, posn_ref, y_hbm, x1_ref, ri_ref, pe_ref, gp_ref, wpg_ref, wp_ref, gfin_ref,
                  o_ref, ybuf, sem):
    t = pl.program_id(0)
    nt = pl.num_programs(0)
    slot = t % 2
    tm = TOK_TM

    @pl.when(t == 0)
    def _():
        _row_gather_start(pos_ref, 2 * tm, y_hbm, ybuf.at[0], sem.at[0])

    @pl.when(t + 1 < nt)
    def _():
        _row_gather_start(posn_ref, 2 * tm, y_hbm, ybuf.at[1 - slot], sem.at[1 - slot])

    _row_gather_wait(2 * tm, y_hbm, ybuf.at[slot], sem.at[slot])
    ri = ri_ref[...]
    moe = ri[:, 2:3] * ybuf[slot, 0:tm] + ri[:, 3:4] * ybuf[slot, tm:2 * tm]
    x2 = x1_ref[...] + moe
    hp = x2 * lax.rsqrt(jnp.mean(x2 * x2, axis=-1, keepdims=True) + EPS) * gp_ref[...]
    gate = _sigmoid(_dot(hp.astype(BF16), wpg_ref[...]))
    x3 = x2 + gate * _dot(pe_ref[...].astype(BF16), wp_ref[...])
    o_ref[...] = x3 * lax.rsqrt(jnp.mean(x3 * x3, axis=-1, keepdims=True) + EPS) * gfin_ref[...]


def _final(x1, rinfo, pos, y_sorted, pe2, g_ple, wpg, wp, g_final):
    n = x1.shape[0]
    tm = TOK_TM
    nt = n // tm
    pos3 = pos.reshape(nt, tm, 2).transpose(0, 2, 1).reshape(nt, 1, 2 * tm)
    row = lambda w: pl.BlockSpec((tm, w), lambda i: (i, 0))
    full = lambda shape: pl.BlockSpec(shape, lambda i: (0,) * len(shape))
    return pl.pallas_call(
        _final_kernel,
        grid=(nt,),
        in_specs=[
            pl.BlockSpec((1, 1, 2 * tm), lambda i: (i, 0, 0), memory_space=pltpu.SMEM),
            pl.BlockSpec((1, 1, 2 * tm), lambda i: (jnp.minimum(i + 1, nt - 1), 0, 0),
                         memory_space=pltpu.SMEM),
            pl.BlockSpec(memory_space=pl.ANY),
            row(D_MODEL), row(LANES), row(PLE_DIM),
            full((1, D_MODEL)), full((D_MODEL, D_MODEL)), full((PLE_DIM, D_MODEL)), full((1, D_MODEL)),
        ],
        out_specs=row(D_MODEL),
        out_shape=jax.ShapeDtypeStruct((n, D_MODEL), F32),
        scratch_shapes=[pltpu.VMEM((2, 2 * tm, D_MODEL), F32), pltpu.SemaphoreType.DMA((2,))],
        compiler_params=_cparams(1),
        name="final",
    )(pos3, pos3, y_sorted, x1, rinfo, pe2, g_ple, wpg, wp, g_final)


def _pack_w_in(w):
    offs = np.concatenate([[0], np.cumsum(PROJ_SIZES)])
    qa, ka, va, qi, ki, wi, qm, km, vm, ig, fg, og, ga, gb = [w[:, offs[k]:offs[k + 1]] for k in range(14)]
    small = jnp.concatenate([ki, wi, ig, fg], axis=1)
    small = jnp.pad(small, ((0, 0), (0, SM_WIDTH - small.shape[1])))
    return jnp.concatenate([qa, qi, qm, km, vm, og, ga, gb, ka, va, small], axis=1).astype(BF16)


def _prep_weights(g_mix, w_in, w_att_out, w_mlstm_out, w_out, g_ffn, router_gw, router_gb, router_ew,
                  router_eb, w1, w3, w2, g_ple, w_ple, w_ple_gate, g_final):
    wr = jnp.concatenate([router_gw, router_ew], axis=1).astype(F32)
    wr = jnp.pad(wr, ((0, 0), (0, LANES - wr.shape[1])))
    br = jnp.concatenate([router_gb, router_eb]).astype(F32)
    br = jnp.pad(br, (0, LANES - br.shape[0]))[None, :]
    return dict(
        g_mix=g_mix.astype(F32)[None, :], w_pack=_pack_w_in(w_in),
        wa=w_att_out.astype(BF16), wb=w_mlstm_out.astype(BF16), wo=w_out.astype(BF16),
        g_ffn=g_ffn.astype(F32)[None, :], wr=wr, br=br,
        w1=w1.reshape((1,) + w1.shape[-3:]), w3=w3.reshape((1,) + w3.shape[-3:]),
        w2=w2.reshape((1,) + w2.shape[-3:]),
        g_ple=g_ple.astype(F32)[None, :], wp=w_ple.astype(BF16), wpg=w_ple_gate.astype(BF16),
        g_final=g_final.astype(F32)[None, :])


def _layer(x, pe, past, conv_prev, c0, n0, m0, wts, conv_w, conv_b, b_igate, b_fgate, mh_gain, qb, ln):
    bsz, t, _ = x.shape
    n = bsz * t
    x2 = x.reshape(n, D_MODEL)
    z, kvb, k_new, v_new, ki_new = _inproj(x2, wts["g_mix"], wts["w_pack"])
    z3 = z.reshape(bsz, t, D_PACK)
    kv3 = kvb.reshape(bsz, t, KV_PACK)
    att = _dsa(z3, kv3, past, qb)
    hm, conv_new, c_new, n_new, m_new = _mlstm(z3, conv_prev, c0, n0, m0, conv_w, conv_b,
                                               b_igate, b_fgate, mh_gain, ln)
    x1, h2, rinfo = _post(att.reshape(n, ATT_WIDTH), hm.reshape(n, M_WIDTH), z, x2,
                          wts["wa"], wts["wb"], wts["wo"], wts["g_ffn"], wts["wr"], wts["br"])
    n_tiles = (2 * n + N_EXPERTS * (MOE_TM - 1) + MOE_TM - 1) // MOE_TM
    tile_expert, n_used, row_token, pos = _route(rinfo, n_tiles)
    y_sorted = _moe(h2, tile_expert, n_used, row_token, wts["w1"], wts["w3"], wts["w2"])
    y = _final(x1, rinfo, pos, y_sorted, pe.reshape(n, PLE_DIM).astype(F32),
               wts["g_ple"], wts["wpg"], wts["wp"], wts["g_final"])
    k_new = k_new.reshape(bsz, t, N_KV_HEADS, HEAD_DIM)
    v_new = v_new.reshape(bsz, t, N_KV_HEADS, HEAD_DIM)
    ki_new = ki_new.reshape(bsz, t, IDX_DIM)
    return y.reshape(bsz, t, D_MODEL), (k_new, v_new, ki_new, conv_new, c_new, n_new, m_new)


def kernel(x_prompt, x_sample, cache_k, cache_v, cache_kidx, state_conv, state_C, state_n, state_m,
           p_prompt, p_sample, g_mix, w_in, conv_w, conv_b, b_igate, b_fgate, mh_gain,
           w_att_out, w_mlstm_out, w_out, g_ffn, router_gw, router_gb, router_ew, router_eb,
           w1, w3, w2, g_ple, w_ple, w_ple_gate, g_final):
    assert g_mix.shape[0] == 1, "single-layer step"
    bp, tp, _ = x_prompt.shape
    bs, ts, _ = x_sample.shape
    sdt = state_C.dtype
    wts = _prep_weights(g_mix[0], w_in[0], w_att_out[0], w_mlstm_out[0], w_out[0], g_ffn[0],
                        router_gw[0], router_gb[0], router_ew[0], router_eb[0], w1, w3, w2,
                        g_ple[0], w_ple[0], w_ple_gate[0], g_final)
    mix = (conv_w[0], conv_b[0], b_igate[0], b_fgate[0], mh_gain[0])
    yp, st_p = _layer(
        x_prompt, p_prompt[0], None,
        jnp.zeros((bp, CONV_W - 1, 2 * M_WIDTH), F32),
        jnp.zeros((bp, M_HEADS, M_HEAD_DIM, M_HEAD_DIM), F32),
        jnp.zeros((bp, M_HEADS, M_HEAD_DIM), F32),
        jnp.zeros((bp, M_HEADS), F32),
        wts, *mix, qb=min(128, tp), ln=min(256, tp))
    plen = cache_k.shape[2]
    past = (cache_k[0].reshape(bs, plen, KV_WIDTH), cache_v[0].reshape(bs, plen, KV_WIDTH), cache_kidx[0])
    ys, st_s = _layer(
        x_sample, p_sample[0], past, state_conv[0], state_C[0], state_n[0], state_m[0],
        wts, *mix, qb=ts, ln=ts)
    outs_p = [s[None] for s in st_p]
    outs_s = [s[None] for s in st_s]
    for lst in (outs_p, outs_s):
        for k in (4, 5, 6):
            lst[k] = lst[k].astype(sdt)
    return (yp, ys, *outs_p, *outs_s)
```

```python
import functools

import numpy as np
import jax
import jax.numpy as jnp
from jax import lax
from jax.experimental import pallas as pl
from jax.experimental.pallas import tpu as pltpu

F32 = jnp.float32
BF16 = jnp.bfloat16
I32 = jnp.int32

D_MODEL = 2048
CHUNK = 64
CHUNK_SHIFT = 6
assert 1 << CHUNK_SHIFT == CHUNK
N_HEADS = 8
N_KV_HEADS = 4
HEAD_DIM = 128
ATT_WIDTH = N_HEADS * HEAD_DIM
KV_WIDTH = N_KV_HEADS * HEAD_DIM
IDX_HEADS = 16
IDX_DIM = 64
TOPK_MAX = 256
M_HEADS = 4
M_HEAD_DIM = 256
M_WIDTH = M_HEADS * M_HEAD_DIM
CONV_W = 4
N_GROUPS = 4
EXP_PER_GROUP = 8
N_EXPERTS = N_GROUPS * EXP_PER_GROUP
D_EXPERT = 512
PLE_DIM = 256
EPS = 1e-6
PROJ_SIZES = (ATT_WIDTH, KV_WIDTH, KV_WIDTH, IDX_HEADS * IDX_DIM, IDX_DIM, IDX_HEADS,
              M_WIDTH, M_WIDTH, M_WIDTH, M_HEADS, M_HEADS, M_WIDTH, D_MODEL, D_MODEL)

LANES = 128
SUBLANES = 8
VMEM_LIMIT = 56 * 1024 * 1024

OFF_QA, OFF_QI, OFF_QM, OFF_KM, OFF_VM, OFF_OG = 0, 1024, 2048, 3072, 4096, 5120
OFF_GA, OFF_GB, OFF_KA, OFF_VA, OFF_SM = 6144, 8192, 10240, 10752, 11264
SM_WIDTH = 512
D_PACK = OFF_SM + SM_WIDTH
SM_KI, SM_WI, SM_IG, SM_FG = 0, 64, 80, 84
PROJ_TN = 512
KV_BLK0 = OFF_KA // PROJ_TN
KV_PACK = D_PACK - OFF_KA

LOG2E = 1.4426950408889634
INT_MIN = np.int32(-2 ** 31)
NEG_BIG = -1e30

DSA_KT = 512
DSA_QB = 128
DSA_SB = 128
MOE_TM = 512
TOK_TM = 256


def _cparams(n_axes):
    return pltpu.CompilerParams(dimension_semantics=("arbitrary",) * n_axes,
                                vmem_limit_bytes=VMEM_LIMIT)


def _dot(a, b):
    return jnp.dot(a, b, preferred_element_type=F32)


def _dot_nt(a, b):
    return lax.dot_general(a, b, (((1,), (1,)), ((), ())), preferred_element_type=F32)


def _dot_tn(a, b):
    return lax.dot_general(a, b, (((0,), (0,)), ((), ())), preferred_element_type=F32)


def _tile_rows(t, size):
    start = t * size
    return pl.ds(start if isinstance(start, int) else pl.multiple_of(start, size), size)


def _sigmoid(x):
    return 1.0 / (1.0 + jnp.exp(-x))


def _split3(x):
    hi = x.astype(BF16)
    r1 = x - hi.astype(F32)
    mid = r1.astype(BF16)
    lo = (r1 - mid.astype(F32)).astype(BF16)
    return hi, mid, lo


def _inproj_kernel(x_ref, g_ref, w_ref, z_ref, kv_ref, k_ref, v_ref, ki_ref, h_ref):
    j = pl.program_id(1)

    @pl.when(j == 0)
    def _():
        x = x_ref[...]
        r = lax.rsqrt(jnp.mean(x * x, axis=-1, keepdims=True) + EPS)
        h_ref[...] = (x * r * g_ref[...]).astype(BF16)

    acc = _dot(h_ref[...], w_ref[...])
    z_ref[...] = acc

    @pl.when(j >= KV_BLK0)
    def _():
        kv_ref[...] = acc.astype(BF16)

    @pl.when(j == KV_BLK0)
    def _():
        k_ref[...] = acc

    @pl.when(j == KV_BLK0 + 1)
    def _():
        v_ref[...] = acc

    @pl.when(j == KV_BLK0 + 2)
    def _():
        ki_ref[...] = acc[:, SM_KI:SM_KI + IDX_DIM]


def _inproj(x2d, g, w_pack):
    n = x2d.shape[0]
    tm = min(1024, n)
    grid = (n // tm, D_PACK // PROJ_TN)
    return pl.pallas_call(
        _inproj_kernel,
        grid=grid,
        in_specs=[
            pl.BlockSpec((tm, D_MODEL), lambda i, j: (i, 0)),
            pl.BlockSpec((1, D_MODEL), lambda i, j: (0, 0)),
            pl.BlockSpec((D_MODEL, PROJ_TN), lambda i, j: (0, j)),
        ],
        out_specs=[
            pl.BlockSpec((tm, PROJ_TN), lambda i, j: (i, j)),
            pl.BlockSpec((tm, PROJ_TN), lambda i, j: (i, jnp.maximum(j - KV_BLK0, 0))),
            pl.BlockSpec((tm, KV_WIDTH), lambda i, j: (i, 0)),
            pl.BlockSpec((tm, KV_WIDTH), lambda i, j: (i, 0)),
            pl.BlockSpec((tm, IDX_DIM), lambda i, j: (i, 0)),
        ],
        out_shape=[jax.ShapeDtypeStruct((n, D_PACK), F32),
                   jax.ShapeDtypeStruct((n, KV_PACK), BF16),
                   jax.ShapeDtypeStruct((n, KV_WIDTH), F32),
                   jax.ShapeDtypeStruct((n, KV_WIDTH), F32),
                   jax.ShapeDtypeStruct((n, IDX_DIM), F32)],
        scratch_shapes=[pltpu.VMEM((tm, D_MODEL), BF16)],
        compiler_params=_cparams(2),
        name="inproj",
    )(x2d, g, w_pack)


def _dsa_kernel(*refs, has_past, qblk, sb, n_past_tiles, n_new_tiles, t_valid, past_len, topk):
    n_in = 9 if has_past else 6
    if has_past:
        qa_ref, qi_ref, sm_ref, kip_ref, kp_ref, vp_ref, kin_ref, kn_ref, vn_ref = refs[:n_in]
    else:
        qa_ref, qi_ref, sm_ref, kin_ref, kn_ref, vn_ref = refs[:n_in]
    o_ref, keys_ref, d_ref, s_ref = refs[n_in:n_in + 4]
    state = refs[n_in + 4:]
    mx_refs, acc_refs = state[0::2], state[1::2]
    kt = DSA_KT
    nsub = qblk // sb
    i = pl.program_id(1)
    q0 = past_len + i * qblk
    j_end = ((q0 + qblk - 1) // CHUNK + 1) * CHUNK - past_len
    nk_new = jnp.minimum((j_end + kt - 1) // kt, n_new_tiles)
    n_tiles = n_past_tiles + nk_new
    lane_sb = lax.broadcasted_iota(I32, (sb, kt), 1)

    for sub in range(nsub):
        r0 = sub * sb
        qchunk = (q0 + r0 + lax.broadcasted_iota(I32, (sb, 1), 0)) >> CHUNK_SHIFT
        wsc = sm_ref[0, r0:r0 + sb, SM_WI:SM_WI + IDX_HEADS] * (IDX_HEADS ** -0.5 * IDX_DIM ** -0.5)
        wcols = [wsc[:, h:h + 1] for h in range(IDX_HEADS)]
        qi_all = qi_ref[0, 0, sub * IDX_HEADS * sb:(sub + 1) * IDX_HEADS * sb]

        def score_tile(ki_t, kpos0, jvalid0, col, r0=r0, qchunk=qchunk, wcols=wcols, qi_all=qi_all):
            d_ref[...] = _dot_nt(qi_all, ki_t)
            acc = wcols[0] * jnp.maximum(d_ref[0:sb], 0.0)
            for h in range(1, IDX_HEADS):
                acc = acc + wcols[h] * jnp.maximum(d_ref[h * sb:(h + 1) * sb], 0.0)
            bits = lax.bitcast_convert_type(acc + 0.0, I32)
            key = bits ^ ((bits >> 31) & np.int32(0x7FFFFFFF))
            adm = (((kpos0 + lane_sb) >> CHUNK_SHIFT) <= qchunk) & (jvalid0 + lane_sb < t_valid)
            keys_ref[col, r0:r0 + sb, :] = jnp.where(adm, key, INT_MIN)

        if has_past:
            def p1_past(t, c, score_tile=score_tile):
                ki_t = kip_ref[0, _tile_rows(t, kt), :].astype(BF16)
                score_tile(ki_t, t * kt, -(2 ** 30), t)
                return c
            lax.fori_loop(0, n_past_tiles, p1_past, 0)

        def p1_new(t, c, score_tile=score_tile):
            ki_t = kin_ref[0, _tile_rows(t, kt), 0:IDX_DIM]
            score_tile(ki_t, past_len + t * kt, t * kt, n_past_tiles + t)
            return c
        lax.fori_loop(0, nk_new, p1_new, 0)

    qb = qblk
    lane = lax.broadcasted_iota(I32, (qb, kt), 1)

    def count(fn):
        def body(t, part):
            kall = keys_ref[t]
            for s in range(kt // LANES):
                ks = kall[:, s * LANES:(s + 1) * LANES]
                part = part + jnp.where(fn(ks, t, s), 1.0, 0.0)
            return part
        part = lax.fori_loop(0, n_tiles, body, jnp.zeros((qb, LANES), F32))
        return jnp.sum(part, axis=1, keepdims=True)

    kf = float(topk)

    bits_per_check = 4

    def sgroup(carry):
        grp, tu, done, _ = carry
        for b in range(bits_per_check):
            shift = jnp.asarray(31 - b, I32) - grp * bits_per_check
            cand = tu | lax.shift_left(np.int32(1), shift)
            cs = cand ^ INT_MIN
            cnt = count(lambda k, t, s: k >= cs)
            tu = jnp.where((cnt >= kf) & (done == 0.0), cand, tu)
            done = jnp.where(cnt == kf, 1.0, done)
        return grp + 1, tu, done, jnp.min(done)

    def scond(carry):
        grp, _, _, all_done = carry
        return (grp < 32 // bits_per_check) & (all_done == 0.0)

    _, tu, _, _ = lax.while_loop(
        scond, sgroup, (jnp.int32(0), jnp.zeros((qb, 1), I32), jnp.zeros((qb, 1), F32), jnp.float32(0.0)))
    thr = tu ^ INT_MIN
    cnt_ge = count(lambda k, t, s: k >= thr)
    cnt_gt = count(lambda k, t, s: k > thr)
    tie = (cnt_ge > kf) & (thr > INT_MIN)
    need = kf - cnt_gt

    @pl.when(jnp.max(jnp.where(tie, 1.0, 0.0)) > 0.0)
    def _():
        lane1 = lax.broadcasted_iota(I32, (qb, LANES), 1)

        def jbody(it, a):
            cand = a | lax.shift_left(np.int32(1), jnp.asarray(15 - it, I32))
            cnt = count(lambda k, t, s: (k == thr) & (t * kt + s * LANES + lane1 < cand))
            return jnp.where(cnt < need, cand, a)
        a = lax.fori_loop(0, 16, jbody, jnp.zeros((qb, 1), I32))

        def drop(t, c):
            k = keys_ref[t]
            keys_ref[t] = jnp.where(tie & (k == thr) & (t * kt + lane > a), INT_MIN, k)
            return c
        lax.fori_loop(0, n_tiles, drop, 0)

    thr_eff = jnp.maximum(thr, INT_MIN + 1)

    def hs(g):
        return slice(g * HEAD_DIM, (g + 1) * HEAD_DIM)

    ones_blk = jnp.ones((kt, HEAD_DIM), BF16)

    for sub in range(nsub):
        r0 = sub * sb
        thr_sub = thr_eff[r0:r0 + sb]
        q = qa_ref[0, r0:r0 + sb, :] * (HEAD_DIM ** -0.5 * LOG2E)
        q2 = []
        for g in range(N_KV_HEADS):
            a = q[:, (2 * g) * HEAD_DIM:(2 * g + 1) * HEAD_DIM]
            b = q[:, (2 * g + 1) * HEAD_DIM:(2 * g + 2) * HEAD_DIM]
            q2.append(jnp.concatenate([a, b], axis=0).astype(BF16))
        for g in range(N_KV_HEADS):
            mx_refs[g][...] = jnp.full(mx_refs[g].shape, -jnp.inf, F32)
            acc_refs[g][...] = jnp.zeros(acc_refs[g].shape, F32)

        def logits(col, k_fn, r0=r0, thr_sub=thr_sub, q2=q2):
            bias = jnp.where(keys_ref[col, r0:r0 + sb, :] >= thr_sub, 0.0, NEG_BIG)
            bias2 = jnp.concatenate([bias, bias], axis=0)
            for g in range(N_KV_HEADS):
                s = _dot_nt(q2[g], k_fn(g)) + bias2
                s_ref[g, col] = s
                mx = mx_refs[g][...]
                for c in range(kt // LANES):
                    mx = jnp.maximum(mx, s[:, c * LANES:(c + 1) * LANES])
                mx_refs[g][...] = mx

        def weighted(col, v_fn, m_rows):
            for g in range(N_KV_HEADS):
                p = jnp.exp2(s_ref[g, col] - m_rows[g]).astype(BF16)
                v_aug = jnp.concatenate([v_fn(g), ones_blk], axis=1)
                acc_refs[g][...] = acc_refs[g][...] + _dot(p, v_aug)

        if has_past:
            def pa_past(t, c, logits=logits):
                rows = _tile_rows(t, kt)
                logits(t, lambda g: kp_ref[0, rows, hs(g)].astype(BF16))
                return c
            lax.fori_loop(0, n_past_tiles, pa_past, 0)

        def pa_new(t, c, logits=logits):
            rows = _tile_rows(t, kt)
            logits(n_past_tiles + t, lambda g: kn_ref[0, rows, hs(g)])
            return c
        lax.fori_loop(0, nk_new, pa_new, 0)

        m_rows = [jnp.max(mx_refs[g][...], axis=1, keepdims=True) for g in range(N_KV_HEADS)]

        if has_past:
            def pb_past(t, c, m_rows=m_rows, weighted=weighted):
                rows = _tile_rows(t, kt)
                weighted(t, lambda g: vp_ref[0, rows, hs(g)].astype(BF16), m_rows)
                return c
            lax.fori_loop(0, n_past_tiles, pb_past, 0)

        def pb_new(t, c, m_rows=m_rows, weighted=weighted):
            rows = _tile_rows(t, kt)
            weighted(n_past_tiles + t, lambda g: vn_ref[0, rows, hs(g)], m_rows)
            return c
        lax.fori_loop(0, nk_new, pb_new, 0)

        for g in range(N_KV_HEADS):
            acc = acc_refs[g][...]
            o = acc[:, 0:HEAD_DIM] / acc[:, HEAD_DIM:2 * HEAD_DIM]
            o_ref[0, r0:r0 + sb, (2 * g) * HEAD_DIM:(2 * g + 1) * HEAD_DIM] = o[0:sb].astype(BF16)
            o_ref[0, r0:r0 + sb, (2 * g + 1) * HEAD_DIM:(2 * g + 2) * HEAD_DIM] = o[sb:2 * sb].astype(BF16)


def _dsa(z3, kv3, past, qb, sb):
    bsz, t, _ = z3.shape
    kt = DSA_KT
    nb = t // qb
    nsub = qb // sb
    has_past = past is not None
    past_len = past[0].shape[1] if has_past else 0
    topk = min(TOPK_MAX, (past_len + t) // 4)
    t_pad = -(-t // kt) * kt
    if t_pad != t:
        kv3 = jnp.pad(kv3, ((0, 0), (0, t_pad - t), (0, 0)))
    n_new_tiles = t_pad // kt
    n_past_tiles = past_len // kt
    qi = z3[:, :, OFF_QI:OFF_QI + IDX_HEADS * IDX_DIM].astype(BF16)
    qi = qi.reshape(bsz, nb, nsub, sb, IDX_HEADS, IDX_DIM).transpose(0, 1, 2, 4, 3, 5)
    qi = qi.reshape(bsz, nb, IDX_HEADS * qb, IDX_DIM)

    in_specs = [
        pl.BlockSpec((1, qb, ATT_WIDTH), lambda b, i: (b, i, OFF_QA // ATT_WIDTH)),
        pl.BlockSpec((1, 1, IDX_HEADS * qb, IDX_DIM), lambda b, i: (b, i, 0, 0)),
        pl.BlockSpec((1, qb, LANES), lambda b, i: (b, i, OFF_SM // LANES)),
    ]
    args = [z3, qi, z3]
    if has_past:
        pk, pv, pki = past
        in_specs += [
            pl.BlockSpec((1, past_len, IDX_DIM), lambda b, i: (b, 0, 0)),
            pl.BlockSpec((1, past_len, KV_WIDTH), lambda b, i: (b, 0, 0)),
            pl.BlockSpec((1, past_len, KV_WIDTH), lambda b, i: (b, 0, 0)),
        ]
        args += [pki, pk, pv]
    in_specs += [
        pl.BlockSpec((1, t_pad, LANES), lambda b, i: (b, 0, 2 * KV_WIDTH // LANES)),
        pl.BlockSpec((1, t_pad, KV_WIDTH), lambda b, i: (b, 0, 0)),
        pl.BlockSpec((1, t_pad, KV_WIDTH), lambda b, i: (b, 0, 1)),
    ]
    args += [kv3, kv3, kv3]
    kern = functools.partial(_dsa_kernel, has_past=has_past, qblk=qb, sb=sb, n_past_tiles=n_past_tiles,
                             n_new_tiles=n_new_tiles, t_valid=t, past_len=past_len, topk=topk)
    return pl.pallas_call(
        kern,
        grid=(bsz, nb),
        in_specs=in_specs,
        out_specs=pl.BlockSpec((1, qb, ATT_WIDTH), lambda b, i: (b, i, 0)),
        out_shape=jax.ShapeDtypeStruct((bsz, t, ATT_WIDTH), BF16),
        scratch_shapes=[pltpu.VMEM((n_past_tiles + n_new_tiles, qb, kt), I32),
                        pltpu.VMEM((IDX_HEADS * sb, kt), F32),
                        pltpu.VMEM((N_KV_HEADS, n_past_tiles + n_new_tiles, 2 * sb, kt), F32)] + [
            pltpu.VMEM((2 * sb, LANES), F32), pltpu.VMEM((2 * sb, 2 * HEAD_DIM), F32)] * N_KV_HEADS,
        compiler_params=_cparams(2),
        name="dsa_past" if has_past else "dsa",
    )(*args)


def _mlstm_kernel(qk_ref, vm_ref, og_ref, sm_ref, gt_ref, cprev_ref, c0_ref, n0_ref, m0_ref,
                  cw_ref, cb_ref, gbrow_ref, gbcol_ref, gain_ref,
                  hm_ref, cnew_ref, cout_ref, nout_ref, mout_ref,
                  ubuf, c_s, n_s, m_s, *, ln):
    c = pl.program_id(1)
    nc = pl.num_programs(1)
    pad = SUBLANES

    @pl.when(c == 0)
    def _():
        ubuf[0:pad] = cprev_ref[0]
        c_s[...] = c0_ref[0]
        n_s[...] = n0_ref[0]
        m_s[...] = m0_ref[0]

    @pl.when(c > 0)
    def _():
        ubuf[0:pad] = ubuf[ln:ln + pad]

    ubuf[pad:pad + ln] = qk_ref[0]
    y = cb_ref[...] + ubuf[pad - 3:pad - 3 + ln] * cw_ref[0:1]
    for j in range(1, CONV_W):
        y = y + ubuf[pad - 3 + j:pad - 3 + j + ln] * cw_ref[j:j + 1]
    qkc = y * _sigmoid(y)
    cnew_ref[0] = ubuf[ln:ln + pad]

    pre_c = sm_ref[0] + gbrow_ref[...]
    pre_r = gt_ref[0, 0] + gbcol_ref[...][:, 0:1]

    def log_sigmoid(x):
        return jnp.minimum(x, 0.0) - jnp.log(1.0 + jnp.exp(-jnp.abs(x)))

    lf_c = log_sigmoid(pre_c)
    lf_r = log_sigmoid(pre_r)
    ri = lax.broadcasted_iota(I32, (ln, ln), 0)
    ci = lax.broadcasted_iota(I32, (ln, ln), 1)
    causal = ci <= ri
    tril = jnp.where(causal, 1.0, 0.0).astype(BF16)
    triu = jnp.where(ri <= ci, 1.0, 0.0).astype(BF16)
    b_c = sum(_dot(tril, p) for p in _split3(lf_c))
    b_r = sum(_dot(p, triu) for p in _split3(lf_r))

    vm = vm_ref[0]
    og = og_ref[0]
    for h in range(M_HEADS):
        hs = slice(h * M_HEAD_DIM, (h + 1) * M_HEAD_DIM)
        qf = qkc[:, hs]
        kf = qkc[:, M_WIDTH + h * M_HEAD_DIM:M_WIDTH + (h + 1) * M_HEAD_DIM] * (M_HEAD_DIM ** -0.5)
        qb16 = qf.astype(BF16)
        kb16 = kf.astype(BF16)
        vb16 = vm[:, hs].astype(BF16)
        bcol = b_c[:, SM_FG + h:SM_FG + h + 1]
        igcol = pre_c[:, SM_IG + h:SM_IG + h + 1]
        brow = b_r[M_HEADS + h:M_HEADS + h + 1, :]
        igrow = pre_r[h:h + 1, :]
        blast = bcol[ln - 1:ln, :]
        m_prev = m_s[h:h + 1, 0:1]
        dmat = jnp.where(causal, bcol - brow + igrow, -jnp.inf)
        m_inter = bcol + m_prev
        m_i = jnp.maximum(m_inter, jnp.max(dmat, axis=1, keepdims=True))
        s = _dot_nt(qb16, kb16) * jnp.exp(dmat - m_i)
        scale = jnp.exp(m_inter - m_i)
        c_prev = c_s[h]
        n_prev = n_s[h:h + 1, :]
        num = _dot(s.astype(BF16), vb16) + scale * _dot(qb16, c_prev.astype(BF16))
        den = jnp.sum(s, axis=1, keepdims=True) + scale * jnp.sum(qf * n_prev, axis=1, keepdims=True)
        hh = num / jnp.maximum(jnp.abs(den), jnp.exp(-m_i))
        m_new = m_i[ln - 1:ln, :]
        decay = jnp.exp(blast + m_prev - m_new)
        wcol = jnp.exp(blast - bcol + igcol - m_new)
        kw = kf * wcol
        c_s[h] = decay * c_prev + _dot_tn(kw.astype(BF16), vb16)
        n_s[h:h + 1, :] = decay * n_prev + jnp.sum(kw, axis=0, keepdims=True)
        m_s[h:h + 1, :] = jnp.broadcast_to(m_new, (1, LANES))
        hn = hh * lax.rsqrt(jnp.mean(hh * hh, axis=1, keepdims=True) + EPS) * gain_ref[:, hs]
        hm_ref[0, :, hs] = (hn * _sigmoid(og[:, hs])).astype(BF16)

    @pl.when(c == nc - 1)
    def _():
        cout_ref[0] = c_s[...]
        nout_ref[0] = n_s[...]
        mout_ref[0] = m_s[...]


def _mlstm(z3, conv_prev, c0, n0, m0, conv_w, conv_b, b_igate, b_fgate, mh_gain, ln):
    bsz, t, _ = z3.shape
    nc = t // ln
    pad = SUBLANES
    gt = z3[:, :, OFF_SM + SM_IG:OFF_SM + SM_IG + 2 * M_HEADS]
    gt = gt.reshape(bsz, nc, ln, 2 * M_HEADS).transpose(0, 1, 3, 2)
    cprev = jnp.pad(conv_prev.astype(F32), ((0, 0), (pad - (CONV_W - 1), 0), (0, 0)))
    gbias = jnp.concatenate([b_igate, b_fgate]).astype(F32)
    gbrow = jnp.zeros((1, LANES), F32).at[0, SM_IG:SM_IG + 2 * M_HEADS].set(gbias)
    gbcol = jnp.broadcast_to(gbias[:, None], (2 * M_HEADS, LANES))
    m0b = jnp.broadcast_to(m0.astype(F32)[:, :, None], (bsz, M_HEADS, LANES))
    kern = functools.partial(_mlstm_kernel, ln=ln)
    full = lambda shape: pl.BlockSpec(shape, lambda b, c: (0,) * len(shape))
    hm, cnew, cout, nout, mout = pl.pallas_call(
        kern,
        grid=(bsz, nc),
        in_specs=[
            pl.BlockSpec((1, ln, 2 * M_WIDTH), lambda b, c: (b, c, OFF_QM // (2 * M_WIDTH))),
            pl.BlockSpec((1, ln, M_WIDTH), lambda b, c: (b, c, OFF_VM // M_WIDTH)),
            pl.BlockSpec((1, ln, M_WIDTH), lambda b, c: (b, c, OFF_OG // M_WIDTH)),
            pl.BlockSpec((1, ln, LANES), lambda b, c: (b, c, OFF_SM // LANES)),
            pl.BlockSpec((1, 1, 2 * M_HEADS, ln), lambda b, c: (b, c, 0, 0)),
            pl.BlockSpec((1, pad, 2 * M_WIDTH), lambda b, c: (b, 0, 0)),
            pl.BlockSpec((1, M_HEADS, M_HEAD_DIM, M_HEAD_DIM), lambda b, c: (b, 0, 0, 0)),
            pl.BlockSpec((1, M_HEADS, M_HEAD_DIM), lambda b, c: (b, 0, 0)),
            pl.BlockSpec((1, M_HEADS, LANES), lambda b, c: (b, 0, 0)),
            full((CONV_W, 2 * M_WIDTH)),
            full((1, 2 * M_WIDTH)),
            full((1, LANES)),
            full((2 * M_HEADS, LANES)),
            full((1, M_WIDTH)),
        ],
        out_specs=[
            pl.BlockSpec((1, ln, M_WIDTH), lambda b, c: (b, c, 0)),
            pl.BlockSpec((1, pad, 2 * M_WIDTH), lambda b, c: (b, 0, 0)),
            pl.BlockSpec((1, M_HEADS, M_HEAD_DIM, M_HEAD_DIM), lambda b, c: (b, 0, 0, 0)),
            pl.BlockSpec((1, M_HEADS, M_HEAD_DIM), lambda b, c: (b, 0, 0)),
            pl.BlockSpec((1, M_HEADS, LANES), lambda b, c: (b, 0, 0)),
        ],
        out_shape=[
            jax.ShapeDtypeStruct((bsz, t, M_WIDTH), BF16),
            jax.ShapeDtypeStruct((bsz, pad, 2 * M_WIDTH), F32),
            jax.ShapeDtypeStruct((bsz, M_HEADS, M_HEAD_DIM, M_HEAD_DIM), F32),
            jax.ShapeDtypeStruct((bsz, M_HEADS, M_HEAD_DIM), F32),
            jax.ShapeDtypeStruct((bsz, M_HEADS, LANES), F32),
        ],
        scratch_shapes=[
            pltpu.VMEM((ln + pad, 2 * M_WIDTH), F32),
            pltpu.VMEM((M_HEADS, M_HEAD_DIM, M_HEAD_DIM), F32),
            pltpu.VMEM((M_HEADS, M_HEAD_DIM), F32),
            pltpu.VMEM((M_HEADS, LANES), F32),
        ],
        compiler_params=_cparams(2),
        name="mlstm",
    )(z3, z3, z3, z3, gt, cprev, c0.astype(F32), n0.astype(F32), m0b,
      conv_w.astype(F32), conv_b.astype(F32)[None, :], gbrow, gbcol, mh_gain.astype(F32)[None, :])
    return hm, cnew[:, pad - (CONV_W - 1):, :], cout, nout, mout[:, :, 0]


def _post_kernel(att_ref, hm_ref, ga_ref, gb_ref, x_ref, wa_ref, wb_ref, wo_ref, gf_ref, wr_ref, br_ref,
                 x1_ref, h2_ref, ri_ref):
    ya = _dot(att_ref[...], wa_ref[...])
    yb = _dot(hm_ref[...], wb_ref[...])
    mixed = _sigmoid(ga_ref[...]) * ya + _sigmoid(gb_ref[...]) * yb
    x1 = x_ref[...] + _dot(mixed.astype(BF16), wo_ref[...])
    x1_ref[...] = x1
    h2 = x1 * lax.rsqrt(jnp.mean(x1 * x1, axis=-1, keepdims=True) + EPS) * gf_ref[...]
    h2_ref[...] = h2
    lg = jnp.dot(h2, wr_ref[...], preferred_element_type=F32, precision=lax.Precision.HIGHEST) + br_ref[...]
    lane = lax.broadcasted_iota(I32, lg.shape, 1)
    glog = jnp.where(lane < N_GROUPS, lg, -jnp.inf)
    gmax = jnp.max(glog, axis=1, keepdims=True)
    gstar = jnp.min(jnp.where(glog == gmax, lane, LANES), axis=1, keepdims=True)
    pg = 1.0 / jnp.sum(jnp.exp(glog - gmax), axis=1, keepdims=True)
    lo = N_GROUPS + EXP_PER_GROUP * gstar
    elog = jnp.where((lane >= lo) & (lane < lo + EXP_PER_GROUP), lg, -jnp.inf)
    v0 = jnp.max(elog, axis=1, keepdims=True)
    i0 = jnp.min(jnp.where(elog == v0, lane, LANES), axis=1, keepdims=True)
    elog2 = jnp.where(lane == i0, -jnp.inf, elog)
    v1 = jnp.max(elog2, axis=1, keepdims=True)
    i1 = jnp.min(jnp.where(elog2 == v1, lane, LANES), axis=1, keepdims=True)
    e1 = jnp.exp(v1 - v0)
    w0 = pg / (1.0 + e1)
    w1 = pg * e1 / (1.0 + e1)
    out = jnp.where(lane == 0, (i0 - N_GROUPS).astype(F32), 0.0)
    out = jnp.where(lane == 1, (i1 - N_GROUPS).astype(F32), out)
    out = jnp.where(lane == 2, w0, out)
    out = jnp.where(lane == 3, w1, out)
    ri_ref[...] = out


def _post(att2, hm2, z2, x2, wa, wb, wo, g_ffn, wr, br):
    n = x2.shape[0]
    tm = TOK_TM
    row = lambda w: pl.BlockSpec((tm, w), lambda i: (i, 0))
    full = lambda shape: pl.BlockSpec(shape, lambda i: (0,) * len(shape))
    return pl.pallas_call(
        _post_kernel,
        grid=(n // tm,),
        in_specs=[
            row(ATT_WIDTH), row(M_WIDTH),
            pl.BlockSpec((tm, D_MODEL), lambda i: (i, OFF_GA // D_MODEL)),
            pl.BlockSpec((tm, D_MODEL), lambda i: (i, OFF_GB // D_MODEL)),
            row(D_MODEL),
            full((ATT_WIDTH, D_MODEL)), full((M_WIDTH, D_MODEL)), full((D_MODEL, D_MODEL)),
            full((1, D_MODEL)), full((D_MODEL, LANES)), full((1, LANES)),
        ],
        out_specs=[row(D_MODEL), row(D_MODEL), row(LANES)],
        out_shape=[jax.ShapeDtypeStruct((n, D_MODEL), F32),
                   jax.ShapeDtypeStruct((n, D_MODEL), F32),
                   jax.ShapeDtypeStruct((n, LANES), F32)],
        compiler_params=_cparams(1),
        name="post",
    )(att2, hm2, z2, z2, x2, wa, wb, wo, g_ffn, wr, br)


def _row_gather_start(idx_ref, n_rows, src_hbm, dst_buf, sem):
    def body(r, c):
        tok = idx_ref[0, 0, r]
        pltpu.make_async_copy(src_hbm.at[pl.ds(tok, 1)], dst_buf.at[pl.ds(r, 1)], sem).start()
        return c
    lax.fori_loop(0, n_rows, body, 0, unroll=8)


def _row_gather_wait(n_rows, src_hbm, dst_buf, sem):
    pltpu.make_async_copy(src_hbm.at[pl.ds(0, n_rows)], dst_buf, sem).wait()


def _moe_kernel(te_ref, nu_ref, tok_ref, tokn_ref, h2_hbm, w1_ref, w3_ref, w2_ref, y_ref,
                xbuf, sem, w1b, w3b, w2b):
    t = pl.program_id(0)
    n_used = nu_ref[0]
    slot = t % 2
    tm = MOE_TM

    @pl.when((t < n_used) & ((t == 0) | (te_ref[t] != te_ref[jnp.maximum(t - 1, 0)])))
    def _():
        w1b[...] = w1_ref[0, 0].astype(BF16)
        w3b[...] = w3_ref[0, 0].astype(BF16)
        w2b[...] = w2_ref[0, 0].astype(BF16)

    @pl.when(t == 0)
    def _():
        _row_gather_start(tok_ref, tm, h2_hbm, xbuf.at[0], sem.at[0])

    @pl.when(t + 1 < n_used)
    def _():
        _row_gather_start(tokn_ref, tm, h2_hbm, xbuf.at[1 - slot], sem.at[1 - slot])

    @pl.when(t < n_used)
    def _():
        _row_gather_wait(tm, h2_hbm, xbuf.at[slot], sem.at[slot])
        xb = xbuf[slot].astype(BF16)
        u = _dot(xb, w1b[...])
        a = (u * _sigmoid(u)) * _dot(xb, w3b[...])
        y_ref[...] = _dot(a.astype(BF16), w2b[...])

    @pl.when(t >= n_used)
    def _():
        y_ref[...] = jnp.zeros(y_ref.shape, F32)


def _moe(h2, tile_expert, n_used, row_token, w1, w3, w2):
    n_tiles = tile_expert.shape[0]
    tm = MOE_TM
    tok3 = row_token.reshape(n_tiles, 1, tm)
    grid_spec = pltpu.PrefetchScalarGridSpec(
        num_scalar_prefetch=2,
        grid=(n_tiles,),
        in_specs=[
            pl.BlockSpec((1, 1, tm), lambda t, te, nu: (t, 0, 0), memory_space=pltpu.SMEM),
            pl.BlockSpec((1, 1, tm), lambda t, te, nu: (jnp.minimum(t + 1, n_tiles - 1), 0, 0),
                         memory_space=pltpu.SMEM),
            pl.BlockSpec(memory_space=pl.ANY),
            pl.BlockSpec((1, 1, D_MODEL, D_EXPERT), lambda t, te, nu: (0, te[t], 0, 0)),
            pl.BlockSpec((1, 1, D_MODEL, D_EXPERT), lambda t, te, nu: (0, te[t], 0, 0)),
            pl.BlockSpec((1, 1, D_EXPERT, D_MODEL), lambda t, te, nu: (0, te[t], 0, 0)),
        ],
        out_specs=pl.BlockSpec((tm, D_MODEL), lambda t, te, nu: (t, 0)),
        scratch_shapes=[pltpu.VMEM((2, tm, D_MODEL), F32), pltpu.SemaphoreType.DMA((2,)),
                        pltpu.VMEM((D_MODEL, D_EXPERT), BF16), pltpu.VMEM((D_MODEL, D_EXPERT), BF16),
                        pltpu.VMEM((D_EXPERT, D_MODEL), BF16)],
    )
    return pl.pallas_call(
        _moe_kernel,
        grid_spec=grid_spec,
        out_shape=jax.ShapeDtypeStruct((n_tiles * tm, D_MODEL), F32),
        compiler_params=_cparams(1),
        name="moe",
    )(tile_expert, n_used, tok3, tok3, h2, w1, w3, w2)


def _route(rinfo, n_tiles):
    n = rinfo.shape[0]
    tm = MOE_TM
    eid = rinfo[:, 0:2].astype(I32).reshape(-1)
    order = jnp.argsort(eid, stable=True).astype(I32)
    inv = jnp.argsort(order).astype(I32)
    onehot = eid[:, None] == jnp.arange(N_EXPERTS, dtype=I32)[None, :]
    counts = jnp.sum(onehot.astype(I32), axis=0)
    tiles_e = (counts + tm - 1) // tm
    tile_end = jnp.cumsum(tiles_e)
    row_start = (tile_end - tiles_e) * tm
    grp_start = jnp.cumsum(counts) - counts
    shift = row_start - grp_start
    pos = inv + jnp.sum(jnp.where(onehot, shift[None, :], 0), axis=1)
    n_used = tile_end[-1]
    tile_ids = jnp.arange(n_tiles, dtype=I32)
    te_raw = jnp.sum((tile_end[None, :] <= tile_ids[:, None]).astype(I32), axis=1)
    last_e = jnp.sum((tile_end <= n_used - 1).astype(I32))
    tile_expert = jnp.minimum(te_raw, last_e)
    te_c = jnp.minimum(te_raw, N_EXPERTS - 1)
    src = jnp.arange(n_tiles * tm, dtype=I32) - jnp.repeat(shift[te_c], tm)
    lo = jnp.repeat(grp_start[te_c], tm)
    hi = lo + jnp.repeat(counts[te_c], tm)
    valid = (src >= lo) & (src < hi)
    row_token = jnp.where(valid, order[jnp.clip(src, 0, 2 * n - 1)] // 2, 0)
    return tile_expert, n_used.reshape(1).astype(I32), row_token, pos.reshape(n, 2)


def _final_kernel(pos_ref, posn_ref, y_hbm, x1_ref, ri_ref, pe_ref, gp_ref, wpg_ref, wp_ref, gfin_ref,
                  o_ref, ybuf, sem):
    t = pl.program_id(0)
    nt = pl.num_programs(0)
    slot = t % 2
    tm = TOK_TM

    @pl.when(t == 0)
    def _():
        _row_gather_start(pos_ref, 2 * tm, y_hbm, ybuf.at[0], sem.at[0])

    @pl.when(t + 1 < nt)
    def _():
        _row_gather_start(posn_ref, 2 * tm, y_hbm, ybuf.at[1 - slot], sem.at[1 - slot])

    _row_gather_wait(2 * tm, y_hbm, ybuf.at[slot], sem.at[slot])
    ri = ri_ref[...]
    moe = ri[:, 2:3] * ybuf[slot, 0:tm] + ri[:, 3:4] * ybuf[slot, tm:2 * tm]
    x2 = x1_ref[...] + moe
    hp = x2 * lax.rsqrt(jnp.mean(x2 * x2, axis=-1, keepdims=True) + EPS) * gp_ref[...]
    gate = _sigmoid(_dot(hp.astype(BF16), wpg_ref[...]))
    x3 = x2 + gate * _dot(pe_ref[...].astype(BF16), wp_ref[...])
    o_ref[...] = x3 * lax.rsqrt(jnp.mean(x3 * x3, axis=-1, keepdims=True) + EPS) * gfin_ref[...]


def _final(x1, rinfo, pos, y_sorted, pe2, g_ple, wpg, wp, g_final):
    n = x1.shape[0]
    tm = TOK_TM
    nt = n // tm
    pos3 = pos.reshape(nt, tm, 2).transpose(0, 2, 1).reshape(nt, 1, 2 * tm)
    row = lambda w: pl.BlockSpec((tm, w), lambda i: (i, 0))
    full = lambda shape: pl.BlockSpec(shape, lambda i: (0,) * len(shape))
    return pl.pallas_call(
        _final_kernel,
        grid=(nt,),
        in_specs=[
            pl.BlockSpec((1, 1, 2 * tm), lambda i: (i, 0, 0), memory_space=pltpu.SMEM),
            pl.BlockSpec((1, 1, 2 * tm), lambda i: (jnp.minimum(i + 1, nt - 1), 0, 0),
                         memory_space=pltpu.SMEM),
            pl.BlockSpec(memory_space=pl.ANY),
            row(D_MODEL), row(LANES), row(PLE_DIM),
            full((1, D_MODEL)), full((D_MODEL, D_MODEL)), full((PLE_DIM, D_MODEL)), full((1, D_MODEL)),
        ],
        out_specs=row(D_MODEL),
        out_shape=jax.ShapeDtypeStruct((n, D_MODEL), F32),
        scratch_shapes=[pltpu.VMEM((2, 2 * tm, D_MODEL), F32), pltpu.SemaphoreType.DMA((2,))],
        compiler_params=_cparams(1),
        name="final",
    )(pos3, pos3, y_sorted, x1, rinfo, pe2, g_ple, wpg, wp, g_final)


def _pack_w_in(w):
    offs = np.concatenate([[0], np.cumsum(PROJ_SIZES)])
    qa, ka, va, qi, ki, wi, qm, km, vm, ig, fg, og, ga, gb = [w[:, offs[k]:offs[k + 1]] for k in range(14)]
    small = jnp.concatenate([ki, wi, ig, fg], axis=1)
    small = jnp.pad(small, ((0, 0), (0, SM_WIDTH - small.shape[1])))
    return jnp.concatenate([qa, qi, qm, km, vm, og, ga, gb, ka, va, small], axis=1).astype(BF16)


def _prep_weights(g_mix, w_in, w_att_out, w_mlstm_out, w_out, g_ffn, router_gw, router_gb, router_ew,
                  router_eb, w1, w3, w2, g_ple, w_ple, w_ple_gate, g_final):
    wr = jnp.concatenate([router_gw, router_ew], axis=1).astype(F32)
    wr = jnp.pad(wr, ((0, 0), (0, LANES - wr.shape[1])))
    br = jnp.concatenate([router_gb, router_eb]).astype(F32)
    br = jnp.pad(br, (0, LANES - br.shape[0]))[None, :]
    return dict(
        g_mix=g_mix.astype(F32)[None, :], w_pack=_pack_w_in(w_in),
        wa=w_att_out.astype(BF16), wb=w_mlstm_out.astype(BF16), wo=w_out.astype(BF16),
        g_ffn=g_ffn.astype(F32)[None, :], wr=wr, br=br,
        w1=w1.reshape((1,) + w1.shape[-3:]), w3=w3.reshape((1,) + w3.shape[-3:]),
        w2=w2.reshape((1,) + w2.shape[-3:]),
        g_ple=g_ple.astype(F32)[None, :], wp=w_ple.astype(BF16), wpg=w_ple_gate.astype(BF16),
        g_final=g_final.astype(F32)[None, :])


def _layer(x, pe, past, conv_prev, c0, n0, m0, wts, conv_w, conv_b, b_igate, b_fgate, mh_gain, qb, ln):
    bsz, t, _ = x.shape
    n = bsz * t
    x2 = x.reshape(n, D_MODEL)
    z, kvb, k_new, v_new, ki_new = _inproj(x2, wts["g_mix"], wts["w_pack"])
    z3 = z.reshape(bsz, t, D_PACK)
    kv3 = kvb.reshape(bsz, t, KV_PACK)
    att = _dsa(z3, kv3, past, qb, min(DSA_SB, qb))
    hm, conv_new, c_new, n_new, m_new = _mlstm(z3, conv_prev, c0, n0, m0, conv_w, conv_b,
                                               b_igate, b_fgate, mh_gain, ln)
    x1, h2, rinfo = _post(att.reshape(n, ATT_WIDTH), hm.reshape(n, M_WIDTH), z, x2,
                          wts["wa"], wts["wb"], wts["wo"], wts["g_ffn"], wts["wr"], wts["br"])
    n_tiles = (2 * n + N_EXPERTS * (MOE_TM - 1) + MOE_TM - 1) // MOE_TM
    tile_expert, n_used, row_token, pos = _route(rinfo, n_tiles)
    y_sorted = _moe(h2, tile_expert, n_used, row_token, wts["w1"], wts["w3"], wts["w2"])
    y = _final(x1, rinfo, pos, y_sorted, pe.reshape(n, PLE_DIM).astype(F32),
               wts["g_ple"], wts["wpg"], wts["wp"], wts["g_final"])
    k_new = k_new.reshape(bsz, t, N_KV_HEADS, HEAD_DIM)
    v_new = v_new.reshape(bsz, t, N_KV_HEADS, HEAD_DIM)
    ki_new = ki_new.reshape(bsz, t, IDX_DIM)
    return y.reshape(bsz, t, D_MODEL), (k_new, v_new, ki_new, conv_new, c_new, n_new, m_new)


def kernel(x_prompt, x_sample, cache_k, cache_v, cache_kidx, state_conv, state_C, state_n, state_m,
           p_prompt, p_sample, g_mix, w_in, conv_w, conv_b, b_igate, b_fgate, mh_gain,
           w_att_out, w_mlstm_out, w_out, g_ffn, router_gw, router_gb, router_ew, router_eb,
           w1, w3, w2, g_ple, w_ple, w_ple_gate, g_final):
    assert g_mix.shape[0] == 1, "single-layer step"
    bp, tp, _ = x_prompt.shape
    bs, ts, _ = x_sample.shape
    sdt = state_C.dtype
    wts = _prep_weights(g_mix[0], w_in[0], w_att_out[0], w_mlstm_out[0], w_out[0], g_ffn[0],
                        router_gw[0], router_gb[0], router_ew[0], router_eb[0], w1, w3, w2,
                        g_ple[0], w_ple[0], w_ple_gate[0], g_final)
    mix = (conv_w[0], conv_b[0], b_igate[0], b_fgate[0], mh_gain[0])
    yp, st_p = _layer(
        x_prompt, p_prompt[0], None,
        jnp.zeros((bp, CONV_W - 1, 2 * M_WIDTH), F32),
        jnp.zeros((bp, M_HEADS, M_HEAD_DIM, M_HEAD_DIM), F32),
        jnp.zeros((bp, M_HEADS, M_HEAD_DIM), F32),
        jnp.zeros((bp, M_HEADS), F32),
        wts, *mix, qb=min(DSA_QB, tp), ln=min(256, tp))
    plen = cache_k.shape[2]
    past = (cache_k[0].reshape(bs, plen, KV_WIDTH).astype(BF16),
            cache_v[0].reshape(bs, plen, KV_WIDTH).astype(BF16), cache_kidx[0])
    ys, st_s = _layer(
        x_sample, p_sample[0], past, state_conv[0], state_C[0], state_n[0], state_m[0],
        wts, *mix, qb=ts, ln=ts)
    outs_p = [s[None] for s in st_p]
    outs_s = [s[None] for s in st_s]
    for lst in (outs_p, outs_s):
        for k in (4, 5, 6):
            lst[k] = lst[k].astype(sdt)
    return (yp, ys, *outs_p, *outs_s)
```

```python
import functools

import numpy as np
import jax
import jax.numpy as jnp
from jax import lax
from jax.experimental import pallas as pl
from jax.experimental.pallas import tpu as pltpu

F32 = jnp.float32
BF16 = jnp.bfloat16
I32 = jnp.int32

D_MODEL = 2048
CHUNK = 64
CHUNK_SHIFT = 6
assert 1 << CHUNK_SHIFT == CHUNK
N_HEADS = 8
N_KV_HEADS = 4
HEAD_DIM = 128
ATT_WIDTH = N_HEADS * HEAD_DIM
KV_WIDTH = N_KV_HEADS * HEAD_DIM
IDX_HEADS = 16
IDX_DIM = 64
TOPK_MAX = 256
M_HEADS = 4
M_HEAD_DIM = 256
M_WIDTH = M_HEADS * M_HEAD_DIM
CONV_W = 4
N_GROUPS = 4
EXP_PER_GROUP = 8
N_EXPERTS = N_GROUPS * EXP_PER_GROUP
D_EXPERT = 512
PLE_DIM = 256
EPS = 1e-6
PROJ_SIZES = (ATT_WIDTH, KV_WIDTH, KV_WIDTH, IDX_HEADS * IDX_DIM, IDX_DIM, IDX_HEADS,
              M_WIDTH, M_WIDTH, M_WIDTH, M_HEADS, M_HEADS, M_WIDTH, D_MODEL, D_MODEL)

LANES = 128
SUBLANES = 8
VMEM_LIMIT = 56 * 1024 * 1024

OFF_QA, OFF_QI, OFF_QM, OFF_KM, OFF_VM, OFF_OG = 0, 1024, 2048, 3072, 4096, 5120
OFF_GA, OFF_GB, OFF_KA, OFF_VA, OFF_SM = 6144, 8192, 10240, 10752, 11264
SM_WIDTH = 512
D_PACK = OFF_SM + SM_WIDTH
SM_KI, SM_WI, SM_IG, SM_FG = 0, 64, 80, 84
PROJ_TN = 512
KV_BLK0 = OFF_KA // PROJ_TN
KV_PACK = D_PACK - OFF_KA

LOG2E = 1.4426950408889634
INT_MIN = np.int32(-2 ** 31)
NEG_BIG = -1e30

DSA_KT = 512
DSA_QB = 128
DSA_SB = 128
MOE_TM = 512
MOE_TM_SMALL = 128
TOK_TM = 256


def _cparams(n_axes):
    return pltpu.CompilerParams(dimension_semantics=("arbitrary",) * n_axes,
                                vmem_limit_bytes=VMEM_LIMIT)


def _dot(a, b):
    return jnp.dot(a, b, preferred_element_type=F32)


def _dot_nt(a, b):
    return lax.dot_general(a, b, (((1,), (1,)), ((), ())), preferred_element_type=F32)


def _dot_tn(a, b):
    return lax.dot_general(a, b, (((0,), (0,)), ((), ())), preferred_element_type=F32)


def _tile_rows(t, size):
    start = t * size
    return pl.ds(start if isinstance(start, int) else pl.multiple_of(start, size), size)


def _head_rows(t, size, g):
    start = t * (size * N_KV_HEADS)
    if not isinstance(start, int):
        start = pl.multiple_of(start, size * N_KV_HEADS)
    return pl.ds(start + g, size, stride=N_KV_HEADS)


def _sigmoid(x):
    return 1.0 / (1.0 + jnp.exp(-x))


def _split3(x):
    hi = x.astype(BF16)
    r1 = x - hi.astype(F32)
    mid = r1.astype(BF16)
    lo = (r1 - mid.astype(F32)).astype(BF16)
    return hi, mid, lo


def _inproj_kernel(x_ref, g_ref, w_ref, z_ref, kv_ref, k_ref, v_ref, ki_ref, h_ref):
    j = pl.program_id(1)

    @pl.when(j == 0)
    def _():
        x = x_ref[...]
        r = lax.rsqrt(jnp.mean(x * x, axis=-1, keepdims=True) + EPS)
        h_ref[...] = (x * r * g_ref[...]).astype(BF16)

    acc = _dot(h_ref[...], w_ref[...])
    z_ref[...] = acc

    @pl.when(j >= KV_BLK0)
    def _():
        kv_ref[...] = acc.astype(BF16)

    tm = acc.shape[0]

    def store_heads(ref):
        for g in range(N_KV_HEADS):
            ref[pl.ds(g, tm, stride=N_KV_HEADS), :] = acc[:, g * HEAD_DIM:(g + 1) * HEAD_DIM]

    @pl.when(j == KV_BLK0)
    def _():
        store_heads(k_ref)

    @pl.when(j == KV_BLK0 + 1)
    def _():
        store_heads(v_ref)

    @pl.when(j == KV_BLK0 + 2)
    def _():
        ki_ref[...] = acc[:, SM_KI:SM_KI + IDX_DIM]


def _inproj(x2d, g, w_pack):
    n = x2d.shape[0]
    tm = min(1024, n)
    grid = (n // tm, D_PACK // PROJ_TN)
    return pl.pallas_call(
        _inproj_kernel,
        grid=grid,
        in_specs=[
            pl.BlockSpec((tm, D_MODEL), lambda i, j: (i, 0)),
            pl.BlockSpec((1, D_MODEL), lambda i, j: (0, 0)),
            pl.BlockSpec((D_MODEL, PROJ_TN), lambda i, j: (0, j)),
        ],
        out_specs=[
            pl.BlockSpec((tm, PROJ_TN), lambda i, j: (i, j)),
            pl.BlockSpec((tm, PROJ_TN), lambda i, j: (i, jnp.maximum(j - KV_BLK0, 0))),
            pl.BlockSpec((tm * N_KV_HEADS, HEAD_DIM), lambda i, j: (i, 0)),
            pl.BlockSpec((tm * N_KV_HEADS, HEAD_DIM), lambda i, j: (i, 0)),
            pl.BlockSpec((tm, IDX_DIM), lambda i, j: (i, 0)),
        ],
        out_shape=[jax.ShapeDtypeStruct((n, D_PACK), F32),
                   jax.ShapeDtypeStruct((n, KV_PACK), BF16),
                   jax.ShapeDtypeStruct((n * N_KV_HEADS, HEAD_DIM), F32),
                   jax.ShapeDtypeStruct((n * N_KV_HEADS, HEAD_DIM), F32),
                   jax.ShapeDtypeStruct((n, IDX_DIM), F32)],
        scratch_shapes=[pltpu.VMEM((tm, D_MODEL), BF16)],
        compiler_params=_cparams(2),
        name="inproj",
    )(x2d, g, w_pack)


def _dsa_kernel(*refs, has_past, qblk, sb, n_past_tiles, n_new_tiles, t_valid, past_len, topk):
    n_in = 9 if has_past else 6
    if has_past:
        qa_ref, qi_ref, sm_ref, kip_ref, kp_ref, vp_ref, kin_ref, kn_ref, vn_ref = refs[:n_in]
    else:
        qa_ref, qi_ref, sm_ref, kin_ref, kn_ref, vn_ref = refs[:n_in]
    o_ref, keys_ref, d_ref, s_ref = refs[n_in:n_in + 4]
    state = refs[n_in + 4:]
    mx_refs, acc_refs = state[0::2], state[1::2]
    kt = DSA_KT
    nsub = qblk // sb
    i = pl.program_id(1)
    q0 = past_len + i * qblk
    j_end = ((q0 + qblk - 1) // CHUNK + 1) * CHUNK - past_len
    nk_new = jnp.minimum((j_end + kt - 1) // kt, n_new_tiles)
    n_tiles = n_past_tiles + nk_new
    lane_sb = lax.broadcasted_iota(I32, (sb, kt), 1)

    for sub in range(nsub):
        r0 = sub * sb
        qchunk = (q0 + r0 + lax.broadcasted_iota(I32, (sb, 1), 0)) >> CHUNK_SHIFT
        wsc = sm_ref[0, r0:r0 + sb, SM_WI:SM_WI + IDX_HEADS] * (IDX_HEADS ** -0.5 * IDX_DIM ** -0.5)
        wcols = [wsc[:, h:h + 1] for h in range(IDX_HEADS)]
        qi_all = qi_ref[0, 0, sub * IDX_HEADS * sb:(sub + 1) * IDX_HEADS * sb]

        def score_tile(ki_t, kpos0, jvalid0, col, r0=r0, qchunk=qchunk, wcols=wcols, qi_all=qi_all):
            d_ref[...] = _dot_nt(qi_all, ki_t)
            acc = wcols[0] * jnp.maximum(d_ref[0:sb], 0.0)
            for h in range(1, IDX_HEADS):
                acc = acc + wcols[h] * jnp.maximum(d_ref[h * sb:(h + 1) * sb], 0.0)
            bits = lax.bitcast_convert_type(acc + 0.0, I32)
            key = bits ^ ((bits >> 31) & np.int32(0x7FFFFFFF))
            adm = (((kpos0 + lane_sb) >> CHUNK_SHIFT) <= qchunk) & (jvalid0 + lane_sb < t_valid)
            keys_ref[col, r0:r0 + sb, :] = jnp.where(adm, key, INT_MIN)

        if has_past:
            def p1_past(t, c, score_tile=score_tile):
                ki_t = kip_ref[0, _tile_rows(t, kt), :].astype(BF16)
                score_tile(ki_t, t * kt, -(2 ** 30), t)
                return c
            lax.fori_loop(0, n_past_tiles, p1_past, 0)

        def p1_new(t, c, score_tile=score_tile):
            ki_t = kin_ref[0, _tile_rows(t, kt), 0:IDX_DIM]
            score_tile(ki_t, past_len + t * kt, t * kt, n_past_tiles + t)
            return c
        lax.fori_loop(0, nk_new, p1_new, 0)

    qb = qblk
    lane = lax.broadcasted_iota(I32, (qb, kt), 1)

    def count(fn):
        def body(t, part):
            kall = keys_ref[t]
            for s in range(kt // LANES):
                ks = kall[:, s * LANES:(s + 1) * LANES]
                part = part + jnp.where(fn(ks, t, s), 1.0, 0.0)
            return part
        part = lax.fori_loop(0, n_tiles, body, jnp.zeros((qb, LANES), F32))
        return jnp.sum(part, axis=1, keepdims=True)

    kf = float(topk)

    bits_per_check = 4

    def sgroup(carry):
        grp, tu, done, _ = carry
        for b in range(bits_per_check):
            shift = jnp.asarray(31 - b, I32) - grp * bits_per_check
            cand = tu | lax.shift_left(np.int32(1), shift)
            cs = cand ^ INT_MIN
            cnt = count(lambda k, t, s: k >= cs)
            tu = jnp.where((cnt >= kf) & (done == 0.0), cand, tu)
            done = jnp.where(cnt == kf, 1.0, done)
        return grp + 1, tu, done, jnp.min(done)

    def scond(carry):
        grp, _, _, all_done = carry
        return (grp < 32 // bits_per_check) & (all_done == 0.0)

    _, tu, _, _ = lax.while_loop(
        scond, sgroup, (jnp.int32(0), jnp.zeros((qb, 1), I32), jnp.zeros((qb, 1), F32), jnp.float32(0.0)))
    thr = tu ^ INT_MIN
    cnt_ge = count(lambda k, t, s: k >= thr)
    cnt_gt = count(lambda k, t, s: k > thr)
    tie = (cnt_ge > kf) & (thr > INT_MIN)
    need = kf - cnt_gt

    @pl.when(jnp.max(jnp.where(tie, 1.0, 0.0)) > 0.0)
    def _():
        lane1 = lax.broadcasted_iota(I32, (qb, LANES), 1)

        def jbody(it, a):
            cand = a | lax.shift_left(np.int32(1), jnp.asarray(15 - it, I32))
            cnt = count(lambda k, t, s: (k == thr) & (t * kt + s * LANES + lane1 < cand))
            return jnp.where(cnt < need, cand, a)
        a = lax.fori_loop(0, 16, jbody, jnp.zeros((qb, 1), I32))

        def drop(t, c):
            k = keys_ref[t]
            keys_ref[t] = jnp.where(tie & (k == thr) & (t * kt + lane > a), INT_MIN, k)
            return c
        lax.fori_loop(0, n_tiles, drop, 0)

    thr_eff = jnp.maximum(thr, INT_MIN + 1)

    def hs(g):
        return slice(g * HEAD_DIM, (g + 1) * HEAD_DIM)

    ones_blk = jnp.ones((kt, HEAD_DIM), BF16)

    for sub in range(nsub):
        r0 = sub * sb
        thr_sub = thr_eff[r0:r0 + sb]
        q = qa_ref[0, r0:r0 + sb, :] * (HEAD_DIM ** -0.5 * LOG2E)
        q2 = []
        for g in range(N_KV_HEADS):
            a = q[:, (2 * g) * HEAD_DIM:(2 * g + 1) * HEAD_DIM]
            b = q[:, (2 * g + 1) * HEAD_DIM:(2 * g + 2) * HEAD_DIM]
            q2.append(jnp.concatenate([a, b], axis=0).astype(BF16))
        for g in range(N_KV_HEADS):
            mx_refs[g][...] = jnp.full(mx_refs[g].shape, -jnp.inf, F32)
            acc_refs[g][...] = jnp.zeros(acc_refs[g].shape, F32)

        def logits(col, k_fn, r0=r0, thr_sub=thr_sub, q2=q2):
            bias = jnp.where(keys_ref[col, r0:r0 + sb, :] >= thr_sub, 0.0, NEG_BIG)
            bias2 = jnp.concatenate([bias, bias], axis=0)
            for g in range(N_KV_HEADS):
                s = _dot_nt(q2[g], k_fn(g)) + bias2
                s_ref[g, col] = s
                mx = mx_refs[g][...]
                for c in range(kt // LANES):
                    mx = jnp.maximum(mx, s[:, c * LANES:(c + 1) * LANES])
                mx_refs[g][...] = mx

        def weighted(col, v_fn, m_rows):
            for g in range(N_KV_HEADS):
                p = jnp.exp2(s_ref[g, col] - m_rows[g]).astype(BF16)
                v_aug = jnp.concatenate([v_fn(g), ones_blk], axis=1)
                acc_refs[g][...] = acc_refs[g][...] + _dot(p, v_aug)

        if has_past:
            def pa_past(t, c, logits=logits):
                rows = _tile_rows(t, kt)
                logits(t, lambda g: kp_ref[0, _head_rows(t, kt, g), :].astype(BF16))
                return c
            lax.fori_loop(0, n_past_tiles, pa_past, 0)

        def pa_new(t, c, logits=logits):
            rows = _tile_rows(t, kt)
            logits(n_past_tiles + t, lambda g: kn_ref[0, rows, hs(g)])
            return c
        lax.fori_loop(0, nk_new, pa_new, 0)

        m_rows = [jnp.max(mx_refs[g][...], axis=1, keepdims=True) for g in range(N_KV_HEADS)]

        if has_past:
            def pb_past(t, c, m_rows=m_rows, weighted=weighted):
                rows = _tile_rows(t, kt)
                weighted(t, lambda g: vp_ref[0, _head_rows(t, kt, g), :].astype(BF16), m_rows)
                return c
            lax.fori_loop(0, n_past_tiles, pb_past, 0)

        def pb_new(t, c, m_rows=m_rows, weighted=weighted):
            rows = _tile_rows(t, kt)
            weighted(n_past_tiles + t, lambda g: vn_ref[0, rows, hs(g)], m_rows)
            return c
        lax.fori_loop(0, nk_new, pb_new, 0)

        for g in range(N_KV_HEADS):
            acc = acc_refs[g][...]
            o = acc[:, 0:HEAD_DIM] / acc[:, HEAD_DIM:2 * HEAD_DIM]
            o_ref[0, r0:r0 + sb, (2 * g) * HEAD_DIM:(2 * g + 1) * HEAD_DIM] = o[0:sb].astype(BF16)
            o_ref[0, r0:r0 + sb, (2 * g + 1) * HEAD_DIM:(2 * g + 2) * HEAD_DIM] = o[sb:2 * sb].astype(BF16)


def _dsa(z3, kv3, past, qb, sb):
    bsz, t, _ = z3.shape
    kt = DSA_KT
    nb = t // qb
    nsub = qb // sb
    has_past = past is not None
    past_len = past[0].shape[1] if has_past else 0
    topk = min(TOPK_MAX, (past_len + t) // 4)
    t_pad = -(-t // kt) * kt
    if t_pad != t:
        kv3 = jnp.pad(kv3, ((0, 0), (0, t_pad - t), (0, 0)))
    n_new_tiles = t_pad // kt
    n_past_tiles = past_len // kt
    qi = z3[:, :, OFF_QI:OFF_QI + IDX_HEADS * IDX_DIM].astype(BF16)
    qi = qi.reshape(bsz, nb, nsub, sb, IDX_HEADS, IDX_DIM).transpose(0, 1, 2, 4, 3, 5)
    qi = qi.reshape(bsz, nb, IDX_HEADS * qb, IDX_DIM)

    in_specs = [
        pl.BlockSpec((1, qb, ATT_WIDTH), lambda b, i: (b, i, OFF_QA // ATT_WIDTH)),
        pl.BlockSpec((1, 1, IDX_HEADS * qb, IDX_DIM), lambda b, i: (b, i, 0, 0)),
        pl.BlockSpec((1, qb, LANES), lambda b, i: (b, i, OFF_SM // LANES)),
    ]
    args = [z3, qi, z3]
    if has_past:
        pk, pv, pki = past
        pk = pk.reshape(bsz, past_len * N_KV_HEADS, HEAD_DIM)
        pv = pv.reshape(bsz, past_len * N_KV_HEADS, HEAD_DIM)
        in_specs += [
            pl.BlockSpec((1, past_len, IDX_DIM), lambda b, i: (b, 0, 0)),
            pl.BlockSpec((1, past_len * N_KV_HEADS, HEAD_DIM), lambda b, i: (b, 0, 0)),
            pl.BlockSpec((1, past_len * N_KV_HEADS, HEAD_DIM), lambda b, i: (b, 0, 0)),
        ]
        args += [pki, pk, pv]
    in_specs += [
        pl.BlockSpec((1, t_pad, LANES), lambda b, i: (b, 0, 2 * KV_WIDTH // LANES)),
        pl.BlockSpec((1, t_pad, KV_WIDTH), lambda b, i: (b, 0, 0)),
        pl.BlockSpec((1, t_pad, KV_WIDTH), lambda b, i: (b, 0, 1)),
    ]
    args += [kv3, kv3, kv3]
    kern = functools.partial(_dsa_kernel, has_past=has_past, qblk=qb, sb=sb, n_past_tiles=n_past_tiles,
                             n_new_tiles=n_new_tiles, t_valid=t, past_len=past_len, topk=topk)
    return pl.pallas_call(
        kern,
        grid=(bsz, nb),
        in_specs=in_specs,
        out_specs=pl.BlockSpec((1, qb, ATT_WIDTH), lambda b, i: (b, i, 0)),
        out_shape=jax.ShapeDtypeStruct((bsz, t, ATT_WIDTH), BF16),
        scratch_shapes=[pltpu.VMEM((n_past_tiles + n_new_tiles, qb, kt), I32),
                        pltpu.VMEM((IDX_HEADS * sb, kt), F32),
                        pltpu.VMEM((N_KV_HEADS, n_past_tiles + n_new_tiles, 2 * sb, kt), F32)] + [
            pltpu.VMEM((2 * sb, LANES), F32), pltpu.VMEM((2 * sb, 2 * HEAD_DIM), F32)] * N_KV_HEADS,
        compiler_params=_cparams(2),
        name="dsa_past" if has_past else "dsa",
    )(*args)


def _mlstm_kernel(qk_ref, vm_ref, og_ref, sm_ref, gt_ref, cprev_ref, c0_ref, n0_ref, m0_ref,
                  cw_ref, cb_ref, gbrow_ref, gbcol_ref, gain_ref,
                  hm_ref, cnew_ref, cout_ref, nout_ref, mout_ref,
                  ubuf, c_s, n_s, m_s, *, ln):
    c = pl.program_id(1)
    nc = pl.num_programs(1)
    pad = SUBLANES

    @pl.when(c == 0)
    def _():
        ubuf[0:pad] = cprev_ref[0]
        c_s[...] = c0_ref[0]
        n_s[...] = n0_ref[0]
        m_s[...] = m0_ref[0]

    @pl.when(c > 0)
    def _():
        ubuf[0:pad] = ubuf[ln:ln + pad]

    ubuf[pad:pad + ln] = qk_ref[0]
    y = cb_ref[...] + ubuf[pad - 3:pad - 3 + ln] * cw_ref[0:1]
    for j in range(1, CONV_W):
        y = y + ubuf[pad - 3 + j:pad - 3 + j + ln] * cw_ref[j:j + 1]
    qkc = y * _sigmoid(y)
    cnew_ref[0] = ubuf[ln:ln + pad]

    pre_c = sm_ref[0] + gbrow_ref[...]
    pre_r = gt_ref[0, 0] + gbcol_ref[...][:, 0:1]

    def log_sigmoid(x):
        return jnp.minimum(x, 0.0) - jnp.log(1.0 + jnp.exp(-jnp.abs(x)))

    lf_c = log_sigmoid(pre_c)
    lf_r = log_sigmoid(pre_r)
    ri = lax.broadcasted_iota(I32, (ln, ln), 0)
    ci = lax.broadcasted_iota(I32, (ln, ln), 1)
    causal = ci <= ri
    tril = jnp.where(causal, 1.0, 0.0).astype(BF16)
    triu = jnp.where(ri <= ci, 1.0, 0.0).astype(BF16)
    b_c = sum(_dot(tril, p) for p in _split3(lf_c))
    b_r = sum(_dot(p, triu) for p in _split3(lf_r))

    vm = vm_ref[0]
    og = og_ref[0]
    for h in range(M_HEADS):
        hs = slice(h * M_HEAD_DIM, (h + 1) * M_HEAD_DIM)
        qf = qkc[:, hs]
        kf = qkc[:, M_WIDTH + h * M_HEAD_DIM:M_WIDTH + (h + 1) * M_HEAD_DIM] * (M_HEAD_DIM ** -0.5)
        qb16 = qf.astype(BF16)
        kb16 = kf.astype(BF16)
        vb16 = vm[:, hs].astype(BF16)
        bcol = b_c[:, SM_FG + h:SM_FG + h + 1]
        igcol = pre_c[:, SM_IG + h:SM_IG + h + 1]
        brow = b_r[M_HEADS + h:M_HEADS + h + 1, :]
        igrow = pre_r[h:h + 1, :]
        blast = bcol[ln - 1:ln, :]
        m_prev = m_s[h:h + 1, 0:1]
        dmat = jnp.where(causal, bcol - brow + igrow, -jnp.inf)
        m_inter = bcol + m_prev
        m_i = jnp.maximum(m_inter, jnp.max(dmat, axis=1, keepdims=True))
        s = _dot_nt(qb16, kb16) * jnp.exp(dmat - m_i)
        scale = jnp.exp(m_inter - m_i)
        c_prev = c_s[h]
        n_prev = n_s[h:h + 1, :]
        num = _dot(s.astype(BF16), vb16) + scale * _dot(qb16, c_prev.astype(BF16))
        den = jnp.sum(s, axis=1, keepdims=True) + scale * jnp.sum(qf * n_prev, axis=1, keepdims=True)
        hh = num / jnp.maximum(jnp.abs(den), jnp.exp(-m_i))
        m_new = m_i[ln - 1:ln, :]
        decay = jnp.exp(blast + m_prev - m_new)
        wcol = jnp.exp(blast - bcol + igcol - m_new)
        kw = kf * wcol
        c_s[h] = decay * c_prev + _dot_tn(kw.astype(BF16), vb16)
        n_s[h:h + 1, :] = decay * n_prev + jnp.sum(kw, axis=0, keepdims=True)
        m_s[h:h + 1, :] = jnp.broadcast_to(m_new, (1, LANES))
        hn = hh * lax.rsqrt(jnp.mean(hh * hh, axis=1, keepdims=True) + EPS) * gain_ref[:, hs]
        hm_ref[0, :, hs] = (hn * _sigmoid(og[:, hs])).astype(BF16)

    @pl.when(c == nc - 1)
    def _():
        cout_ref[0] = c_s[...]
        nout_ref[0] = n_s[...]
        mout_ref[0] = m_s[...]


def _mlstm(z3, conv_prev, c0, n0, m0, conv_w, conv_b, b_igate, b_fgate, mh_gain, ln):
    bsz, t, _ = z3.shape
    nc = t // ln
    pad = SUBLANES
    gt = z3[:, :, OFF_SM + SM_IG:OFF_SM + SM_IG + 2 * M_HEADS]
    gt = gt.reshape(bsz, nc, ln, 2 * M_HEADS).transpose(0, 1, 3, 2)
    cprev = jnp.pad(conv_prev.astype(F32), ((0, 0), (pad - (CONV_W - 1), 0), (0, 0)))
    gbias = jnp.concatenate([b_igate, b_fgate]).astype(F32)
    gbrow = jnp.zeros((1, LANES), F32).at[0, SM_IG:SM_IG + 2 * M_HEADS].set(gbias)
    gbcol = jnp.broadcast_to(gbias[:, None], (2 * M_HEADS, LANES))
    m0b = jnp.broadcast_to(m0.astype(F32)[:, :, None], (bsz, M_HEADS, LANES))
    kern = functools.partial(_mlstm_kernel, ln=ln)
    full = lambda shape: pl.BlockSpec(shape, lambda b, c: (0,) * len(shape))
    hm, cnew, cout, nout, mout = pl.pallas_call(
        kern,
        grid=(bsz, nc),
        in_specs=[
            pl.BlockSpec((1, ln, 2 * M_WIDTH), lambda b, c: (b, c, OFF_QM // (2 * M_WIDTH))),
            pl.BlockSpec((1, ln, M_WIDTH), lambda b, c: (b, c, OFF_VM // M_WIDTH)),
            pl.BlockSpec((1, ln, M_WIDTH), lambda b, c: (b, c, OFF_OG // M_WIDTH)),
            pl.BlockSpec((1, ln, LANES), lambda b, c: (b, c, OFF_SM // LANES)),
            pl.BlockSpec((1, 1, 2 * M_HEADS, ln), lambda b, c: (b, c, 0, 0)),
            pl.BlockSpec((1, pad, 2 * M_WIDTH), lambda b, c: (b, 0, 0)),
            pl.BlockSpec((1, M_HEADS, M_HEAD_DIM, M_HEAD_DIM), lambda b, c: (b, 0, 0, 0)),
            pl.BlockSpec((1, M_HEADS, M_HEAD_DIM), lambda b, c: (b, 0, 0)),
            pl.BlockSpec((1, M_HEADS, LANES), lambda b, c: (b, 0, 0)),
            full((CONV_W, 2 * M_WIDTH)),
            full((1, 2 * M_WIDTH)),
            full((1, LANES)),
            full((2 * M_HEADS, LANES)),
            full((1, M_WIDTH)),
        ],
        out_specs=[
            pl.BlockSpec((1, ln, M_WIDTH), lambda b, c: (b, c, 0)),
            pl.BlockSpec((1, pad, 2 * M_WIDTH), lambda b, c: (b, 0, 0)),
            pl.BlockSpec((1, M_HEADS, M_HEAD_DIM, M_HEAD_DIM), lambda b, c: (b, 0, 0, 0)),
            pl.BlockSpec((1, M_HEADS, M_HEAD_DIM), lambda b, c: (b, 0, 0)),
            pl.BlockSpec((1, M_HEADS, LANES), lambda b, c: (b, 0, 0)),
        ],
        out_shape=[
            jax.ShapeDtypeStruct((bsz, t, M_WIDTH), BF16),
            jax.ShapeDtypeStruct((bsz, pad, 2 * M_WIDTH), F32),
            jax.ShapeDtypeStruct((bsz, M_HEADS, M_HEAD_DIM, M_HEAD_DIM), F32),
            jax.ShapeDtypeStruct((bsz, M_HEADS, M_HEAD_DIM), F32),
            jax.ShapeDtypeStruct((bsz, M_HEADS, LANES), F32),
        ],
        scratch_shapes=[
            pltpu.VMEM((ln + pad, 2 * M_WIDTH), F32),
            pltpu.VMEM((M_HEADS, M_HEAD_DIM, M_HEAD_DIM), F32),
            pltpu.VMEM((M_HEADS, M_HEAD_DIM), F32),
            pltpu.VMEM((M_HEADS, LANES), F32),
        ],
        compiler_params=_cparams(2),
        name="mlstm",
    )(z3, z3, z3, z3, gt, cprev, c0.astype(F32), n0.astype(F32), m0b,
      conv_w.astype(F32), conv_b.astype(F32)[None, :], gbrow, gbcol, mh_gain.astype(F32)[None, :])
    return hm, cnew[:, pad - (CONV_W - 1):, :], cout, nout, mout[:, :, 0]


def _post_kernel(att_ref, hm_ref, ga_ref, gb_ref, x_ref, wa_ref, wb_ref, wo_ref, gf_ref, wr_ref, br_ref,
                 x1_ref, h2_ref, ri_ref):
    ya = _dot(att_ref[...], wa_ref[...])
    yb = _dot(hm_ref[...], wb_ref[...])
    mixed = _sigmoid(ga_ref[...]) * ya + _sigmoid(gb_ref[...]) * yb
    x1 = x_ref[...] + _dot(mixed.astype(BF16), wo_ref[...])
    x1_ref[...] = x1
    h2 = x1 * lax.rsqrt(jnp.mean(x1 * x1, axis=-1, keepdims=True) + EPS) * gf_ref[...]
    h2_ref[...] = h2
    lg = jnp.dot(h2, wr_ref[...], preferred_element_type=F32, precision=lax.Precision.HIGHEST) + br_ref[...]
    lane = lax.broadcasted_iota(I32, lg.shape, 1)
    glog = jnp.where(lane < N_GROUPS, lg, -jnp.inf)
    gmax = jnp.max(glog, axis=1, keepdims=True)
    gstar = jnp.min(jnp.where(glog == gmax, lane, LANES), axis=1, keepdims=True)
    pg = 1.0 / jnp.sum(jnp.exp(glog - gmax), axis=1, keepdims=True)
    lo = N_GROUPS + EXP_PER_GROUP * gstar
    elog = jnp.where((lane >= lo) & (lane < lo + EXP_PER_GROUP), lg, -jnp.inf)
    v0 = jnp.max(elog, axis=1, keepdims=True)
    i0 = jnp.min(jnp.where(elog == v0, lane, LANES), axis=1, keepdims=True)
    elog2 = jnp.where(lane == i0, -jnp.inf, elog)
    v1 = jnp.max(elog2, axis=1, keepdims=True)
    i1 = jnp.min(jnp.where(elog2 == v1, lane, LANES), axis=1, keepdims=True)
    e1 = jnp.exp(v1 - v0)
    w0 = pg / (1.0 + e1)
    w1 = pg * e1 / (1.0 + e1)
    out = jnp.where(lane == 0, (i0 - N_GROUPS).astype(F32), 0.0)
    out = jnp.where(lane == 1, (i1 - N_GROUPS).astype(F32), out)
    out = jnp.where(lane == 2, w0, out)
    out = jnp.where(lane == 3, w1, out)
    ri_ref[...] = out


def _post(att2, hm2, z2, x2, wa, wb, wo, g_ffn, wr, br):
    n = x2.shape[0]
    tm = TOK_TM
    row = lambda w: pl.BlockSpec((tm, w), lambda i: (i, 0))
    full = lambda shape: pl.BlockSpec(shape, lambda i: (0,) * len(shape))
    return pl.pallas_call(
        _post_kernel,
        grid=(n // tm,),
        in_specs=[
            row(ATT_WIDTH), row(M_WIDTH),
            pl.BlockSpec((tm, D_MODEL), lambda i: (i, OFF_GA // D_MODEL)),
            pl.BlockSpec((tm, D_MODEL), lambda i: (i, OFF_GB // D_MODEL)),
            row(D_MODEL),
            full((ATT_WIDTH, D_MODEL)), full((M_WIDTH, D_MODEL)), full((D_MODEL, D_MODEL)),
            full((1, D_MODEL)), full((D_MODEL, LANES)), full((1, LANES)),
        ],
        out_specs=[row(D_MODEL), row(D_MODEL), row(LANES)],
        out_shape=[jax.ShapeDtypeStruct((n, D_MODEL), F32),
                   jax.ShapeDtypeStruct((n, D_MODEL), F32),
                   jax.ShapeDtypeStruct((n, LANES), F32)],
        compiler_params=_cparams(1),
        name="post",
    )(att2, hm2, z2, z2, x2, wa, wb, wo, g_ffn, wr, br)


def _row_gather_start(idx_ref, n_rows, src_hbm, dst_buf, sem):
    def body(r, c):
        tok = idx_ref[0, 0, r]
        pltpu.make_async_copy(src_hbm.at[pl.ds(tok, 1)], dst_buf.at[pl.ds(r, 1)], sem).start()
        return c
    lax.fori_loop(0, n_rows, body, 0, unroll=8)


def _row_gather_wait(n_rows, src_hbm, dst_buf, sem):
    pltpu.make_async_copy(src_hbm.at[pl.ds(0, n_rows)], dst_buf, sem).wait()


def _moe_kernel(te_ref, nu_ref, tok_ref, tokn_ref, h2_hbm, w1_ref, w3_ref, w2_ref, y_ref,
                xbuf, sem, w1b, w3b, w2b, *, tm):
    t = pl.program_id(0)
    n_used = nu_ref[0]
    slot = t % 2

    @pl.when((t < n_used) & ((t == 0) | (te_ref[t] != te_ref[jnp.maximum(t - 1, 0)])))
    def _():
        w1b[...] = w1_ref[0, 0].astype(BF16)
        w3b[...] = w3_ref[0, 0].astype(BF16)
        w2b[...] = w2_ref[0, 0].astype(BF16)

    @pl.when(t == 0)
    def _():
        _row_gather_start(tok_ref, tm, h2_hbm, xbuf.at[0], sem.at[0])

    @pl.when(t + 1 < n_used)
    def _():
        _row_gather_start(tokn_ref, tm, h2_hbm, xbuf.at[1 - slot], sem.at[1 - slot])

    @pl.when(t < n_used)
    def _():
        _row_gather_wait(tm, h2_hbm, xbuf.at[slot], sem.at[slot])
        xb = xbuf[slot].astype(BF16)
        u = _dot(xb, w1b[...])
        a = (u * _sigmoid(u)) * _dot(xb, w3b[...])
        y_ref[...] = _dot(a.astype(BF16), w2b[...])

    @pl.when(t >= n_used)
    def _():
        y_ref[...] = jnp.zeros(y_ref.shape, F32)


def _moe(h2, tile_expert, n_used, row_token, w1, w3, w2, tm):
    n_tiles = tile_expert.shape[0]
    tok3 = row_token.reshape(n_tiles, 1, tm)
    grid_spec = pltpu.PrefetchScalarGridSpec(
        num_scalar_prefetch=2,
        grid=(n_tiles,),
        in_specs=[
            pl.BlockSpec((1, 1, tm), lambda t, te, nu: (t, 0, 0), memory_space=pltpu.SMEM),
            pl.BlockSpec((1, 1, tm), lambda t, te, nu: (jnp.minimum(t + 1, n_tiles - 1), 0, 0),
                         memory_space=pltpu.SMEM),
            pl.BlockSpec(memory_space=pl.ANY),
            pl.BlockSpec((1, 1, D_MODEL, D_EXPERT), lambda t, te, nu: (0, te[t], 0, 0)),
            pl.BlockSpec((1, 1, D_MODEL, D_EXPERT), lambda t, te, nu: (0, te[t], 0, 0)),
            pl.BlockSpec((1, 1, D_EXPERT, D_MODEL), lambda t, te, nu: (0, te[t], 0, 0)),
        ],
        out_specs=pl.BlockSpec((tm, D_MODEL), lambda t, te, nu: (t, 0)),
        scratch_shapes=[pltpu.VMEM((2, tm, D_MODEL), F32), pltpu.SemaphoreType.DMA((2,)),
                        pltpu.VMEM((D_MODEL, D_EXPERT), BF16), pltpu.VMEM((D_MODEL, D_EXPERT), BF16),
                        pltpu.VMEM((D_EXPERT, D_MODEL), BF16)],
    )
    return pl.pallas_call(
        functools.partial(_moe_kernel, tm=tm),
        grid_spec=grid_spec,
        out_shape=jax.ShapeDtypeStruct((n_tiles * tm, D_MODEL), F32),
        compiler_params=_cparams(1),
        name="moe",
    )(tile_expert, n_used, tok3, tok3, h2, w1, w3, w2)


def _route(rinfo, n_tiles, tm):
    n = rinfo.shape[0]
    eid = rinfo[:, 0:2].astype(I32).reshape(-1)
    order = jnp.argsort(eid, stable=True).astype(I32)
    inv = jnp.argsort(order).astype(I32)
    onehot = eid[:, None] == jnp.arange(N_EXPERTS, dtype=I32)[None, :]
    counts = jnp.sum(onehot.astype(I32), axis=0)
    tiles_e = (counts + tm - 1) // tm
    tile_end = jnp.cumsum(tiles_e)
    row_start = (tile_end - tiles_e) * tm
    grp_start = jnp.cumsum(counts) - counts
    shift = row_start - grp_start
    pos = inv + jnp.sum(jnp.where(onehot, shift[None, :], 0), axis=1)
    n_used = tile_end[-1]
    tile_ids = jnp.arange(n_tiles, dtype=I32)
    te_raw = jnp.sum((tile_end[None, :] <= tile_ids[:, None]).astype(I32), axis=1)
    last_e = jnp.sum((tile_end <= n_used - 1).astype(I32))
    tile_expert = jnp.minimum(te_raw, last_e)
    te_c = jnp.minimum(te_raw, N_EXPERTS - 1)
    src = jnp.arange(n_tiles * tm, dtype=I32) - jnp.repeat(shift[te_c], tm)
    lo = jnp.repeat(grp_start[te_c], tm)
    hi = lo + jnp.repeat(counts[te_c], tm)
    valid = (src >= lo) & (src < hi)
    row_token = jnp.where(valid, order[jnp.clip(src, 0, 2 * n - 1)] // 2, 0)
    return tile_expert, n_used.reshape(1).astype(I32), row_token, pos.reshape(n, 2)


def _final_kernel(pos_ref, posn_ref, y_hbm, x1_ref, ri_ref, pe_ref, gp_ref, wpg_ref, wp_ref, gfin_ref,
                  o_ref, ybuf, sem):
    t = pl.program_id(0)
    nt = pl.num_programs(0)
    slot = t % 2
    tm = TOK_TM

    @pl.when(t == 0)
    def _():
        _row_gather_start(pos_ref, 2 * tm, y_hbm, ybuf.at[0], sem.at[0])

    @pl.when(t + 1 < nt)
    def _():
        _row_gather_start(posn_ref, 2 * tm, y_hbm, ybuf.at[1 - slot], sem.at[1 - slot])

    _row_gather_wait(2 * tm, y_hbm, ybuf.at[slot], sem.at[slot])
    ri = ri_ref[...]
    moe = ri[:, 2:3] * ybuf[slot, 0:tm] + ri[:, 3:4] * ybuf[slot, tm:2 * tm]
    x2 = x1_ref[...] + moe
    hp = x2 * lax.rsqrt(jnp.mean(x2 * x2, axis=-1, keepdims=True) + EPS) * gp_ref[...]
    gate = _sigmoid(_dot(hp.astype(BF16), wpg_ref[...]))
    x3 = x2 + gate * _dot(pe_ref[...].astype(BF16), wp_ref[...])
    o_ref[...] = x3 * lax.rsqrt(jnp.mean(x3 * x3, axis=-1, keepdims=True) + EPS) * gfin_ref[...]


def _final(x1, rinfo, pos, y_sorted, pe2, g_ple, wpg, wp, g_final):
    n = x1.shape[0]
    tm = TOK_TM
    nt = n // tm
    pos3 = pos.reshape(nt, tm, 2).transpose(0, 2, 1).reshape(nt, 1, 2 * tm)
    row = lambda w: pl.BlockSpec((tm, w), lambda i: (i, 0))
    full = lambda shape: pl.BlockSpec(shape, lambda i: (0,) * len(shape))
    return pl.pallas_call(
        _final_kernel,
        grid=(nt,),
        in_specs=[
            pl.BlockSpec((1, 1, 2 * tm), lambda i: (i, 0, 0), memory_space=pltpu.SMEM),
            pl.BlockSpec((1, 1, 2 * tm), lambda i: (jnp.minimum(i + 1, nt - 1), 0, 0),
                         memory_space=pltpu.SMEM),
            pl.BlockSpec(memory_space=pl.ANY),
            row(D_MODEL), row(LANES), row(PLE_DIM),
            full((1, D_MODEL)), full((D_MODEL, D_MODEL)), full((PLE_DIM, D_MODEL)), full((1, D_MODEL)),
        ],
        out_specs=row(D_MODEL),
        out_shape=jax.ShapeDtypeStruct((n, D_MODEL), F32),
        scratch_shapes=[pltpu.VMEM((2, 2 * tm, D_MODEL), F32), pltpu.SemaphoreType.DMA((2,))],
        compiler_params=_cparams(1),
        name="final",
    )(pos3, pos3, y_sorted, x1, rinfo, pe2, g_ple, wpg, wp, g_final)


def _pack_w_in(w):
    offs = np.concatenate([[0], np.cumsum(PROJ_SIZES)])
    qa, ka, va, qi, ki, wi, qm, km, vm, ig, fg, og, ga, gb = [w[:, offs[k]:offs[k + 1]] for k in range(14)]
    small = jnp.concatenate([ki, wi, ig, fg], axis=1)
    small = jnp.pad(small, ((0, 0), (0, SM_WIDTH - small.shape[1])))
    return jnp.concatenate([qa, qi, qm, km, vm, og, ga, gb, ka, va, small], axis=1).astype(BF16)


def _prep_weights(g_mix, w_in, w_att_out, w_mlstm_out, w_out, g_ffn, router_gw, router_gb, router_ew,
                  router_eb, w1, w3, w2, g_ple, w_ple, w_ple_gate, g_final):
    wr = jnp.concatenate([router_gw, router_ew], axis=1).astype(F32)
    wr = jnp.pad(wr, ((0, 0), (0, LANES - wr.shape[1])))
    br = jnp.concatenate([router_gb, router_eb]).astype(F32)
    br = jnp.pad(br, (0, LANES - br.shape[0]))[None, :]
    return dict(
        g_mix=g_mix.astype(F32)[None, :], w_pack=_pack_w_in(w_in),
        wa=w_att_out.astype(BF16), wb=w_mlstm_out.astype(BF16), wo=w_out.astype(BF16),
        g_ffn=g_ffn.astype(F32)[None, :], wr=wr, br=br,
        w1=w1.reshape((1,) + w1.shape[-3:]), w3=w3.reshape((1,) + w3.shape[-3:]),
        w2=w2.reshape((1,) + w2.shape[-3:]),
        g_ple=g_ple.astype(F32)[None, :], wp=w_ple.astype(BF16), wpg=w_ple_gate.astype(BF16),
        g_final=g_final.astype(F32)[None, :])


def _layer(x, pe, past, conv_prev, c0, n0, m0, wts, conv_w, conv_b, b_igate, b_fgate, mh_gain, qb, ln):
    bsz, t, _ = x.shape
    n = bsz * t
    x2 = x.reshape(n, D_MODEL)
    z, kvb, k_new, v_new, ki_new = _inproj(x2, wts["g_mix"], wts["w_pack"])
    z3 = z.reshape(bsz, t, D_PACK)
    kv3 = kvb.reshape(bsz, t, KV_PACK)
    att = _dsa(z3, kv3, past, qb, min(DSA_SB, qb))
    hm, conv_new, c_new, n_new, m_new = _mlstm(z3, conv_prev, c0, n0, m0, conv_w, conv_b,
                                               b_igate, b_fgate, mh_gain, ln)
    x1, h2, rinfo = _post(att.reshape(n, ATT_WIDTH), hm.reshape(n, M_WIDTH), z, x2,
                          wts["wa"], wts["wb"], wts["wo"], wts["g_ffn"], wts["wr"], wts["br"])
    tm = MOE_TM if 2 * n >= 4 * MOE_TM * N_EXPERTS else MOE_TM_SMALL
    n_tiles = (2 * n + N_EXPERTS * (tm - 1) + tm - 1) // tm
    tile_expert, n_used, row_token, pos = _route(rinfo, n_tiles, tm)
    y_sorted = _moe(h2, tile_expert, n_used, row_token, wts["w1"], wts["w3"], wts["w2"], tm)
    y = _final(x1, rinfo, pos, y_sorted, pe.reshape(n, PLE_DIM).astype(F32),
               wts["g_ple"], wts["wpg"], wts["wp"], wts["g_final"])
    k_new = k_new.reshape(bsz, t, N_KV_HEADS, HEAD_DIM)
    v_new = v_new.reshape(bsz, t, N_KV_HEADS, HEAD_DIM)
    ki_new = ki_new.reshape(bsz, t, IDX_DIM)
    return y.reshape(bsz, t, D_MODEL), (k_new, v_new, ki_new, conv_new, c_new, n_new, m_new)


def kernel(x_prompt, x_sample, cache_k, cache_v, cache_kidx, state_conv, state_C, state_n, state_m,
           p_prompt, p_sample, g_mix, w_in, conv_w, conv_b, b_igate, b_fgate, mh_gain,
           w_att_out, w_mlstm_out, w_out, g_ffn, router_gw, router_gb, router_ew, router_eb,
           w1, w3, w2, g_ple, w_ple, w_ple_gate, g_final):
    assert g_mix.shape[0] == 1, "single-layer step"
    bp, tp, _ = x_prompt.shape
    bs, ts, _ = x_sample.shape
    sdt = state_C.dtype
    wts = _prep_weights(g_mix[0], w_in[0], w_att_out[0], w_mlstm_out[0], w_out[0], g_ffn[0],
                        router_gw[0], router_gb[0], router_ew[0], router_eb[0], w1, w3, w2,
                        g_ple[0], w_ple[0], w_ple_gate[0], g_final)
    mix = (conv_w[0], conv_b[0], b_igate[0], b_fgate[0], mh_gain[0])
    yp, st_p = _layer(
        x_prompt, p_prompt[0], None,
        jnp.zeros((bp, CONV_W - 1, 2 * M_WIDTH), F32),
        jnp.zeros((bp, M_HEADS, M_HEAD_DIM, M_HEAD_DIM), F32),
        jnp.zeros((bp, M_HEADS, M_HEAD_DIM), F32),
        jnp.zeros((bp, M_HEADS), F32),
        wts, *mix, qb=min(DSA_QB, tp), ln=min(256, tp))
    plen = cache_k.shape[2]
    past = (cache_k[0], cache_v[0], cache_kidx[0])
    ys, st_s = _layer(
        x_sample, p_sample[0], past, state_conv[0], state_C[0], state_n[0], state_m[0],
        wts, *mix, qb=ts, ln=ts)
    outs_p = [s[None] for s in st_p]
    outs_s = [s[None] for s in st_s]
    for lst in (outs_p, outs_s):
        for k in (4, 5, 6):
            lst[k] = lst[k].astype(sdt)
    return (yp, ys, *outs_p, *outs_s)
```

```python
import functools

import numpy as np
import jax
import jax.numpy as jnp
from jax import lax
from jax.experimental import pallas as pl
from jax.experimental.pallas import tpu as pltpu

F32 = jnp.float32
BF16 = jnp.bfloat16
I32 = jnp.int32

D_MODEL = 2048
CHUNK = 64
CHUNK_SHIFT = 6
assert 1 << CHUNK_SHIFT == CHUNK
N_HEADS = 8
N_KV_HEADS = 4
HEAD_DIM = 128
ATT_WIDTH = N_HEADS * HEAD_DIM
KV_WIDTH = N_KV_HEADS * HEAD_DIM
IDX_HEADS = 16
IDX_DIM = 64
TOPK_MAX = 256
M_HEADS = 4
M_HEAD_DIM = 256
M_WIDTH = M_HEADS * M_HEAD_DIM
CONV_W = 4
N_GROUPS = 4
EXP_PER_GROUP = 8
N_EXPERTS = N_GROUPS * EXP_PER_GROUP
D_EXPERT = 512
PLE_DIM = 256
EPS = 1e-6
PROJ_SIZES = (ATT_WIDTH, KV_WIDTH, KV_WIDTH, IDX_HEADS * IDX_DIM, IDX_DIM, IDX_HEADS,
              M_WIDTH, M_WIDTH, M_WIDTH, M_HEADS, M_HEADS, M_WIDTH, D_MODEL, D_MODEL)

LANES = 128
SUBLANES = 8
VMEM_LIMIT = 56 * 1024 * 1024

OFF_QA, OFF_QI, OFF_QM, OFF_KM, OFF_VM, OFF_OG = 0, 1024, 2048, 3072, 4096, 5120
OFF_GA, OFF_GB, OFF_KA, OFF_VA, OFF_SM = 6144, 8192, 10240, 10752, 11264
SM_WIDTH = 512
D_PACK = OFF_SM + SM_WIDTH
SM_KI, SM_WI, SM_IG, SM_FG = 0, 64, 80, 84
PROJ_TN = 512
KV_BLK0 = OFF_KA // PROJ_TN
KV_PACK = D_PACK - OFF_KA

LOG2E = 1.4426950408889634
INT_MIN = np.int32(-2 ** 31)
NEG_BIG = -1e30

DSA_KT = 512
DSA_QB = 128
DSA_SB = 128
MOE_TM = 512
MOE_TM_SMALL = 128
TOK_TM = 256
POST_TM = 512


def _cparams(n_axes):
    return pltpu.CompilerParams(dimension_semantics=("arbitrary",) * n_axes,
                                vmem_limit_bytes=VMEM_LIMIT)


def _dot(a, b):
    return jnp.dot(a, b, preferred_element_type=F32)


def _dot_nt(a, b):
    return lax.dot_general(a, b, (((1,), (1,)), ((), ())), preferred_element_type=F32)


def _dot_tn(a, b):
    return lax.dot_general(a, b, (((0,), (0,)), ((), ())), preferred_element_type=F32)


def _tile_rows(t, size):
    start = t * size
    return pl.ds(start if isinstance(start, int) else pl.multiple_of(start, size), size)


def _head_rows(t, size, g):
    start = t * (size * N_KV_HEADS)
    if not isinstance(start, int):
        start = pl.multiple_of(start, size * N_KV_HEADS)
    return pl.ds(start + g, size, stride=N_KV_HEADS)


def _sigmoid(x):
    return 1.0 / (1.0 + jnp.exp(-x))


def _split3(x):
    hi = x.astype(BF16)
    r1 = x - hi.astype(F32)
    mid = r1.astype(BF16)
    lo = (r1 - mid.astype(F32)).astype(BF16)
    return hi, mid, lo


def _inproj_kernel(x_ref, g_ref, w_ref, z_ref, kv_ref, k_ref, v_ref, ki_ref, h_ref):
    j = pl.program_id(1)

    @pl.when(j == 0)
    def _():
        x = x_ref[...]
        r = lax.rsqrt(jnp.mean(x * x, axis=-1, keepdims=True) + EPS)
        h_ref[...] = (x * r * g_ref[...]).astype(BF16)

    acc = _dot(h_ref[...], w_ref[...])
    z_ref[...] = acc

    @pl.when(j >= KV_BLK0)
    def _():
        kv_ref[...] = acc.astype(BF16)

    tm = acc.shape[0]

    def store_heads(ref):
        for g in range(N_KV_HEADS):
            ref[pl.ds(g, tm, stride=N_KV_HEADS), :] = acc[:, g * HEAD_DIM:(g + 1) * HEAD_DIM]

    @pl.when(j == KV_BLK0)
    def _():
        store_heads(k_ref)

    @pl.when(j == KV_BLK0 + 1)
    def _():
        store_heads(v_ref)

    @pl.when(j == KV_BLK0 + 2)
    def _():
        ki_ref[...] = acc[:, SM_KI:SM_KI + IDX_DIM]


def _inproj(x2d, g, w_pack):
    n = x2d.shape[0]
    tm = min(1024, n)
    grid = (n // tm, D_PACK // PROJ_TN)
    return pl.pallas_call(
        _inproj_kernel,
        grid=grid,
        in_specs=[
            pl.BlockSpec((tm, D_MODEL), lambda i, j: (i, 0)),
            pl.BlockSpec((1, D_MODEL), lambda i, j: (0, 0)),
            pl.BlockSpec((D_MODEL, PROJ_TN), lambda i, j: (0, j)),
        ],
        out_specs=[
            pl.BlockSpec((tm, PROJ_TN), lambda i, j: (i, j)),
            pl.BlockSpec((tm, PROJ_TN), lambda i, j: (i, jnp.maximum(j - KV_BLK0, 0))),
            pl.BlockSpec((tm * N_KV_HEADS, HEAD_DIM), lambda i, j: (i, 0)),
            pl.BlockSpec((tm * N_KV_HEADS, HEAD_DIM), lambda i, j: (i, 0)),
            pl.BlockSpec((tm, IDX_DIM), lambda i, j: (i, 0)),
        ],
        out_shape=[jax.ShapeDtypeStruct((n, D_PACK), F32),
                   jax.ShapeDtypeStruct((n, KV_PACK), BF16),
                   jax.ShapeDtypeStruct((n * N_KV_HEADS, HEAD_DIM), F32),
                   jax.ShapeDtypeStruct((n * N_KV_HEADS, HEAD_DIM), F32),
                   jax.ShapeDtypeStruct((n, IDX_DIM), F32)],
        scratch_shapes=[pltpu.VMEM((tm, D_MODEL), BF16)],
        compiler_params=_cparams(2),
        name="inproj",
    )(x2d, g, w_pack)


def _dsa_kernel(*refs, has_past, qblk, sb, n_past_tiles, n_new_tiles, t_valid, past_len, topk):
    n_in = 9 if has_past else 6
    if has_past:
        qa_ref, qi_ref, sm_ref, kip_ref, kp_ref, vp_ref, kin_ref, kn_ref, vn_ref = refs[:n_in]
    else:
        qa_ref, qi_ref, sm_ref, kin_ref, kn_ref, vn_ref = refs[:n_in]
    o_ref, keys_ref, kT_ref, d_ref, s_ref = refs[n_in:n_in + 5]
    state = refs[n_in + 5:]
    mx_refs, acc_refs = state[0::2], state[1::2]
    kt = DSA_KT
    nsub = qblk // sb
    i = pl.program_id(1)
    q0 = past_len + i * qblk
    j_end = ((q0 + qblk - 1) // CHUNK + 1) * CHUNK - past_len
    nk_new = jnp.minimum((j_end + kt - 1) // kt, n_new_tiles)
    n_tiles = n_past_tiles + nk_new
    lane_sb = lax.broadcasted_iota(I32, (sb, kt), 1)

    for sub in range(nsub):
        r0 = sub * sb
        qchunk = (q0 + r0 + lax.broadcasted_iota(I32, (sb, 1), 0)) >> CHUNK_SHIFT
        wsc = sm_ref[0, r0:r0 + sb, SM_WI:SM_WI + IDX_HEADS] * (IDX_HEADS ** -0.5 * IDX_DIM ** -0.5)
        wcols = [wsc[:, h:h + 1] for h in range(IDX_HEADS)]
        qi_all = qi_ref[0, 0, sub * IDX_HEADS * sb:(sub + 1) * IDX_HEADS * sb]

        def score_tile(ki_t, kpos0, jvalid0, col, r0=r0, qchunk=qchunk, wcols=wcols, qi_all=qi_all):
            d_ref[...] = _dot_nt(qi_all, ki_t)
            acc = wcols[0] * jnp.maximum(d_ref[0:sb], 0.0)
            for h in range(1, IDX_HEADS):
                acc = acc + wcols[h] * jnp.maximum(d_ref[h * sb:(h + 1) * sb], 0.0)
            bits = lax.bitcast_convert_type(acc + 0.0, I32)
            key = bits ^ ((bits >> 31) & np.int32(0x7FFFFFFF))
            adm = (((kpos0 + lane_sb) >> CHUNK_SHIFT) <= qchunk) & (jvalid0 + lane_sb < t_valid)
            keys_ref[col, r0:r0 + sb, :] = jnp.where(adm, key, INT_MIN)

        if has_past:
            def p1_past(t, c, score_tile=score_tile):
                ki_t = kip_ref[0, _tile_rows(t, kt), :].astype(BF16)
                score_tile(ki_t, t * kt, -(2 ** 30), t)
                return c
            lax.fori_loop(0, n_past_tiles, p1_past, 0)

        def p1_new(t, c, score_tile=score_tile):
            ki_t = kin_ref[0, _tile_rows(t, kt), 0:IDX_DIM]
            score_tile(ki_t, past_len + t * kt, t * kt, n_past_tiles + t)
            return c
        lax.fori_loop(0, nk_new, p1_new, 0)

    qb = qblk
    lane = lax.broadcasted_iota(I32, (qb, kt), 1)

    def count(fn):
        def body(t, part):
            kall = keys_ref[t]
            for s in range(kt // LANES):
                ks = kall[:, s * LANES:(s + 1) * LANES]
                part = part + jnp.where(fn(ks, t, s), 1.0, 0.0)
            return part
        part = lax.fori_loop(0, n_tiles, body, jnp.zeros((qb, LANES), F32))
        return jnp.sum(part, axis=1, keepdims=True)

    kf = float(topk)

    bits_per_check = 4

    def search(count_ge, shape):
        def sgroup(carry):
            grp, tu, done, _ = carry
            for b in range(bits_per_check):
                shift = jnp.asarray(31 - b, I32) - grp * bits_per_check
                cand = tu | lax.shift_left(np.int32(1), shift)
                cnt = count_ge(cand ^ INT_MIN)
                tu = jnp.where((cnt >= kf) & (done == 0.0), cand, tu)
                done = jnp.where(cnt == kf, 1.0, done)
            return grp + 1, tu, done, jnp.min(done)

        def scond(carry):
            grp, _, _, all_done = carry
            return (grp < 32 // bits_per_check) & (all_done == 0.0)

        _, tu, _, _ = lax.while_loop(
            scond, sgroup, (jnp.int32(0), jnp.zeros(shape, I32), jnp.zeros(shape, F32), jnp.float32(0.0)))
        return tu ^ INT_MIN

    if qb % LANES == 0:
        def xpose(t, c):
            kT_ref[t] = lax.bitcast_convert_type(lax.bitcast_convert_type(keys_ref[t], F32).T, I32)
            return c
        lax.fori_loop(0, n_tiles, xpose, 0)

        acc_rows = 4 * SUBLANES

        def count_t(fn):
            def body(t, part):
                c = jnp.where(fn(kT_ref[t]), 1.0, 0.0)
                return part + jnp.sum(c.reshape(kt // acc_rows, acc_rows, qb), axis=0)
            part = lax.fori_loop(0, n_tiles, body, jnp.zeros((acc_rows, qb), F32))
            return jnp.sum(part, axis=0, keepdims=True)

        thr_t = search(lambda cs: count_t(lambda k: k >= cs), (1, qb))
        cnt_ge_t = count_t(lambda k: k >= thr_t)
        any_tie = jnp.max(jnp.where((cnt_ge_t > kf) & (thr_t > INT_MIN), 1.0, 0.0)) > 0.0
        eye = lax.broadcasted_iota(I32, (qb, qb), 0) == lax.broadcasted_iota(I32, (qb, qb), 1)
        hi = jnp.sum(jnp.where(eye, (thr_t >> 16).astype(F32), 0.0), axis=1, keepdims=True)
        lo = jnp.sum(jnp.where(eye, (thr_t & np.int32(0xFFFF)).astype(F32), 0.0), axis=1, keepdims=True)
        thr = lax.shift_left(hi.astype(I32), np.int32(16)) | lo.astype(I32)
    else:
        thr = search(lambda cs: count(lambda k, t, s: k >= cs), (qb, 1))
        cnt_ge0 = count(lambda k, t, s: k >= thr)
        any_tie = jnp.max(jnp.where((cnt_ge0 > kf) & (thr > INT_MIN), 1.0, 0.0)) > 0.0

    @pl.when(any_tie)
    def _():
        cnt_ge = count(lambda k, t, s: k >= thr)
        cnt_gt = count(lambda k, t, s: k > thr)
        tie = (cnt_ge > kf) & (thr > INT_MIN)
        need = kf - cnt_gt
        lane1 = lax.broadcasted_iota(I32, (qb, LANES), 1)

        def jbody(it, a):
            cand = a | lax.shift_left(np.int32(1), jnp.asarray(15 - it, I32))
            cnt = count(lambda k, t, s: (k == thr) & (t * kt + s * LANES + lane1 < cand))
            return jnp.where(cnt < need, cand, a)
        a = lax.fori_loop(0, 16, jbody, jnp.zeros((qb, 1), I32))

        def drop(t, c):
            k = keys_ref[t]
            keys_ref[t] = jnp.where(tie & (k == thr) & (t * kt + lane > a), INT_MIN, k)
            return c
        lax.fori_loop(0, n_tiles, drop, 0)

    thr_eff = jnp.maximum(thr, INT_MIN + 1)

    def hs(g):
        return slice(g * HEAD_DIM, (g + 1) * HEAD_DIM)

    ones_blk = jnp.ones((kt, HEAD_DIM), BF16)

    for sub in range(nsub):
        r0 = sub * sb
        thr_sub = thr_eff[r0:r0 + sb]
        q = qa_ref[0, r0:r0 + sb, :] * (HEAD_DIM ** -0.5 * LOG2E)
        q2 = []
        for g in range(N_KV_HEADS):
            a = q[:, (2 * g) * HEAD_DIM:(2 * g + 1) * HEAD_DIM]
            b = q[:, (2 * g + 1) * HEAD_DIM:(2 * g + 2) * HEAD_DIM]
            q2.append(jnp.concatenate([a, b], axis=0).astype(BF16))
        for g in range(N_KV_HEADS):
            mx_refs[g][...] = jnp.full(mx_refs[g].shape, -jnp.inf, F32)
            acc_refs[g][...] = jnp.zeros(acc_refs[g].shape, F32)

        def logits(col, k_fn, r0=r0, thr_sub=thr_sub, q2=q2):
            bias = jnp.where(keys_ref[col, r0:r0 + sb, :] >= thr_sub, 0.0, NEG_BIG)
            bias2 = jnp.concatenate([bias, bias], axis=0)
            for g in range(N_KV_HEADS):
                s = _dot_nt(q2[g], k_fn(g)) + bias2
                s_ref[g, col] = s
                mx = mx_refs[g][...]
                for c in range(kt // LANES):
                    mx = jnp.maximum(mx, s[:, c * LANES:(c + 1) * LANES])
                mx_refs[g][...] = mx

        def weighted(col, v_fn, m_rows):
            for g in range(N_KV_HEADS):
                p = jnp.exp2(s_ref[g, col] - m_rows[g]).astype(BF16)
                v_aug = jnp.concatenate([v_fn(g), ones_blk], axis=1)
                acc_refs[g][...] = acc_refs[g][...] + _dot(p, v_aug)

        if has_past:
            def pa_past(t, c, logits=logits):
                rows = _tile_rows(t, kt)
                logits(t, lambda g: kp_ref[0, _head_rows(t, kt, g), :].astype(BF16))
                return c
            lax.fori_loop(0, n_past_tiles, pa_past, 0)

        def pa_new(t, c, logits=logits):
            rows = _tile_rows(t, kt)
            logits(n_past_tiles + t, lambda g: kn_ref[0, rows, hs(g)])
            return c
        lax.fori_loop(0, nk_new, pa_new, 0)

        m_rows = [jnp.max(mx_refs[g][...], axis=1, keepdims=True) for g in range(N_KV_HEADS)]

        if has_past:
            def pb_past(t, c, m_rows=m_rows, weighted=weighted):
                rows = _tile_rows(t, kt)
                weighted(t, lambda g: vp_ref[0, _head_rows(t, kt, g), :].astype(BF16), m_rows)
                return c
            lax.fori_loop(0, n_past_tiles, pb_past, 0)

        def pb_new(t, c, m_rows=m_rows, weighted=weighted):
            rows = _tile_rows(t, kt)
            weighted(n_past_tiles + t, lambda g: vn_ref[0, rows, hs(g)], m_rows)
            return c
        lax.fori_loop(0, nk_new, pb_new, 0)

        for g in range(N_KV_HEADS):
            acc = acc_refs[g][...]
            o = acc[:, 0:HEAD_DIM] / acc[:, HEAD_DIM:2 * HEAD_DIM]
            o_ref[0, r0:r0 + sb, (2 * g) * HEAD_DIM:(2 * g + 1) * HEAD_DIM] = o[0:sb].astype(BF16)
            o_ref[0, r0:r0 + sb, (2 * g + 1) * HEAD_DIM:(2 * g + 2) * HEAD_DIM] = o[sb:2 * sb].astype(BF16)


def _dsa(z3, kv3, past, qb, sb):
    bsz, t, _ = z3.shape
    kt = DSA_KT
    nb = t // qb
    nsub = qb // sb
    has_past = past is not None
    past_len = past[0].shape[1] if has_past else 0
    topk = min(TOPK_MAX, (past_len + t) // 4)
    t_pad = -(-t // kt) * kt
    if t_pad != t:
        kv3 = jnp.pad(kv3, ((0, 0), (0, t_pad - t), (0, 0)))
    n_new_tiles = t_pad // kt
    n_past_tiles = past_len // kt
    qi = z3[:, :, OFF_QI:OFF_QI + IDX_HEADS * IDX_DIM].astype(BF16)
    qi = qi.reshape(bsz, nb, nsub, sb, IDX_HEADS, IDX_DIM).transpose(0, 1, 2, 4, 3, 5)
    qi = qi.reshape(bsz, nb, IDX_HEADS * qb, IDX_DIM)

    in_specs = [
        pl.BlockSpec((1, qb, ATT_WIDTH), lambda b, i: (b, i, OFF_QA // ATT_WIDTH)),
        pl.BlockSpec((1, 1, IDX_HEADS * qb, IDX_DIM), lambda b, i: (b, i, 0, 0)),
        pl.BlockSpec((1, qb, LANES), lambda b, i: (b, i, OFF_SM // LANES)),
    ]
    args = [z3, qi, z3]
    if has_past:
        pk, pv, pki = past
        pk = pk.reshape(bsz, past_len * N_KV_HEADS, HEAD_DIM)
        pv = pv.reshape(bsz, past_len * N_KV_HEADS, HEAD_DIM)
        in_specs += [
            pl.BlockSpec((1, past_len, IDX_DIM), lambda b, i: (b, 0, 0)),
            pl.BlockSpec((1, past_len * N_KV_HEADS, HEAD_DIM), lambda b, i: (b, 0, 0)),
            pl.BlockSpec((1, past_len * N_KV_HEADS, HEAD_DIM), lambda b, i: (b, 0, 0)),
        ]
        args += [pki, pk, pv]
    in_specs += [
        pl.BlockSpec((1, t_pad, LANES), lambda b, i: (b, 0, 2 * KV_WIDTH // LANES)),
        pl.BlockSpec((1, t_pad, KV_WIDTH), lambda b, i: (b, 0, 0)),
        pl.BlockSpec((1, t_pad, KV_WIDTH), lambda b, i: (b, 0, 1)),
    ]
    args += [kv3, kv3, kv3]
    kern = functools.partial(_dsa_kernel, has_past=has_past, qblk=qb, sb=sb, n_past_tiles=n_past_tiles,
                             n_new_tiles=n_new_tiles, t_valid=t, past_len=past_len, topk=topk)
    return pl.pallas_call(
        kern,
        grid=(bsz, nb),
        in_specs=in_specs,
        out_specs=pl.BlockSpec((1, qb, ATT_WIDTH), lambda b, i: (b, i, 0)),
        out_shape=jax.ShapeDtypeStruct((bsz, t, ATT_WIDTH), BF16),
        scratch_shapes=[pltpu.VMEM((n_past_tiles + n_new_tiles, qb, kt), I32),
                        pltpu.VMEM((n_past_tiles + n_new_tiles, kt, qb) if qb % LANES == 0
                                   else (1, SUBLANES, LANES), I32),
                        pltpu.VMEM((IDX_HEADS * sb, kt), F32),
                        pltpu.VMEM((N_KV_HEADS, n_past_tiles + n_new_tiles, 2 * sb, kt), F32)] + [
            pltpu.VMEM((2 * sb, LANES), F32), pltpu.VMEM((2 * sb, 2 * HEAD_DIM), F32)] * N_KV_HEADS,
        compiler_params=_cparams(2),
        name="dsa_past" if has_past else "dsa",
    )(*args)


def _mlstm_kernel(qk_ref, vm_ref, og_ref, sm_ref, gt_ref, cprev_ref, c0_ref, n0_ref, m0_ref,
                  cw_ref, cb_ref, gbrow_ref, gbcol_ref, gain_ref,
                  hm_ref, cnew_ref, cout_ref, nout_ref, mout_ref,
                  ubuf, c_s, n_s, m_s, *, ln):
    c = pl.program_id(1)
    nc = pl.num_programs(1)
    pad = SUBLANES

    @pl.when(c == 0)
    def _():
        ubuf[0:pad] = cprev_ref[0]
        c_s[...] = c0_ref[0]
        n_s[...] = n0_ref[0]
        m_s[...] = m0_ref[0]

    @pl.when(c > 0)
    def _():
        ubuf[0:pad] = ubuf[ln:ln + pad]

    ubuf[pad:pad + ln] = qk_ref[0]
    y = cb_ref[...] + ubuf[pad - 3:pad - 3 + ln] * cw_ref[0:1]
    for j in range(1, CONV_W):
        y = y + ubuf[pad - 3 + j:pad - 3 + j + ln] * cw_ref[j:j + 1]
    qkc = y * _sigmoid(y)
    cnew_ref[0] = ubuf[ln:ln + pad]

    pre_c = sm_ref[0] + gbrow_ref[...]
    pre_r = gt_ref[0, 0] + gbcol_ref[...][:, 0:1]

    def log_sigmoid(x):
        return jnp.minimum(x, 0.0) - jnp.log(1.0 + jnp.exp(-jnp.abs(x)))

    lf_c = log_sigmoid(pre_c)
    lf_r = log_sigmoid(pre_r)
    ri = lax.broadcasted_iota(I32, (ln, ln), 0)
    ci = lax.broadcasted_iota(I32, (ln, ln), 1)
    causal = ci <= ri
    tril = jnp.where(causal, 1.0, 0.0).astype(BF16)
    triu = jnp.where(ri <= ci, 1.0, 0.0).astype(BF16)
    b_c = sum(_dot(tril, p) for p in _split3(lf_c))
    b_r = sum(_dot(p, triu) for p in _split3(lf_r))

    vm = vm_ref[0]
    og = og_ref[0]
    for h in range(M_HEADS):
        hs = slice(h * M_HEAD_DIM, (h + 1) * M_HEAD_DIM)
        qf = qkc[:, hs]
        kf = qkc[:, M_WIDTH + h * M_HEAD_DIM:M_WIDTH + (h + 1) * M_HEAD_DIM] * (M_HEAD_DIM ** -0.5)
        qb16 = qf.astype(BF16)
        kb16 = kf.astype(BF16)
        vb16 = vm[:, hs].astype(BF16)
        bcol = b_c[:, SM_FG + h:SM_FG + h + 1]
        igcol = pre_c[:, SM_IG + h:SM_IG + h + 1]
        brow = b_r[M_HEADS + h:M_HEADS + h + 1, :]
        igrow = pre_r[h:h + 1, :]
        blast = bcol[ln - 1:ln, :]
        m_prev = m_s[h:h + 1, 0:1]
        dmat = jnp.where(causal, bcol - brow + igrow, -jnp.inf)
        m_inter = bcol + m_prev
        m_i = jnp.maximum(m_inter, jnp.max(dmat, axis=1, keepdims=True))
        s = _dot_nt(qb16, kb16) * jnp.exp(dmat - m_i)
        scale = jnp.exp(m_inter - m_i)
        c_prev = c_s[h]
        n_prev = n_s[h:h + 1, :]
        num = _dot(s.astype(BF16), vb16) + scale * _dot(qb16, c_prev.astype(BF16))
        den = jnp.sum(s, axis=1, keepdims=True) + scale * jnp.sum(qf * n_prev, axis=1, keepdims=True)
        hh = num / jnp.maximum(jnp.abs(den), jnp.exp(-m_i))
        m_new = m_i[ln - 1:ln, :]
        decay = jnp.exp(blast + m_prev - m_new)
        wcol = jnp.exp(blast - bcol + igcol - m_new)
        kw = kf * wcol
        c_s[h] = decay * c_prev + _dot_tn(kw.astype(BF16), vb16)
        n_s[h:h + 1, :] = decay * n_prev + jnp.sum(kw, axis=0, keepdims=True)
        m_s[h:h + 1, :] = jnp.broadcast_to(m_new, (1, LANES))
        hn = hh * lax.rsqrt(jnp.mean(hh * hh, axis=1, keepdims=True) + EPS) * gain_ref[:, hs]
        hm_ref[0, :, hs] = (hn * _sigmoid(og[:, hs])).astype(BF16)

    @pl.when(c == nc - 1)
    def _():
        cout_ref[0] = c_s[...]
        nout_ref[0] = n_s[...]
        mout_ref[0] = m_s[...]


def _mlstm(z3, conv_prev, c0, n0, m0, conv_w, conv_b, b_igate, b_fgate, mh_gain, ln):
    bsz, t, _ = z3.shape
    nc = t // ln
    pad = SUBLANES
    gt = z3[:, :, OFF_SM + SM_IG:OFF_SM + SM_IG + 2 * M_HEADS]
    gt = gt.reshape(bsz, nc, ln, 2 * M_HEADS).transpose(0, 1, 3, 2)
    cprev = jnp.pad(conv_prev.astype(F32), ((0, 0), (pad - (CONV_W - 1), 0), (0, 0)))
    gbias = jnp.concatenate([b_igate, b_fgate]).astype(F32)
    gbrow = jnp.zeros((1, LANES), F32).at[0, SM_IG:SM_IG + 2 * M_HEADS].set(gbias)
    gbcol = jnp.broadcast_to(gbias[:, None], (2 * M_HEADS, LANES))
    m0b = jnp.broadcast_to(m0.astype(F32)[:, :, None], (bsz, M_HEADS, LANES))
    kern = functools.partial(_mlstm_kernel, ln=ln)
    full = lambda shape: pl.BlockSpec(shape, lambda b, c: (0,) * len(shape))
    hm, cnew, cout, nout, mout = pl.pallas_call(
        kern,
        grid=(bsz, nc),
        in_specs=[
            pl.BlockSpec((1, ln, 2 * M_WIDTH), lambda b, c: (b, c, OFF_QM // (2 * M_WIDTH))),
            pl.BlockSpec((1, ln, M_WIDTH), lambda b, c: (b, c, OFF_VM // M_WIDTH)),
            pl.BlockSpec((1, ln, M_WIDTH), lambda b, c: (b, c, OFF_OG // M_WIDTH)),
            pl.BlockSpec((1, ln, LANES), lambda b, c: (b, c, OFF_SM // LANES)),
            pl.BlockSpec((1, 1, 2 * M_HEADS, ln), lambda b, c: (b, c, 0, 0)),
            pl.BlockSpec((1, pad, 2 * M_WIDTH), lambda b, c: (b, 0, 0)),
            pl.BlockSpec((1, M_HEADS, M_HEAD_DIM, M_HEAD_DIM), lambda b, c: (b, 0, 0, 0)),
            pl.BlockSpec((1, M_HEADS, M_HEAD_DIM), lambda b, c: (b, 0, 0)),
            pl.BlockSpec((1, M_HEADS, LANES), lambda b, c: (b, 0, 0)),
            full((CONV_W, 2 * M_WIDTH)),
            full((1, 2 * M_WIDTH)),
            full((1, LANES)),
            full((2 * M_HEADS, LANES)),
            full((1, M_WIDTH)),
        ],
        out_specs=[
            pl.BlockSpec((1, ln, M_WIDTH), lambda b, c: (b, c, 0)),
            pl.BlockSpec((1, pad, 2 * M_WIDTH), lambda b, c: (b, 0, 0)),
            pl.BlockSpec((1, M_HEADS, M_HEAD_DIM, M_HEAD_DIM), lambda b, c: (b, 0, 0, 0)),
            pl.BlockSpec((1, M_HEADS, M_HEAD_DIM), lambda b, c: (b, 0, 0)),
            pl.BlockSpec((1, M_HEADS, LANES), lambda b, c: (b, 0, 0)),
        ],
        out_shape=[
            jax.ShapeDtypeStruct((bsz, t, M_WIDTH), BF16),
            jax.ShapeDtypeStruct((bsz, pad, 2 * M_WIDTH), F32),
            jax.ShapeDtypeStruct((bsz, M_HEADS, M_HEAD_DIM, M_HEAD_DIM), F32),
            jax.ShapeDtypeStruct((bsz, M_HEADS, M_HEAD_DIM), F32),
            jax.ShapeDtypeStruct((bsz, M_HEADS, LANES), F32),
        ],
        scratch_shapes=[
            pltpu.VMEM((ln + pad, 2 * M_WIDTH), F32),
            pltpu.VMEM((M_HEADS, M_HEAD_DIM, M_HEAD_DIM), F32),
            pltpu.VMEM((M_HEADS, M_HEAD_DIM), F32),
            pltpu.VMEM((M_HEADS, LANES), F32),
        ],
        compiler_params=_cparams(2),
        name="mlstm",
    )(z3, z3, z3, z3, gt, cprev, c0.astype(F32), n0.astype(F32), m0b,
      conv_w.astype(F32), conv_b.astype(F32)[None, :], gbrow, gbcol, mh_gain.astype(F32)[None, :])
    return hm, cnew[:, pad - (CONV_W - 1):, :], cout, nout, mout[:, :, 0]


def _mix_kernel(att_ref, hm_ref, ga_ref, gb_ref, wa_ref, wb_ref, mixed_ref):
    ya = _dot(att_ref[...], wa_ref[...])
    yb = _dot(hm_ref[...], wb_ref[...])
    mixed_ref[...] = (_sigmoid(ga_ref[...]) * ya + _sigmoid(gb_ref[...]) * yb).astype(BF16)


def _mix(att2, hm2, z2, wa, wb):
    n = att2.shape[0]
    tm = min(POST_TM, n)
    row = lambda w: pl.BlockSpec((tm, w), lambda i: (i, 0))
    full = lambda shape: pl.BlockSpec(shape, lambda i: (0,) * len(shape))
    return pl.pallas_call(
        _mix_kernel,
        grid=(n // tm,),
        in_specs=[
            row(ATT_WIDTH), row(M_WIDTH),
            pl.BlockSpec((tm, D_MODEL), lambda i: (i, OFF_GA // D_MODEL)),
            pl.BlockSpec((tm, D_MODEL), lambda i: (i, OFF_GB // D_MODEL)),
            full((ATT_WIDTH, D_MODEL)), full((M_WIDTH, D_MODEL)),
        ],
        out_specs=row(D_MODEL),
        out_shape=jax.ShapeDtypeStruct((n, D_MODEL), BF16),
        compiler_params=_cparams(1),
        name="mix",
    )(att2, hm2, z2, z2, wa, wb)


def _post_kernel(mixed_ref, x_ref, wo_ref, gf_ref, wr_ref, wrl_ref, br_ref, x1_ref, h2_ref, ri_ref):
    x1 = x_ref[...] + _dot(mixed_ref[...], wo_ref[...])
    x1_ref[...] = x1
    h2 = x1 * lax.rsqrt(jnp.mean(x1 * x1, axis=-1, keepdims=True) + EPS) * gf_ref[...]
    h2_ref[...] = h2
    h_hi = h2.astype(BF16)
    h_lo = (h2 - h_hi.astype(F32)).astype(BF16)
    lg = (_dot(h_hi, wr_ref[...]) + _dot(h_lo, wr_ref[...]) + _dot(h_hi, wrl_ref[...])) + br_ref[...]
    lane = lax.broadcasted_iota(I32, lg.shape, 1)
    glog = jnp.where(lane < N_GROUPS, lg, -jnp.inf)
    gmax = jnp.max(glog, axis=1, keepdims=True)
    gstar = jnp.min(jnp.where(glog == gmax, lane, LANES), axis=1, keepdims=True)
    pg = 1.0 / jnp.sum(jnp.exp(glog - gmax), axis=1, keepdims=True)
    lo = N_GROUPS + EXP_PER_GROUP * gstar
    elog = jnp.where((lane >= lo) & (lane < lo + EXP_PER_GROUP), lg, -jnp.inf)
    v0 = jnp.max(elog, axis=1, keepdims=True)
    i0 = jnp.min(jnp.where(elog == v0, lane, LANES), axis=1, keepdims=True)
    elog2 = jnp.where(lane == i0, -jnp.inf, elog)
    v1 = jnp.max(elog2, axis=1, keepdims=True)
    i1 = jnp.min(jnp.where(elog2 == v1, lane, LANES), axis=1, keepdims=True)
    e1 = jnp.exp(v1 - v0)
    w0 = pg / (1.0 + e1)
    w1 = pg * e1 / (1.0 + e1)
    out = jnp.where(lane == 0, (i0 - N_GROUPS).astype(F32), 0.0)
    out = jnp.where(lane == 1, (i1 - N_GROUPS).astype(F32), out)
    out = jnp.where(lane == 2, w0, out)
    out = jnp.where(lane == 3, w1, out)
    ri_ref[...] = out


def _post(mixed, x2, wo, g_ffn, wr, br):
    n = x2.shape[0]
    wr_hi = wr.astype(BF16)
    wr_lo = (wr - wr_hi.astype(F32)).astype(BF16)
    tm = min(POST_TM, n)
    row = lambda w: pl.BlockSpec((tm, w), lambda i: (i, 0))
    full = lambda shape: pl.BlockSpec(shape, lambda i: (0,) * len(shape))
    return pl.pallas_call(
        _post_kernel,
        grid=(n // tm,),
        in_specs=[
            row(D_MODEL), row(D_MODEL), full((D_MODEL, D_MODEL)),
            full((1, D_MODEL)), full((D_MODEL, LANES)), full((D_MODEL, LANES)), full((1, LANES)),
        ],
        out_specs=[row(D_MODEL), row(D_MODEL), row(LANES)],
        out_shape=[jax.ShapeDtypeStruct((n, D_MODEL), F32),
                   jax.ShapeDtypeStruct((n, D_MODEL), F32),
                   jax.ShapeDtypeStruct((n, LANES), F32)],
        compiler_params=_cparams(1),
        name="post",
    )(mixed, x2, wo, g_ffn, wr_hi, wr_lo, br)


def _row_gather_start(idx_ref, n_rows, src_hbm, dst_buf, sem):
    def body(r, c):
        tok = idx_ref[0, 0, r]
        pltpu.make_async_copy(src_hbm.at[pl.ds(tok, 1)], dst_buf.at[pl.ds(r, 1)], sem).start()
        return c
    lax.fori_loop(0, n_rows, body, 0, unroll=8)


def _row_gather_wait(n_rows, src_hbm, dst_buf, sem):
    pltpu.make_async_copy(src_hbm.at[pl.ds(0, n_rows)], dst_buf, sem).wait()


def _moe_kernel(te_ref, nu_ref, tok_ref, tokn_ref, h2_hbm, w1_ref, w3_ref, w2_ref, y_ref,
                xbuf, sem, w1b, w3b, w2b, *, tm):
    t = pl.program_id(0)
    n_used = nu_ref[0]
    slot = t % 2

    @pl.when((t < n_used) & ((t == 0) | (te_ref[t] != te_ref[jnp.maximum(t - 1, 0)])))
    def _():
        w1b[...] = w1_ref[0, 0].astype(BF16)
        w3b[...] = w3_ref[0, 0].astype(BF16)
        w2b[...] = w2_ref[0, 0].astype(BF16)

    @pl.when(t == 0)
    def _():
        _row_gather_start(tok_ref, tm, h2_hbm, xbuf.at[0], sem.at[0])

    @pl.when(t + 1 < n_used)
    def _():
        _row_gather_start(tokn_ref, tm, h2_hbm, xbuf.at[1 - slot], sem.at[1 - slot])

    @pl.when(t < n_used)
    def _():
        _row_gather_wait(tm, h2_hbm, xbuf.at[slot], sem.at[slot])
        xb = xbuf[slot].astype(BF16)
        u = _dot(xb, w1b[...])
        a = (u * _sigmoid(u)) * _dot(xb, w3b[...])
        y_ref[...] = _dot(a.astype(BF16), w2b[...])

    @pl.when(t >= n_used)
    def _():
        y_ref[...] = jnp.zeros(y_ref.shape, F32)


def _moe(h2, tile_expert, n_used, row_token, w1, w3, w2, tm):
    n_tiles = tile_expert.shape[0]
    tok3 = row_token.reshape(n_tiles, 1, tm)
    grid_spec = pltpu.PrefetchScalarGridSpec(
        num_scalar_prefetch=2,
        grid=(n_tiles,),
        in_specs=[
            pl.BlockSpec((1, 1, tm), lambda t, te, nu: (t, 0, 0), memory_space=pltpu.SMEM),
            pl.BlockSpec((1, 1, tm), lambda t, te, nu: (jnp.minimum(t + 1, n_tiles - 1), 0, 0),
                         memory_space=pltpu.SMEM),
            pl.BlockSpec(memory_space=pl.ANY),
            pl.BlockSpec((1, 1, D_MODEL, D_EXPERT), lambda t, te, nu: (0, te[t], 0, 0)),
            pl.BlockSpec((1, 1, D_MODEL, D_EXPERT), lambda t, te, nu: (0, te[t], 0, 0)),
            pl.BlockSpec((1, 1, D_EXPERT, D_MODEL), lambda t, te, nu: (0, te[t], 0, 0)),
        ],
        out_specs=pl.BlockSpec((tm, D_MODEL), lambda t, te, nu: (t, 0)),
        scratch_shapes=[pltpu.VMEM((2, tm, D_MODEL), F32), pltpu.SemaphoreType.DMA((2,)),
                        pltpu.VMEM((D_MODEL, D_EXPERT), BF16), pltpu.VMEM((D_MODEL, D_EXPERT), BF16),
                        pltpu.VMEM((D_EXPERT, D_MODEL), BF16)],
    )
    return pl.pallas_call(
        functools.partial(_moe_kernel, tm=tm),
        grid_spec=grid_spec,
        out_shape=jax.ShapeDtypeStruct((n_tiles * tm, D_MODEL), F32),
        compiler_params=_cparams(1),
        name="moe",
    )(tile_expert, n_used, tok3, tok3, h2, w1, w3, w2)


def _route(rinfo, n_tiles, tm):
    n = rinfo.shape[0]
    eid = rinfo[:, 0:2].astype(I32).reshape(-1)
    order = jnp.argsort(eid, stable=True).astype(I32)
    inv = jnp.argsort(order).astype(I32)
    onehot = eid[:, None] == jnp.arange(N_EXPERTS, dtype=I32)[None, :]
    counts = jnp.sum(onehot.astype(I32), axis=0)
    tiles_e = (counts + tm - 1) // tm
    tile_end = jnp.cumsum(tiles_e)
    row_start = (tile_end - tiles_e) * tm
    grp_start = jnp.cumsum(counts) - counts
    shift = row_start - grp_start
    pos = inv + jnp.sum(jnp.where(onehot, shift[None, :], 0), axis=1)
    n_used = tile_end[-1]
    tile_ids = jnp.arange(n_tiles, dtype=I32)
    te_raw = jnp.sum((tile_end[None, :] <= tile_ids[:, None]).astype(I32), axis=1)
    last_e = jnp.sum((tile_end <= n_used - 1).astype(I32))
    tile_expert = jnp.minimum(te_raw, last_e)
    te_c = jnp.minimum(te_raw, N_EXPERTS - 1)
    src = jnp.arange(n_tiles * tm, dtype=I32) - jnp.repeat(shift[te_c], tm)
    lo = jnp.repeat(grp_start[te_c], tm)
    hi = lo + jnp.repeat(counts[te_c], tm)
    valid = (src >= lo) & (src < hi)
    row_token = jnp.where(valid, order[jnp.clip(src, 0, 2 * n - 1)] // 2, 0)
    return tile_expert, n_used.reshape(1).astype(I32), row_token, pos.reshape(n, 2)


def _final_kernel(pos_ref, posn_ref, y_hbm, x1_ref, ri_ref, pe_ref, gp_ref, wpg_ref, wp_ref, gfin_ref,
                  o_ref, ybuf, sem):
    t = pl.program_id(0)
    nt = pl.num_programs(0)
    slot = t % 2
    tm = TOK_TM

    @pl.when(t == 0)
    def _():
        _row_gather_start(pos_ref, 2 * tm, y_hbm, ybuf.at[0], sem.at[0])

    @pl.when(t + 1 < nt)
    def _():
        _row_gather_start(posn_ref, 2 * tm, y_hbm, ybuf.at[1 - slot], sem.at[1 - slot])

    _row_gather_wait(2 * tm, y_hbm, ybuf.at[slot], sem.at[slot])
    ri = ri_ref[...]
    moe = ri[:, 2:3] * ybuf[slot, 0:tm] + ri[:, 3:4] * ybuf[slot, tm:2 * tm]
    x2 = x1_ref[...] + moe
    hp = x2 * lax.rsqrt(jnp.mean(x2 * x2, axis=-1, keepdims=True) + EPS) * gp_ref[...]
    gate = _sigmoid(_dot(hp.astype(BF16), wpg_ref[...]))
    x3 = x2 + gate * _dot(pe_ref[...].astype(BF16), wp_ref[...])
    o_ref[...] = x3 * lax.rsqrt(jnp.mean(x3 * x3, axis=-1, keepdims=True) + EPS) * gfin_ref[...]


def _final(x1, rinfo, pos, y_sorted, pe2, g_ple, wpg, wp, g_final):
    n = x1.shape[0]
    tm = TOK_TM
    nt = n // tm
    pos3 = pos.reshape(nt, tm, 2).transpose(0, 2, 1).reshape(nt, 1, 2 * tm)
    row = lambda w: pl.BlockSpec((tm, w), lambda i: (i, 0))
    full = lambda shape: pl.BlockSpec(shape, lambda i: (0,) * len(shape))
    return pl.pallas_call(
        _final_kernel,
        grid=(nt,),
        in_specs=[
            pl.BlockSpec((1, 1, 2 * tm), lambda i: (i, 0, 0), memory_space=pltpu.SMEM),
            pl.BlockSpec((1, 1, 2 * tm), lambda i: (jnp.minimum(i + 1, nt - 1), 0, 0),
                         memory_space=pltpu.SMEM),
            pl.BlockSpec(memory_space=pl.ANY),
            row(D_MODEL), row(LANES), row(PLE_DIM),
            full((1, D_MODEL)), full((D_MODEL, D_MODEL)), full((PLE_DIM, D_MODEL)), full((1, D_MODEL)),
        ],
        out_specs=row(D_MODEL),
        out_shape=jax.ShapeDtypeStruct((n, D_MODEL), F32),
        scratch_shapes=[pltpu.VMEM((2, 2 * tm, D_MODEL), F32), pltpu.SemaphoreType.DMA((2,))],
        compiler_params=_cparams(1),
        name="final",
    )(pos3, pos3, y_sorted, x1, rinfo, pe2, g_ple, wpg, wp, g_final)


def _pack_w_in(w):
    offs = np.concatenate([[0], np.cumsum(PROJ_SIZES)])
    qa, ka, va, qi, ki, wi, qm, km, vm, ig, fg, og, ga, gb = [w[:, offs[k]:offs[k + 1]] for k in range(14)]
    small = jnp.concatenate([ki, wi, ig, fg], axis=1)
    small = jnp.pad(small, ((0, 0), (0, SM_WIDTH - small.shape[1])))
    return jnp.concatenate([qa, qi, qm, km, vm, og, ga, gb, ka, va, small], axis=1).astype(BF16)


def _prep_weights(g_mix, w_in, w_att_out, w_mlstm_out, w_out, g_ffn, router_gw, router_gb, router_ew,
                  router_eb, w1, w3, w2, g_ple, w_ple, w_ple_gate, g_final):
    wr = jnp.concatenate([router_gw, router_ew], axis=1).astype(F32)
    wr = jnp.pad(wr, ((0, 0), (0, LANES - wr.shape[1])))
    br = jnp.concatenate([router_gb, router_eb]).astype(F32)
    br = jnp.pad(br, (0, LANES - br.shape[0]))[None, :]
    return dict(
        g_mix=g_mix.astype(F32)[None, :], w_pack=_pack_w_in(w_in),
        wa=w_att_out.astype(BF16), wb=w_mlstm_out.astype(BF16), wo=w_out.astype(BF16),
        g_ffn=g_ffn.astype(F32)[None, :], wr=wr, br=br,
        w1=w1.reshape((1,) + w1.shape[-3:]), w3=w3.reshape((1,) + w3.shape[-3:]),
        w2=w2.reshape((1,) + w2.shape[-3:]),
        g_ple=g_ple.astype(F32)[None, :], wp=w_ple.astype(BF16), wpg=w_ple_gate.astype(BF16),
        g_final=g_final.astype(F32)[None, :])


def _layer(x, pe, past, conv_prev, c0, n0, m0, wts, conv_w, conv_b, b_igate, b_fgate, mh_gain, qb, ln):
    bsz, t, _ = x.shape
    n = bsz * t
    x2 = x.reshape(n, D_MODEL)
    z, kvb, k_new, v_new, ki_new = _inproj(x2, wts["g_mix"], wts["w_pack"])
    z3 = z.reshape(bsz, t, D_PACK)
    kv3 = kvb.reshape(bsz, t, KV_PACK)
    att = _dsa(z3, kv3, past, qb, min(DSA_SB, qb))
    hm, conv_new, c_new, n_new, m_new = _mlstm(z3, conv_prev, c0, n0, m0, conv_w, conv_b,
                                               b_igate, b_fgate, mh_gain, ln)
    mixed = _mix(att.reshape(n, ATT_WIDTH), hm.reshape(n, M_WIDTH), z, wts["wa"], wts["wb"])
    x1, h2, rinfo = _post(mixed, x2, wts["wo"], wts["g_ffn"], wts["wr"], wts["br"])
    tm = MOE_TM if 2 * n >= 4 * MOE_TM * N_EXPERTS else MOE_TM_SMALL
    n_tiles = (2 * n + N_EXPERTS * (tm - 1) + tm - 1) // tm
    tile_expert, n_used, row_token, pos = _route(rinfo, n_tiles, tm)
    y_sorted = _moe(h2, tile_expert, n_used, row_token, wts["w1"], wts["w3"], wts["w2"], tm)
    y = _final(x1, rinfo, pos, y_sorted, pe.reshape(n, PLE_DIM).astype(F32),
               wts["g_ple"], wts["wpg"], wts["wp"], wts["g_final"])
    k_new = k_new.reshape(bsz, t, N_KV_HEADS, HEAD_DIM)
    v_new = v_new.reshape(bsz, t, N_KV_HEADS, HEAD_DIM)
    ki_new = ki_new.reshape(bsz, t, IDX_DIM)
    return y.reshape(bsz, t, D_MODEL), (k_new, v_new, ki_new, conv_new, c_new, n_new, m_new)


def kernel(x_prompt, x_sample, cache_k, cache_v, cache_kidx, state_conv, state_C, state_n, state_m,
           p_prompt, p_sample, g_mix, w_in, conv_w, conv_b, b_igate, b_fgate, mh_gain,
           w_att_out, w_mlstm_out, w_out, g_ffn, router_gw, router_gb, router_ew, router_eb,
           w1, w3, w2, g_ple, w_ple, w_ple_gate, g_final):
    assert g_mix.shape[0] == 1, "single-layer step"
    bp, tp, _ = x_prompt.shape
    bs, ts, _ = x_sample.shape
    sdt = state_C.dtype
    wts = _prep_weights(g_mix[0], w_in[0], w_att_out[0], w_mlstm_out[0], w_out[0], g_ffn[0],
                        router_gw[0], router_gb[0], router_ew[0], router_eb[0], w1, w3, w2,
                        g_ple[0], w_ple[0], w_ple_gate[0], g_final)
    mix = (conv_w[0], conv_b[0], b_igate[0], b_fgate[0], mh_gain[0])
    yp, st_p = _layer(
        x_prompt, p_prompt[0], None,
        jnp.zeros((bp, CONV_W - 1, 2 * M_WIDTH), F32),
        jnp.zeros((bp, M_HEADS, M_HEAD_DIM, M_HEAD_DIM), F32),
        jnp.zeros((bp, M_HEADS, M_HEAD_DIM), F32),
        jnp.zeros((bp, M_HEADS), F32),
        wts, *mix, qb=min(DSA_QB, tp), ln=min(256, tp))
    plen = cache_k.shape[2]
    past = (cache_k[0], cache_v[0], cache_kidx[0])
    ys, st_s = _layer(
        x_sample, p_sample[0], past, state_conv[0], state_C[0], state_n[0], state_m[0],
        wts, *mix, qb=ts, ln=ts)
    outs_p = [s[None] for s in st_p]
    outs_s = [s[None] for s in st_s]
    for lst in (outs_p, outs_s):
        for k in (4, 5, 6):
            lst[k] = lst[k].astype(sdt)
    return (yp, ys, *outs_p, *outs_s)
```

```python
import functools

import numpy as np
import jax
import jax.numpy as jnp
from jax import lax
from jax.experimental import pallas as pl
from jax.experimental.pallas import tpu as pltpu

F32 = jnp.float32
BF16 = jnp.bfloat16
I32 = jnp.int32

D_MODEL = 2048
CHUNK = 64
CHUNK_SHIFT = 6
assert 1 << CHUNK_SHIFT == CHUNK
N_HEADS = 8
N_KV_HEADS = 4
HEAD_DIM = 128
ATT_WIDTH = N_HEADS * HEAD_DIM
KV_WIDTH = N_KV_HEADS * HEAD_DIM
IDX_HEADS = 16
IDX_DIM = 64
TOPK_MAX = 256
M_HEADS = 4
M_HEAD_DIM = 256
M_WIDTH = M_HEADS * M_HEAD_DIM
CONV_W = 4
N_GROUPS = 4
EXP_PER_GROUP = 8
N_EXPERTS = N_GROUPS * EXP_PER_GROUP
D_EXPERT = 512
PLE_DIM = 256
EPS = 1e-6
PROJ_SIZES = (ATT_WIDTH, KV_WIDTH, KV_WIDTH, IDX_HEADS * IDX_DIM, IDX_DIM, IDX_HEADS,
              M_WIDTH, M_WIDTH, M_WIDTH, M_HEADS, M_HEADS, M_WIDTH, D_MODEL, D_MODEL)

LANES = 128
SUBLANES = 8
VMEM_LIMIT = 56 * 1024 * 1024

OFF_QA, OFF_QI, OFF_QM, OFF_KM, OFF_VM, OFF_OG = 0, 1024, 2048, 3072, 4096, 5120
OFF_GA, OFF_GB, OFF_KA, OFF_VA, OFF_SM = 6144, 8192, 10240, 10752, 11264
SM_WIDTH = 512
D_PACK = OFF_SM + SM_WIDTH
SM_KI, SM_WI, SM_IG, SM_FG = 0, 64, 80, 84
PROJ_TN = 512
KV_BLK0 = OFF_KA // PROJ_TN
KV_PACK = D_PACK - OFF_KA

LOG2E = 1.4426950408889634
INT_MIN = np.int32(-2 ** 31)
NEG_BIG = -1e30

DSA_KT = 512
DSA_QB = 128
DSA_SB = 128
MOE_TM = 512
MOE_TM_SMALL = 128
TOK_TM = 256
POST_TM = 512


def _cparams(n_axes):
    return pltpu.CompilerParams(dimension_semantics=("arbitrary",) * n_axes,
                                vmem_limit_bytes=VMEM_LIMIT)


def _dot(a, b):
    return jnp.dot(a, b, preferred_element_type=F32)


def _dot_nt(a, b):
    return lax.dot_general(a, b, (((1,), (1,)), ((), ())), preferred_element_type=F32)


def _dot_tn(a, b):
    return lax.dot_general(a, b, (((0,), (0,)), ((), ())), preferred_element_type=F32)


def _tile_rows(t, size):
    start = t * size
    return pl.ds(start if isinstance(start, int) else pl.multiple_of(start, size), size)


def _head_rows(t, size, g):
    start = t * (size * N_KV_HEADS)
    if not isinstance(start, int):
        start = pl.multiple_of(start, size * N_KV_HEADS)
    return pl.ds(start + g, size, stride=N_KV_HEADS)


def _sigmoid(x):
    return 1.0 / (1.0 + jnp.exp(-x))


def _split3(x):
    hi = x.astype(BF16)
    r1 = x - hi.astype(F32)
    mid = r1.astype(BF16)
    lo = (r1 - mid.astype(F32)).astype(BF16)
    return hi, mid, lo


def _inproj_kernel(x_ref, g_ref, w_ref, z_ref, kv_ref, k_ref, v_ref, ki_ref, h_ref):
    j = pl.program_id(1)

    @pl.when(j == 0)
    def _():
        x = x_ref[...]
        r = lax.rsqrt(jnp.mean(x * x, axis=-1, keepdims=True) + EPS)
        h_ref[...] = (x * r * g_ref[...]).astype(BF16)

    acc = _dot(h_ref[...], w_ref[...])
    z_ref[...] = acc

    @pl.when(j >= KV_BLK0)
    def _():
        kv_ref[...] = acc.astype(BF16)

    tm = acc.shape[0]

    def store_heads(ref):
        for g in range(N_KV_HEADS):
            ref[pl.ds(g, tm, stride=N_KV_HEADS), :] = acc[:, g * HEAD_DIM:(g + 1) * HEAD_DIM]

    @pl.when(j == KV_BLK0)
    def _():
        store_heads(k_ref)

    @pl.when(j == KV_BLK0 + 1)
    def _():
        store_heads(v_ref)

    @pl.when(j == KV_BLK0 + 2)
    def _():
        ki_ref[...] = acc[:, SM_KI:SM_KI + IDX_DIM]


def _inproj(x2d, g, w_pack):
    n = x2d.shape[0]
    tm = min(1024, n)
    grid = (n // tm, D_PACK // PROJ_TN)
    return pl.pallas_call(
        _inproj_kernel,
        grid=grid,
        in_specs=[
            pl.BlockSpec((tm, D_MODEL), lambda i, j: (i, 0)),
            pl.BlockSpec((1, D_MODEL), lambda i, j: (0, 0)),
            pl.BlockSpec((D_MODEL, PROJ_TN), lambda i, j: (0, j)),
        ],
        out_specs=[
            pl.BlockSpec((tm, PROJ_TN), lambda i, j: (i, j)),
            pl.BlockSpec((tm, PROJ_TN), lambda i, j: (i, jnp.maximum(j - KV_BLK0, 0))),
            pl.BlockSpec((tm * N_KV_HEADS, HEAD_DIM), lambda i, j: (i, 0)),
            pl.BlockSpec((tm * N_KV_HEADS, HEAD_DIM), lambda i, j: (i, 0)),
            pl.BlockSpec((tm, IDX_DIM), lambda i, j: (i, 0)),
        ],
        out_shape=[jax.ShapeDtypeStruct((n, D_PACK), F32),
                   jax.ShapeDtypeStruct((n, KV_PACK), BF16),
                   jax.ShapeDtypeStruct((n * N_KV_HEADS, HEAD_DIM), F32),
                   jax.ShapeDtypeStruct((n * N_KV_HEADS, HEAD_DIM), F32),
                   jax.ShapeDtypeStruct((n, IDX_DIM), F32)],
        scratch_shapes=[pltpu.VMEM((tm, D_MODEL), BF16)],
        compiler_params=_cparams(2),
        name="inproj",
    )(x2d, g, w_pack)


def _dsa_kernel(*refs, has_past, qblk, sb, n_past_tiles, n_new_tiles, t_valid, past_len, topk):
    n_in = 9 if has_past else 6
    if has_past:
        qa_ref, qi_ref, sm_ref, kip_ref, kp_ref, vp_ref, kin_ref, kn_ref, vn_ref = refs[:n_in]
    else:
        qa_ref, qi_ref, sm_ref, kin_ref, kn_ref, vn_ref = refs[:n_in]
    o_ref, keys_ref, kT_ref, d_ref, s_ref = refs[n_in:n_in + 5]
    state = refs[n_in + 5:]
    mx_refs, acc_refs = state[0::2], state[1::2]
    kt = DSA_KT
    nsub = qblk // sb
    i = pl.program_id(1)
    q0 = past_len + i * qblk
    j_end = ((q0 + qblk - 1) // CHUNK + 1) * CHUNK - past_len
    nk_new = jnp.minimum((j_end + kt - 1) // kt, n_new_tiles)
    n_tiles = n_past_tiles + nk_new
    lane_sb = lax.broadcasted_iota(I32, (sb, kt), 1)

    for sub in range(nsub):
        r0 = sub * sb
        qchunk = (q0 + r0 + lax.broadcasted_iota(I32, (sb, 1), 0)) >> CHUNK_SHIFT
        wsc = sm_ref[0, r0:r0 + sb, SM_WI:SM_WI + IDX_HEADS] * (IDX_HEADS ** -0.5 * IDX_DIM ** -0.5)
        wcols = [wsc[:, h:h + 1] for h in range(IDX_HEADS)]
        qi_all = qi_ref[0, 0, sub * IDX_HEADS * sb:(sub + 1) * IDX_HEADS * sb]

        def score_tile(ki_t, kpos0, jvalid0, col, r0=r0, qchunk=qchunk, wcols=wcols, qi_all=qi_all):
            d_ref[...] = _dot_nt(qi_all, ki_t)
            acc = wcols[0] * jnp.maximum(d_ref[0:sb], 0.0)
            for h in range(1, IDX_HEADS):
                acc = acc + wcols[h] * jnp.maximum(d_ref[h * sb:(h + 1) * sb], 0.0)
            bits = lax.bitcast_convert_type(acc + 0.0, I32)
            key = bits ^ ((bits >> 31) & np.int32(0x7FFFFFFF))
            adm = (((kpos0 + lane_sb) >> CHUNK_SHIFT) <= qchunk) & (jvalid0 + lane_sb < t_valid)
            keys_ref[col, r0:r0 + sb, :] = jnp.where(adm, key, INT_MIN)

        if has_past:
            def p1_past(t, c, score_tile=score_tile):
                ki_t = kip_ref[0, _tile_rows(t, kt), :].astype(BF16)
                score_tile(ki_t, t * kt, -(2 ** 30), t)
                return c
            lax.fori_loop(0, n_past_tiles, p1_past, 0)

        def p1_new(t, c, score_tile=score_tile):
            ki_t = kin_ref[0, _tile_rows(t, kt), 0:IDX_DIM]
            score_tile(ki_t, past_len + t * kt, t * kt, n_past_tiles + t)
            return c
        lax.fori_loop(0, nk_new, p1_new, 0)

    qb = qblk
    lane = lax.broadcasted_iota(I32, (qb, kt), 1)

    def count(fn):
        def body(t, part):
            kall = keys_ref[t]
            for s in range(kt // LANES):
                ks = kall[:, s * LANES:(s + 1) * LANES]
                part = part + jnp.where(fn(ks, t, s), 1.0, 0.0)
            return part
        part = lax.fori_loop(0, n_tiles, body, jnp.zeros((qb, LANES), F32))
        return jnp.sum(part, axis=1, keepdims=True)

    kf = float(topk)

    bits_per_check = 4

    def search(count_ge, shape):
        def sgroup(carry):
            grp, tu, done, _ = carry
            for b in range(bits_per_check):
                shift = jnp.asarray(31 - b, I32) - grp * bits_per_check
                cand = tu | lax.shift_left(np.int32(1), shift)
                cnt = count_ge(cand ^ INT_MIN)
                tu = jnp.where((cnt >= kf) & (done == 0.0), cand, tu)
                done = jnp.where(cnt == kf, 1.0, done)
            return grp + 1, tu, done, jnp.min(done)

        def scond(carry):
            grp, _, _, all_done = carry
            return (grp < 32 // bits_per_check) & (all_done == 0.0)

        _, tu, _, _ = lax.while_loop(
            scond, sgroup, (jnp.int32(0), jnp.zeros(shape, I32), jnp.zeros(shape, F32), jnp.float32(0.0)))
        return tu ^ INT_MIN

    if qb % LANES == 0:
        def xpose(t, c):
            kT_ref[t] = lax.bitcast_convert_type(lax.bitcast_convert_type(keys_ref[t], F32).T, I32)
            return c
        lax.fori_loop(0, n_tiles, xpose, 0)

        acc_rows = 4 * SUBLANES

        def count_t(fn):
            def body(t, part):
                c = jnp.where(fn(kT_ref[t]), 1.0, 0.0)
                return part + jnp.sum(c.reshape(kt // acc_rows, acc_rows, qb), axis=0)
            part = lax.fori_loop(0, n_tiles, body, jnp.zeros((acc_rows, qb), F32))
            return jnp.sum(part, axis=0, keepdims=True)

        thr_t = search(lambda cs: count_t(lambda k: k >= cs), (1, qb))
        cnt_ge_t = count_t(lambda k: k >= thr_t)
        any_tie = jnp.max(jnp.where((cnt_ge_t > kf) & (thr_t > INT_MIN), 1.0, 0.0)) > 0.0
        eye = lax.broadcasted_iota(I32, (qb, qb), 0) == lax.broadcasted_iota(I32, (qb, qb), 1)
        hi = jnp.sum(jnp.where(eye, (thr_t >> 16).astype(F32), 0.0), axis=1, keepdims=True)
        lo = jnp.sum(jnp.where(eye, (thr_t & np.int32(0xFFFF)).astype(F32), 0.0), axis=1, keepdims=True)
        thr = lax.shift_left(hi.astype(I32), np.int32(16)) | lo.astype(I32)
    else:
        thr = search(lambda cs: count(lambda k, t, s: k >= cs), (qb, 1))
        cnt_ge0 = count(lambda k, t, s: k >= thr)
        any_tie = jnp.max(jnp.where((cnt_ge0 > kf) & (thr > INT_MIN), 1.0, 0.0)) > 0.0

    @pl.when(any_tie)
    def _():
        cnt_ge = count(lambda k, t, s: k >= thr)
        cnt_gt = count(lambda k, t, s: k > thr)
        tie = (cnt_ge > kf) & (thr > INT_MIN)
        need = kf - cnt_gt
        lane1 = lax.broadcasted_iota(I32, (qb, LANES), 1)

        def jbody(it, a):
            cand = a | lax.shift_left(np.int32(1), jnp.asarray(15 - it, I32))
            cnt = count(lambda k, t, s: (k == thr) & (t * kt + s * LANES + lane1 < cand))
            return jnp.where(cnt < need, cand, a)
        a = lax.fori_loop(0, 16, jbody, jnp.zeros((qb, 1), I32))

        def drop(t, c):
            k = keys_ref[t]
            keys_ref[t] = jnp.where(tie & (k == thr) & (t * kt + lane > a), INT_MIN, k)
            return c
        lax.fori_loop(0, n_tiles, drop, 0)

    thr_eff = jnp.maximum(thr, INT_MIN + 1)

    def hs(g):
        return slice(g * HEAD_DIM, (g + 1) * HEAD_DIM)

    ones_blk = jnp.ones((kt, HEAD_DIM), BF16)

    for sub in range(nsub):
        r0 = sub * sb
        thr_sub = thr_eff[r0:r0 + sb]
        q = qa_ref[0, r0:r0 + sb, :] * (HEAD_DIM ** -0.5 * LOG2E)
        q2 = []
        for g in range(N_KV_HEADS):
            a = q[:, (2 * g) * HEAD_DIM:(2 * g + 1) * HEAD_DIM]
            b = q[:, (2 * g + 1) * HEAD_DIM:(2 * g + 2) * HEAD_DIM]
            q2.append(jnp.concatenate([a, b], axis=0).astype(BF16))
        for g in range(N_KV_HEADS):
            mx_refs[g][...] = jnp.full(mx_refs[g].shape, -jnp.inf, F32)
            acc_refs[g][...] = jnp.zeros(acc_refs[g].shape, F32)

        def logits(col, k_fn, r0=r0, thr_sub=thr_sub, q2=q2):
            bias = jnp.where(keys_ref[col, r0:r0 + sb, :] >= thr_sub, 0.0, NEG_BIG)
            bias2 = jnp.concatenate([bias, bias], axis=0)
            for g in range(N_KV_HEADS):
                s = _dot_nt(q2[g], k_fn(g)) + bias2
                s_ref[g, col] = s
                mx = mx_refs[g][...]
                for c in range(kt // LANES):
                    mx = jnp.maximum(mx, s[:, c * LANES:(c + 1) * LANES])
                mx_refs[g][...] = mx

        def weighted(col, v_fn, m_rows):
            for g in range(N_KV_HEADS):
                p = jnp.exp2(s_ref[g, col] - m_rows[g]).astype(BF16)
                v_aug = jnp.concatenate([v_fn(g), ones_blk], axis=1)
                acc_refs[g][...] = acc_refs[g][...] + _dot(p, v_aug)

        if has_past:
            def pa_past(t, c, logits=logits):
                rows = _tile_rows(t, kt)
                logits(t, lambda g: kp_ref[0, _head_rows(t, kt, g), :].astype(BF16))
                return c
            lax.fori_loop(0, n_past_tiles, pa_past, 0)

        def pa_new(t, c, logits=logits):
            rows = _tile_rows(t, kt)
            logits(n_past_tiles + t, lambda g: kn_ref[0, rows, hs(g)])
            return c
        lax.fori_loop(0, nk_new, pa_new, 0)

        m_rows = [jnp.max(mx_refs[g][...], axis=1, keepdims=True) for g in range(N_KV_HEADS)]

        if has_past:
            def pb_past(t, c, m_rows=m_rows, weighted=weighted):
                rows = _tile_rows(t, kt)
                weighted(t, lambda g: vp_ref[0, _head_rows(t, kt, g), :].astype(BF16), m_rows)
                return c
            lax.fori_loop(0, n_past_tiles, pb_past, 0)

        def pb_new(t, c, m_rows=m_rows, weighted=weighted):
            rows = _tile_rows(t, kt)
            weighted(n_past_tiles + t, lambda g: vn_ref[0, rows, hs(g)], m_rows)
            return c
        lax.fori_loop(0, nk_new, pb_new, 0)

        for g in range(N_KV_HEADS):
            acc = acc_refs[g][...]
            o = acc[:, 0:HEAD_DIM] / acc[:, HEAD_DIM:2 * HEAD_DIM]
            o_ref[0, r0:r0 + sb, (2 * g) * HEAD_DIM:(2 * g + 1) * HEAD_DIM] = o[0:sb].astype(BF16)
            o_ref[0, r0:r0 + sb, (2 * g + 1) * HEAD_DIM:(2 * g + 2) * HEAD_DIM] = o[sb:2 * sb].astype(BF16)


def _dsa(z3, kv3, past, qb, sb):
    bsz, t, _ = z3.shape
    kt = DSA_KT
    nb = t // qb
    nsub = qb // sb
    has_past = past is not None
    past_len = past[0].shape[1] if has_past else 0
    topk = min(TOPK_MAX, (past_len + t) // 4)
    t_pad = -(-t // kt) * kt
    if t_pad != t:
        kv3 = jnp.pad(kv3, ((0, 0), (0, t_pad - t), (0, 0)))
    n_new_tiles = t_pad // kt
    n_past_tiles = past_len // kt
    qi = z3[:, :, OFF_QI:OFF_QI + IDX_HEADS * IDX_DIM].astype(BF16)
    qi = qi.reshape(bsz, nb, nsub, sb, IDX_HEADS, IDX_DIM).transpose(0, 1, 2, 4, 3, 5)
    qi = qi.reshape(bsz, nb, IDX_HEADS * qb, IDX_DIM)

    in_specs = [
        pl.BlockSpec((1, qb, ATT_WIDTH), lambda b, i: (b, i, OFF_QA // ATT_WIDTH)),
        pl.BlockSpec((1, 1, IDX_HEADS * qb, IDX_DIM), lambda b, i: (b, i, 0, 0)),
        pl.BlockSpec((1, qb, LANES), lambda b, i: (b, i, OFF_SM // LANES)),
    ]
    args = [z3, qi, z3]
    if has_past:
        pk, pv, pki = past
        pk = pk.reshape(bsz, past_len * N_KV_HEADS, HEAD_DIM)
        pv = pv.reshape(bsz, past_len * N_KV_HEADS, HEAD_DIM)
        in_specs += [
            pl.BlockSpec((1, past_len, IDX_DIM), lambda b, i: (b, 0, 0)),
            pl.BlockSpec((1, past_len * N_KV_HEADS, HEAD_DIM), lambda b, i: (b, 0, 0)),
            pl.BlockSpec((1, past_len * N_KV_HEADS, HEAD_DIM), lambda b, i: (b, 0, 0)),
        ]
        args += [pki, pk, pv]
    in_specs += [
        pl.BlockSpec((1, t_pad, LANES), lambda b, i: (b, 0, 2 * KV_WIDTH // LANES)),
        pl.BlockSpec((1, t_pad, KV_WIDTH), lambda b, i: (b, 0, 0)),
        pl.BlockSpec((1, t_pad, KV_WIDTH), lambda b, i: (b, 0, 1)),
    ]
    args += [kv3, kv3, kv3]
    kern = functools.partial(_dsa_kernel, has_past=has_past, qblk=qb, sb=sb, n_past_tiles=n_past_tiles,
                             n_new_tiles=n_new_tiles, t_valid=t, past_len=past_len, topk=topk)
    return pl.pallas_call(
        kern,
        grid=(bsz, nb),
        in_specs=in_specs,
        out_specs=pl.BlockSpec((1, qb, ATT_WIDTH), lambda b, i: (b, i, 0)),
        out_shape=jax.ShapeDtypeStruct((bsz, t, ATT_WIDTH), BF16),
        scratch_shapes=[pltpu.VMEM((n_past_tiles + n_new_tiles, qb, kt), I32),
                        pltpu.VMEM((n_past_tiles + n_new_tiles, kt, qb) if qb % LANES == 0
                                   else (1, SUBLANES, LANES), I32),
                        pltpu.VMEM((IDX_HEADS * sb, kt), F32),
                        pltpu.VMEM((N_KV_HEADS, n_past_tiles + n_new_tiles, 2 * sb, kt), F32)] + [
            pltpu.VMEM((2 * sb, LANES), F32), pltpu.VMEM((2 * sb, 2 * HEAD_DIM), F32)] * N_KV_HEADS,
        compiler_params=_cparams(2),
        name="dsa_past" if has_past else "dsa",
    )(*args)


def _mlstm_kernel(qk_ref, vm_ref, og_ref, sm_ref, gt_ref, cprev_ref, c0_ref, n0_ref, m0_ref,
                  cw_ref, cb_ref, gbrow_ref, gbcol_ref, gain_ref,
                  hm_ref, cnew_ref, cout_ref, nout_ref, mout_ref,
                  ubuf, c_s, n_s, m_s, *, ln):
    c = pl.program_id(1)
    nc = pl.num_programs(1)
    pad = SUBLANES

    @pl.when(c == 0)
    def _():
        ubuf[0:pad] = cprev_ref[0]
        c_s[...] = c0_ref[0]
        n_s[...] = n0_ref[0]
        m_s[...] = m0_ref[0]

    @pl.when(c > 0)
    def _():
        ubuf[0:pad] = ubuf[ln:ln + pad]

    ubuf[pad:pad + ln] = qk_ref[0]
    y = cb_ref[...] + ubuf[pad - 3:pad - 3 + ln] * cw_ref[0:1]
    for j in range(1, CONV_W):
        y = y + ubuf[pad - 3 + j:pad - 3 + j + ln] * cw_ref[j:j + 1]
    qkc = y * _sigmoid(y)
    cnew_ref[0] = ubuf[ln:ln + pad]

    pre_c = sm_ref[0] + gbrow_ref[...]
    pre_r = gt_ref[0, 0] + gbcol_ref[...][:, 0:1]

    def log_sigmoid(x):
        return jnp.minimum(x, 0.0) - jnp.log(1.0 + jnp.exp(-jnp.abs(x)))

    lf_c = log_sigmoid(pre_c)
    lf_r = log_sigmoid(pre_r)
    ri = lax.broadcasted_iota(I32, (ln, ln), 0)
    ci = lax.broadcasted_iota(I32, (ln, ln), 1)
    causal = ci <= ri
    tril = jnp.where(causal, 1.0, 0.0).astype(BF16)
    triu = jnp.where(ri <= ci, 1.0, 0.0).astype(BF16)
    b_c = sum(_dot(tril, p) for p in _split3(lf_c))
    b_r = sum(_dot(p, triu) for p in _split3(lf_r))

    vm = vm_ref[0]
    og = og_ref[0]
    for h in range(M_HEADS):
        hs = slice(h * M_HEAD_DIM, (h + 1) * M_HEAD_DIM)
        qf = qkc[:, hs]
        kf = qkc[:, M_WIDTH + h * M_HEAD_DIM:M_WIDTH + (h + 1) * M_HEAD_DIM] * (M_HEAD_DIM ** -0.5)
        qb16 = qf.astype(BF16)
        kb16 = kf.astype(BF16)
        vb16 = vm[:, hs].astype(BF16)
        bcol = b_c[:, SM_FG + h:SM_FG + h + 1]
        igcol = pre_c[:, SM_IG + h:SM_IG + h + 1]
        brow = b_r[M_HEADS + h:M_HEADS + h + 1, :]
        igrow = pre_r[h:h + 1, :]
        blast = bcol[ln - 1:ln, :]
        m_prev = m_s[h:h + 1, 0:1]
        dmat = jnp.where(causal, bcol - brow + igrow, -jnp.inf)
        m_inter = bcol + m_prev
        m_i = jnp.maximum(m_inter, jnp.max(dmat, axis=1, keepdims=True))
        s = _dot_nt(qb16, kb16) * jnp.exp(dmat - m_i)
        scale = jnp.exp(m_inter - m_i)
        c_prev = c_s[h]
        n_prev = n_s[h:h + 1, :]
        num = _dot(s.astype(BF16), vb16) + scale * _dot(qb16, c_prev.astype(BF16))
        den = jnp.sum(s, axis=1, keepdims=True) + scale * jnp.sum(qf * n_prev, axis=1, keepdims=True)
        hh = num / jnp.maximum(jnp.abs(den), jnp.exp(-m_i))
        m_new = m_i[ln - 1:ln, :]
        decay = jnp.exp(blast + m_prev - m_new)
        wcol = jnp.exp(blast - bcol + igcol - m_new)
        kw = kf * wcol
        c_s[h] = decay * c_prev + _dot_tn(kw.astype(BF16), vb16)
        n_s[h:h + 1, :] = decay * n_prev + jnp.sum(kw, axis=0, keepdims=True)
        m_s[h:h + 1, :] = jnp.broadcast_to(m_new, (1, LANES))
        hn = hh * lax.rsqrt(jnp.mean(hh * hh, axis=1, keepdims=True) + EPS) * gain_ref[:, hs]
        hm_ref[0, :, hs] = (hn * _sigmoid(og[:, hs])).astype(BF16)

    @pl.when(c == nc - 1)
    def _():
        cout_ref[0] = c_s[...]
        nout_ref[0] = n_s[...]
        mout_ref[0] = m_s[...]


def _mlstm(z3, conv_prev, c0, n0, m0, conv_w, conv_b, b_igate, b_fgate, mh_gain, ln):
    bsz, t, _ = z3.shape
    nc = t // ln
    pad = SUBLANES
    gt = z3[:, :, OFF_SM + SM_IG:OFF_SM + SM_IG + 2 * M_HEADS]
    gt = gt.reshape(bsz, nc, ln, 2 * M_HEADS).transpose(0, 1, 3, 2)
    cprev = jnp.pad(conv_prev.astype(F32), ((0, 0), (pad - (CONV_W - 1), 0), (0, 0)))
    gbias = jnp.concatenate([b_igate, b_fgate]).astype(F32)
    gbrow = jnp.zeros((1, LANES), F32).at[0, SM_IG:SM_IG + 2 * M_HEADS].set(gbias)
    gbcol = jnp.broadcast_to(gbias[:, None], (2 * M_HEADS, LANES))
    m0b = jnp.broadcast_to(m0.astype(F32)[:, :, None], (bsz, M_HEADS, LANES))
    kern = functools.partial(_mlstm_kernel, ln=ln)
    full = lambda shape: pl.BlockSpec(shape, lambda b, c: (0,) * len(shape))
    hm, cnew, cout, nout, mout = pl.pallas_call(
        kern,
        grid=(bsz, nc),
        in_specs=[
            pl.BlockSpec((1, ln, 2 * M_WIDTH), lambda b, c: (b, c, OFF_QM // (2 * M_WIDTH))),
            pl.BlockSpec((1, ln, M_WIDTH), lambda b, c: (b, c, OFF_VM // M_WIDTH)),
            pl.BlockSpec((1, ln, M_WIDTH), lambda b, c: (b, c, OFF_OG // M_WIDTH)),
            pl.BlockSpec((1, ln, LANES), lambda b, c: (b, c, OFF_SM // LANES)),
            pl.BlockSpec((1, 1, 2 * M_HEADS, ln), lambda b, c: (b, c, 0, 0)),
            pl.BlockSpec((1, pad, 2 * M_WIDTH), lambda b, c: (b, 0, 0)),
            pl.BlockSpec((1, M_HEADS, M_HEAD_DIM, M_HEAD_DIM), lambda b, c: (b, 0, 0, 0)),
            pl.BlockSpec((1, M_HEADS, M_HEAD_DIM), lambda b, c: (b, 0, 0)),
            pl.BlockSpec((1, M_HEADS, LANES), lambda b, c: (b, 0, 0)),
            full((CONV_W, 2 * M_WIDTH)),
            full((1, 2 * M_WIDTH)),
            full((1, LANES)),
            full((2 * M_HEADS, LANES)),
            full((1, M_WIDTH)),
        ],
        out_specs=[
            pl.BlockSpec((1, ln, M_WIDTH), lambda b, c: (b, c, 0)),
            pl.BlockSpec((1, pad, 2 * M_WIDTH), lambda b, c: (b, 0, 0)),
            pl.BlockSpec((1, M_HEADS, M_HEAD_DIM, M_HEAD_DIM), lambda b, c: (b, 0, 0, 0)),
            pl.BlockSpec((1, M_HEADS, M_HEAD_DIM), lambda b, c: (b, 0, 0)),
            pl.BlockSpec((1, M_HEADS, LANES), lambda b, c: (b, 0, 0)),
        ],
        out_shape=[
            jax.ShapeDtypeStruct((bsz, t, M_WIDTH), BF16),
            jax.ShapeDtypeStruct((bsz, pad, 2 * M_WIDTH), F32),
            jax.ShapeDtypeStruct((bsz, M_HEADS, M_HEAD_DIM, M_HEAD_DIM), F32),
            jax.ShapeDtypeStruct((bsz, M_HEADS, M_HEAD_DIM), F32),
            jax.ShapeDtypeStruct((bsz, M_HEADS, LANES), F32),
        ],
        scratch_shapes=[
            pltpu.VMEM((ln + pad, 2 * M_WIDTH), F32),
            pltpu.VMEM((M_HEADS, M_HEAD_DIM, M_HEAD_DIM), F32),
            pltpu.VMEM((M_HEADS, M_HEAD_DIM), F32),
            pltpu.VMEM((M_HEADS, LANES), F32),
        ],
        compiler_params=_cparams(2),
        name="mlstm",
    )(z3, z3, z3, z3, gt, cprev, c0.astype(F32), n0.astype(F32), m0b,
      conv_w.astype(F32), conv_b.astype(F32)[None, :], gbrow, gbcol, mh_gain.astype(F32)[None, :])
    return hm, cnew[:, pad - (CONV_W - 1):, :], cout, nout, mout[:, :, 0]


def _mix_kernel(att_ref, hm_ref, ga_ref, gb_ref, wa_ref, wb_ref, mixed_ref):
    ya = _dot(att_ref[...], wa_ref[...])
    yb = _dot(hm_ref[...], wb_ref[...])
    mixed_ref[...] = (_sigmoid(ga_ref[...]) * ya + _sigmoid(gb_ref[...]) * yb).astype(BF16)


def _mix(att2, hm2, z2, wa, wb):
    n = att2.shape[0]
    tm = min(POST_TM, n)
    row = lambda w: pl.BlockSpec((tm, w), lambda i: (i, 0))
    full = lambda shape: pl.BlockSpec(shape, lambda i: (0,) * len(shape))
    return pl.pallas_call(
        _mix_kernel,
        grid=(n // tm,),
        in_specs=[
            row(ATT_WIDTH), row(M_WIDTH),
            pl.BlockSpec((tm, D_MODEL), lambda i: (i, OFF_GA // D_MODEL)),
            pl.BlockSpec((tm, D_MODEL), lambda i: (i, OFF_GB // D_MODEL)),
            full((ATT_WIDTH, D_MODEL)), full((M_WIDTH, D_MODEL)),
        ],
        out_specs=row(D_MODEL),
        out_shape=jax.ShapeDtypeStruct((n, D_MODEL), BF16),
        compiler_params=_cparams(1),
        name="mix",
    )(att2, hm2, z2, z2, wa, wb)


def _post_kernel(mixed_ref, x_ref, wo_ref, gf_ref, wr_ref, wrl_ref, br_ref, x1_ref, h2_ref, ri_ref):
    x1 = x_ref[...] + _dot(mixed_ref[...], wo_ref[...])
    x1_ref[...] = x1
    h2 = x1 * lax.rsqrt(jnp.mean(x1 * x1, axis=-1, keepdims=True) + EPS) * gf_ref[...]
    h2_ref[...] = h2
    h_hi = h2.astype(BF16)
    h_lo = (h2 - h_hi.astype(F32)).astype(BF16)
    lg = (_dot(h_hi, wr_ref[...]) + _dot(h_lo, wr_ref[...]) + _dot(h_hi, wrl_ref[...])) + br_ref[...]
    lane = lax.broadcasted_iota(I32, lg.shape, 1)
    glog = jnp.where(lane < N_GROUPS, lg, -jnp.inf)
    gmax = jnp.max(glog, axis=1, keepdims=True)
    gstar = jnp.min(jnp.where(glog == gmax, lane, LANES), axis=1, keepdims=True)
    pg = 1.0 / jnp.sum(jnp.exp(glog - gmax), axis=1, keepdims=True)
    lo = N_GROUPS + EXP_PER_GROUP * gstar
    elog = jnp.where((lane >= lo) & (lane < lo + EXP_PER_GROUP), lg, -jnp.inf)
    v0 = jnp.max(elog, axis=1, keepdims=True)
    i0 = jnp.min(jnp.where(elog == v0, lane, LANES), axis=1, keepdims=True)
    elog2 = jnp.where(lane == i0, -jnp.inf, elog)
    v1 = jnp.max(elog2, axis=1, keepdims=True)
    i1 = jnp.min(jnp.where(elog2 == v1, lane, LANES), axis=1, keepdims=True)
    e1 = jnp.exp(v1 - v0)
    w0 = pg / (1.0 + e1)
    w1 = pg * e1 / (1.0 + e1)
    out = jnp.where(lane == 0, (i0 - N_GROUPS).astype(F32), 0.0)
    out = jnp.where(lane == 1, (i1 - N_GROUPS).astype(F32), out)
    out = jnp.where(lane == 2, w0, out)
    out = jnp.where(lane == 3, w1, out)
    ri_ref[...] = out


def _post(mixed, x2, wo, g_ffn, wr, br):
    n = x2.shape[0]
    wr_hi = wr.astype(BF16)
    wr_lo = (wr - wr_hi.astype(F32)).astype(BF16)
    tm = min(POST_TM, n)
    row = lambda w: pl.BlockSpec((tm, w), lambda i: (i, 0))
    full = lambda shape: pl.BlockSpec(shape, lambda i: (0,) * len(shape))
    return pl.pallas_call(
        _post_kernel,
        grid=(n // tm,),
        in_specs=[
            row(D_MODEL), row(D_MODEL), full((D_MODEL, D_MODEL)),
            full((1, D_MODEL)), full((D_MODEL, LANES)), full((D_MODEL, LANES)), full((1, LANES)),
        ],
        out_specs=[row(D_MODEL), row(D_MODEL), row(LANES)],
        out_shape=[jax.ShapeDtypeStruct((n, D_MODEL), F32),
                   jax.ShapeDtypeStruct((n, D_MODEL), F32),
                   jax.ShapeDtypeStruct((n, LANES), F32)],
        compiler_params=_cparams(1),
        name="post",
    )(mixed, x2, wo, g_ffn, wr_hi, wr_lo, br)


def _row_gather_start(idx_ref, n_rows, src_hbm, dst_buf, sem, inline=False):
    def body(r, c):
        tok = idx_ref[0, 0, r]
        pltpu.make_async_copy(src_hbm.at[pl.ds(tok, 1)], dst_buf.at[pl.ds(r, 1)], sem).start()
        return c
    if inline:
        for r in range(n_rows):
            body(r, 0)
    else:
        lax.fori_loop(0, n_rows, body, 0, unroll=8)


def _row_gather_wait(n_rows, src_hbm, dst_buf, sem):
    pltpu.make_async_copy(src_hbm.at[pl.ds(0, n_rows)], dst_buf, sem).wait()


def _moe_kernel(te_ref, nu_ref, tok_ref, tokn_ref, h2_hbm, w1_ref, w3_ref, w2_ref, y_ref,
                xbuf, sem, w1b, w3b, w2b, *, tm):
    t = pl.program_id(0)
    n_used = nu_ref[0]
    slot = t % 2

    @pl.when((t < n_used) & ((t == 0) | (te_ref[t] != te_ref[jnp.maximum(t - 1, 0)])))
    def _():
        w1b[...] = w1_ref[0, 0].astype(BF16)
        w3b[...] = w3_ref[0, 0].astype(BF16)
        w2b[...] = w2_ref[0, 0].astype(BF16)

    @pl.when(t == 0)
    def _():
        _row_gather_start(tok_ref, tm, h2_hbm, xbuf.at[0], sem.at[0])

    @pl.when(t < n_used)
    def _():
        _row_gather_wait(tm, h2_hbm, xbuf.at[slot], sem.at[slot])
        _row_gather_start(tokn_ref, tm, h2_hbm, xbuf.at[1 - slot], sem.at[1 - slot], inline=True)
        xb = xbuf[slot].astype(BF16)
        u = _dot(xb, w1b[...])
        a = (u * _sigmoid(u)) * _dot(xb, w3b[...])
        y_ref[...] = _dot(a.astype(BF16), w2b[...])

        @pl.when(t == pl.num_programs(0) - 1)
        def _():
            _row_gather_wait(tm, h2_hbm, xbuf.at[1 - slot], sem.at[1 - slot])

    @pl.when(t == n_used)
    def _():
        _row_gather_wait(tm, h2_hbm, xbuf.at[slot], sem.at[slot])

    @pl.when(t >= n_used)
    def _():
        y_ref[...] = jnp.zeros(y_ref.shape, F32)


def _moe(h2, tile_expert, n_used, row_token, w1, w3, w2, tm):
    n_tiles = tile_expert.shape[0]
    tok3 = row_token.reshape(n_tiles, 1, tm)
    grid_spec = pltpu.PrefetchScalarGridSpec(
        num_scalar_prefetch=2,
        grid=(n_tiles,),
        in_specs=[
            pl.BlockSpec((1, 1, tm), lambda t, te, nu: (t, 0, 0), memory_space=pltpu.SMEM),
            pl.BlockSpec((1, 1, tm), lambda t, te, nu: (jnp.minimum(t + 1, n_tiles - 1), 0, 0),
                         memory_space=pltpu.SMEM),
            pl.BlockSpec(memory_space=pl.ANY),
            pl.BlockSpec((1, 1, D_MODEL, D_EXPERT), lambda t, te, nu: (0, te[t], 0, 0)),
            pl.BlockSpec((1, 1, D_MODEL, D_EXPERT), lambda t, te, nu: (0, te[t], 0, 0)),
            pl.BlockSpec((1, 1, D_EXPERT, D_MODEL), lambda t, te, nu: (0, te[t], 0, 0)),
        ],
        out_specs=pl.BlockSpec((tm, D_MODEL), lambda t, te, nu: (t, 0)),
        scratch_shapes=[pltpu.VMEM((2, tm, D_MODEL), F32), pltpu.SemaphoreType.DMA((2,)),
                        pltpu.VMEM((D_MODEL, D_EXPERT), BF16), pltpu.VMEM((D_MODEL, D_EXPERT), BF16),
                        pltpu.VMEM((D_EXPERT, D_MODEL), BF16)],
    )
    return pl.pallas_call(
        functools.partial(_moe_kernel, tm=tm),
        grid_spec=grid_spec,
        out_shape=jax.ShapeDtypeStruct((n_tiles * tm, D_MODEL), F32),
        compiler_params=_cparams(1),
        name="moe",
    )(tile_expert, n_used, tok3, tok3, h2, w1, w3, w2)


def _route(rinfo, n_tiles, tm):
    n = rinfo.shape[0]
    eid = rinfo[:, 0:2].astype(I32).reshape(-1)
    order = jnp.argsort(eid, stable=True).astype(I32)
    inv = jnp.argsort(order).astype(I32)
    onehot = eid[:, None] == jnp.arange(N_EXPERTS, dtype=I32)[None, :]
    counts = jnp.sum(onehot.astype(I32), axis=0)
    tiles_e = (counts + tm - 1) // tm
    tile_end = jnp.cumsum(tiles_e)
    row_start = (tile_end - tiles_e) * tm
    grp_start = jnp.cumsum(counts) - counts
    shift = row_start - grp_start
    pos = inv + jnp.sum(jnp.where(onehot, shift[None, :], 0), axis=1)
    n_used = tile_end[-1]
    tile_ids = jnp.arange(n_tiles, dtype=I32)
    te_raw = jnp.sum((tile_end[None, :] <= tile_ids[:, None]).astype(I32), axis=1)
    last_e = jnp.sum((tile_end <= n_used - 1).astype(I32))
    tile_expert = jnp.minimum(te_raw, last_e)
    te_c = jnp.minimum(te_raw, N_EXPERTS - 1)
    src = jnp.arange(n_tiles * tm, dtype=I32) - jnp.repeat(shift[te_c], tm)
    lo = jnp.repeat(grp_start[te_c], tm)
    hi = lo + jnp.repeat(counts[te_c], tm)
    valid = (src >= lo) & (src < hi)
    row_token = jnp.where(valid, order[jnp.clip(src, 0, 2 * n - 1)] // 2, 0)
    return tile_expert, n_used.reshape(1).astype(I32), row_token, pos.reshape(n, 2)


def _final_kernel(pos_ref, posn_ref, y_hbm, x1_ref, ri_ref, pe_ref, gp_ref, wpg_ref, wp_ref, gfin_ref,
                  o_ref, ybuf, sem):
    t = pl.program_id(0)
    nt = pl.num_programs(0)
    slot = t % 2
    tm = TOK_TM

    @pl.when(t == 0)
    def _():
        _row_gather_start(pos_ref, 2 * tm, y_hbm, ybuf.at[0], sem.at[0])

    _row_gather_wait(2 * tm, y_hbm, ybuf.at[slot], sem.at[slot])
    _row_gather_start(posn_ref, 2 * tm, y_hbm, ybuf.at[1 - slot], sem.at[1 - slot], inline=True)
    ri = ri_ref[...]
    moe = ri[:, 2:3] * ybuf[slot, 0:tm] + ri[:, 3:4] * ybuf[slot, tm:2 * tm]
    x2 = x1_ref[...] + moe
    hp = x2 * lax.rsqrt(jnp.mean(x2 * x2, axis=-1, keepdims=True) + EPS) * gp_ref[...]
    gate = _sigmoid(_dot(hp.astype(BF16), wpg_ref[...]))
    x3 = x2 + gate * _dot(pe_ref[...].astype(BF16), wp_ref[...])
    o_ref[...] = x3 * lax.rsqrt(jnp.mean(x3 * x3, axis=-1, keepdims=True) + EPS) * gfin_ref[...]

    @pl.when(t == nt - 1)
    def _():
        _row_gather_wait(2 * tm, y_hbm, ybuf.at[1 - slot], sem.at[1 - slot])


def _final(x1, rinfo, pos, y_sorted, pe2, g_ple, wpg, wp, g_final):
    n = x1.shape[0]
    tm = TOK_TM
    nt = n // tm
    pos3 = pos.reshape(nt, tm, 2).transpose(0, 2, 1).reshape(nt, 1, 2 * tm)
    row = lambda w: pl.BlockSpec((tm, w), lambda i: (i, 0))
    full = lambda shape: pl.BlockSpec(shape, lambda i: (0,) * len(shape))
    return pl.pallas_call(
        _final_kernel,
        grid=(nt,),
        in_specs=[
            pl.BlockSpec((1, 1, 2 * tm), lambda i: (i, 0, 0), memory_space=pltpu.SMEM),
            pl.BlockSpec((1, 1, 2 * tm), lambda i: (jnp.minimum(i + 1, nt - 1), 0, 0),
                         memory_space=pltpu.SMEM),
            pl.BlockSpec(memory_space=pl.ANY),
            row(D_MODEL), row(LANES), row(PLE_DIM),
            full((1, D_MODEL)), full((D_MODEL, D_MODEL)), full((PLE_DIM, D_MODEL)), full((1, D_MODEL)),
        ],
        out_specs=row(D_MODEL),
        out_shape=jax.ShapeDtypeStruct((n, D_MODEL), F32),
        scratch_shapes=[pltpu.VMEM((2, 2 * tm, D_MODEL), F32), pltpu.SemaphoreType.DMA((2,))],
        compiler_params=_cparams(1),
        name="final",
    )(pos3, pos3, y_sorted, x1, rinfo, pe2, g_ple, wpg, wp, g_final)


def _pack_w_in(w):
    offs = np.concatenate([[0], np.cumsum(PROJ_SIZES)])
    qa, ka, va, qi, ki, wi, qm, km, vm, ig, fg, og, ga, gb = [w[:, offs[k]:offs[k + 1]] for k in range(14)]
    small = jnp.concatenate([ki, wi, ig, fg], axis=1)
    small = jnp.pad(small, ((0, 0), (0, SM_WIDTH - small.shape[1])))
    return jnp.concatenate([qa, qi, qm, km, vm, og, ga, gb, ka, va, small], axis=1).astype(BF16)


def _prep_weights(g_mix, w_in, w_att_out, w_mlstm_out, w_out, g_ffn, router_gw, router_gb, router_ew,
                  router_eb, w1, w3, w2, g_ple, w_ple, w_ple_gate, g_final):
    wr = jnp.concatenate([router_gw, router_ew], axis=1).astype(F32)
    wr = jnp.pad(wr, ((0, 0), (0, LANES - wr.shape[1])))
    br = jnp.concatenate([router_gb, router_eb]).astype(F32)
    br = jnp.pad(br, (0, LANES - br.shape[0]))[None, :]
    return dict(
        g_mix=g_mix.astype(F32)[None, :], w_pack=_pack_w_in(w_in),
        wa=w_att_out.astype(BF16), wb=w_mlstm_out.astype(BF16), wo=w_out.astype(BF16),
        g_ffn=g_ffn.astype(F32)[None, :], wr=wr, br=br,
        w1=w1.reshape((1,) + w1.shape[-3:]), w3=w3.reshape((1,) + w3.shape[-3:]),
        w2=w2.reshape((1,) + w2.shape[-3:]),
        g_ple=g_ple.astype(F32)[None, :], wp=w_ple.astype(BF16), wpg=w_ple_gate.astype(BF16),
        g_final=g_final.astype(F32)[None, :])


def _layer(x, pe, past, conv_prev, c0, n0, m0, wts, conv_w, conv_b, b_igate, b_fgate, mh_gain, qb, ln):
    bsz, t, _ = x.shape
    n = bsz * t
    x2 = x.reshape(n, D_MODEL)
    z, kvb, k_new, v_new, ki_new = _inproj(x2, wts["g_mix"], wts["w_pack"])
    z3 = z.reshape(bsz, t, D_PACK)
    kv3 = kvb.reshape(bsz, t, KV_PACK)
    att = _dsa(z3, kv3, past, qb, min(DSA_SB, qb))
    hm, conv_new, c_new, n_new, m_new = _mlstm(z3, conv_prev, c0, n0, m0, conv_w, conv_b,
                                               b_igate, b_fgate, mh_gain, ln)
    mixed = _mix(att.reshape(n, ATT_WIDTH), hm.reshape(n, M_WIDTH), z, wts["wa"], wts["wb"])
    x1, h2, rinfo = _post(mixed, x2, wts["wo"], wts["g_ffn"], wts["wr"], wts["br"])
    tm = MOE_TM if 2 * n >= 4 * MOE_TM * N_EXPERTS else MOE_TM_SMALL
    n_tiles = (2 * n + N_EXPERTS * (tm - 1) + tm - 1) // tm
    tile_expert, n_used, row_token, pos = _route(rinfo, n_tiles, tm)
    y_sorted = _moe(h2, tile_expert, n_used, row_token, wts["w1"], wts["w3"], wts["w2"], tm)
    y = _final(x1, rinfo, pos, y_sorted, pe.reshape(n, PLE_DIM).astype(F32),
               wts["g_ple"], wts["wpg"], wts["wp"], wts["g_final"])
    k_new = k_new.reshape(bsz, t, N_KV_HEADS, HEAD_DIM)
    v_new = v_new.reshape(bsz, t, N_KV_HEADS, HEAD_DIM)
    ki_new = ki_new.reshape(bsz, t, IDX_DIM)
    return y.reshape(bsz, t, D_MODEL), (k_new, v_new, ki_new, conv_new, c_new, n_new, m_new)


def kernel(x_prompt, x_sample, cache_k, cache_v, cache_kidx, state_conv, state_C, state_n, state_m,
           p_prompt, p_sample, g_mix, w_in, conv_w, conv_b, b_igate, b_fgate, mh_gain,
           w_att_out, w_mlstm_out, w_out, g_ffn, router_gw, router_gb, router_ew, router_eb,
           w1, w3, w2, g_ple, w_ple, w_ple_gate, g_final):
    assert g_mix.shape[0] == 1, "single-layer step"
    bp, tp, _ = x_prompt.shape
    bs, ts, _ = x_sample.shape
    sdt = state_C.dtype
    wts = _prep_weights(g_mix[0], w_in[0], w_att_out[0], w_mlstm_out[0], w_out[0], g_ffn[0],
                        router_gw[0], router_gb[0], router_ew[0], router_eb[0], w1, w3, w2,
                        g_ple[0], w_ple[0], w_ple_gate[0], g_final)
    mix = (conv_w[0], conv_b[0], b_igate[0], b_fgate[0], mh_gain[0])
    yp, st_p = _layer(
        x_prompt, p_prompt[0], None,
        jnp.zeros((bp, CONV_W - 1, 2 * M_WIDTH), F32),
        jnp.zeros((bp, M_HEADS, M_HEAD_DIM, M_HEAD_DIM), F32),
        jnp.zeros((bp, M_HEADS, M_HEAD_DIM), F32),
        jnp.zeros((bp, M_HEADS), F32),
        wts, *mix, qb=min(DSA_QB, tp), ln=min(256, tp))
    plen = cache_k.shape[2]
    past = (cache_k[0], cache_v[0], cache_kidx[0])
    ys, st_s = _layer(
        x_sample, p_sample[0], past, state_conv[0], state_C[0], state_n[0], state_m[0],
        wts, *mix, qb=ts, ln=ts)
    outs_p = [s[None] for s in st_p]
    outs_s = [s[None] for s in st_s]
    for lst in (outs_p, outs_s):
        for k in (4, 5, 6):
            lst[k] = lst[k].astype(sdt)
    return (yp, ys, *outs_p, *outs_s)
```

```python
import functools

import numpy as np
import jax
import jax.numpy as jnp
from jax import lax
from jax.experimental import pallas as pl
from jax.experimental.pallas import tpu as pltpu

F32 = jnp.float32
BF16 = jnp.bfloat16
I32 = jnp.int32

D_MODEL = 2048
CHUNK = 64
CHUNK_SHIFT = 6
assert 1 << CHUNK_SHIFT == CHUNK
N_HEADS = 8
N_KV_HEADS = 4
HEAD_DIM = 128
ATT_WIDTH = N_HEADS * HEAD_DIM
KV_WIDTH = N_KV_HEADS * HEAD_DIM
IDX_HEADS = 16
IDX_DIM = 64
TOPK_MAX = 256
M_HEADS = 4
M_HEAD_DIM = 256
M_WIDTH = M_HEADS * M_HEAD_DIM
CONV_W = 4
N_GROUPS = 4
EXP_PER_GROUP = 8
N_EXPERTS = N_GROUPS * EXP_PER_GROUP
D_EXPERT = 512
PLE_DIM = 256
EPS = 1e-6
PROJ_SIZES = (ATT_WIDTH, KV_WIDTH, KV_WIDTH, IDX_HEADS * IDX_DIM, IDX_DIM, IDX_HEADS,
              M_WIDTH, M_WIDTH, M_WIDTH, M_HEADS, M_HEADS, M_WIDTH, D_MODEL, D_MODEL)

LANES = 128
SUBLANES = 8
VMEM_LIMIT = 56 * 1024 * 1024

OFF_QA, OFF_QI, OFF_QM, OFF_KM, OFF_VM, OFF_OG = 0, 1024, 2048, 3072, 4096, 5120
OFF_GA, OFF_GB, OFF_KA, OFF_VA, OFF_SM = 6144, 8192, 10240, 10752, 11264
SM_WIDTH = 512
D_PACK = OFF_SM + SM_WIDTH
SM_KI, SM_WI, SM_IG, SM_FG = 0, 64, 80, 84
PROJ_TN = 512
KV_BLK0 = OFF_KA // PROJ_TN
KV_PACK = D_PACK - OFF_KA

LOG2E = 1.4426950408889634
INT_MIN = np.int32(-2 ** 31)
NEG_BIG = -1e30

DSA_KT = 512
DSA_QB = 128
DSA_SB = 128
MOE_TM = 512
MOE_TM_SMALL = 128
TOK_TM = 256
POST_TM = 512


def _cparams(n_axes):
    return pltpu.CompilerParams(dimension_semantics=("arbitrary",) * n_axes,
                                vmem_limit_bytes=VMEM_LIMIT)


def _dot(a, b):
    return jnp.dot(a, b, preferred_element_type=F32)


def _dot_nt(a, b):
    return lax.dot_general(a, b, (((1,), (1,)), ((), ())), preferred_element_type=F32)


def _dot_tn(a, b):
    return lax.dot_general(a, b, (((0,), (0,)), ((), ())), preferred_element_type=F32)


def _tile_rows(t, size):
    start = t * size
    return pl.ds(start if isinstance(start, int) else pl.multiple_of(start, size), size)


def _head_rows(t, size, g):
    start = t * (size * N_KV_HEADS)
    if not isinstance(start, int):
        start = pl.multiple_of(start, size * N_KV_HEADS)
    return pl.ds(start + g, size, stride=N_KV_HEADS)


def _sigmoid(x):
    return 1.0 / (1.0 + jnp.exp(-x))


def _split3(x):
    hi = x.astype(BF16)
    r1 = x - hi.astype(F32)
    mid = r1.astype(BF16)
    lo = (r1 - mid.astype(F32)).astype(BF16)
    return hi, mid, lo


def _inproj_kernel(x_ref, g_ref, w_ref, z_ref, kv_ref, k_ref, v_ref, ki_ref, h_ref):
    j = pl.program_id(1)

    @pl.when(j == 0)
    def _():
        x = x_ref[...]
        r = lax.rsqrt(jnp.mean(x * x, axis=-1, keepdims=True) + EPS)
        h_ref[...] = (x * r * g_ref[...]).astype(BF16)

    acc = _dot(h_ref[...], w_ref[...])
    z_ref[...] = acc

    @pl.when(j >= KV_BLK0)
    def _():
        kv_ref[...] = acc.astype(BF16)

    tm = acc.shape[0]

    def store_heads(ref):
        for g in range(N_KV_HEADS):
            ref[pl.ds(g, tm, stride=N_KV_HEADS), :] = acc[:, g * HEAD_DIM:(g + 1) * HEAD_DIM]

    @pl.when(j == KV_BLK0)
    def _():
        store_heads(k_ref)

    @pl.when(j == KV_BLK0 + 1)
    def _():
        store_heads(v_ref)

    @pl.when(j == KV_BLK0 + 2)
    def _():
        ki_ref[...] = acc[:, SM_KI:SM_KI + IDX_DIM]


def _inproj(x2d, g, w_pack):
    n = x2d.shape[0]
    tm = min(1024, n)
    grid = (n // tm, D_PACK // PROJ_TN)
    return pl.pallas_call(
        _inproj_kernel,
        grid=grid,
        in_specs=[
            pl.BlockSpec((tm, D_MODEL), lambda i, j: (i, 0)),
            pl.BlockSpec((1, D_MODEL), lambda i, j: (0, 0)),
            pl.BlockSpec((D_MODEL, PROJ_TN), lambda i, j: (0, j)),
        ],
        out_specs=[
            pl.BlockSpec((tm, PROJ_TN), lambda i, j: (i, j)),
            pl.BlockSpec((tm, PROJ_TN), lambda i, j: (i, jnp.maximum(j - KV_BLK0, 0))),
            pl.BlockSpec((tm * N_KV_HEADS, HEAD_DIM), lambda i, j: (i, 0)),
            pl.BlockSpec((tm * N_KV_HEADS, HEAD_DIM), lambda i, j: (i, 0)),
            pl.BlockSpec((tm, IDX_DIM), lambda i, j: (i, 0)),
        ],
        out_shape=[jax.ShapeDtypeStruct((n, D_PACK), F32),
                   jax.ShapeDtypeStruct((n, KV_PACK), BF16),
                   jax.ShapeDtypeStruct((n * N_KV_HEADS, HEAD_DIM), F32),
                   jax.ShapeDtypeStruct((n * N_KV_HEADS, HEAD_DIM), F32),
                   jax.ShapeDtypeStruct((n, IDX_DIM), F32)],
        scratch_shapes=[pltpu.VMEM((tm, D_MODEL), BF16)],
        compiler_params=_cparams(2),
        name="inproj",
    )(x2d, g, w_pack)


def _dsa_kernel(*refs, has_past, qblk, sb, n_past_tiles, n_new_tiles, t_valid, past_len, topk):
    n_in = 9 if has_past else 6
    if has_past:
        qa_ref, qi_ref, sm_ref, kip_ref, kp_ref, vp_ref, kin_ref, kn_ref, vn_ref = refs[:n_in]
    else:
        qa_ref, qi_ref, sm_ref, kin_ref, kn_ref, vn_ref = refs[:n_in]
    o_ref, keys_ref, kT_ref, d_ref, s_ref = refs[n_in:n_in + 5]
    state = refs[n_in + 5:]
    mx_refs, acc_refs = state[0::2], state[1::2]
    kt = DSA_KT
    nsub = qblk // sb
    i = pl.program_id(1)
    q0 = past_len + i * qblk
    j_end = ((q0 + qblk - 1) // CHUNK + 1) * CHUNK - past_len
    nk_new = jnp.minimum((j_end + kt - 1) // kt, n_new_tiles)
    n_tiles = n_past_tiles + nk_new
    lane_sb = lax.broadcasted_iota(I32, (sb, kt), 1)

    for sub in range(nsub):
        r0 = sub * sb
        qchunk = (q0 + r0 + lax.broadcasted_iota(I32, (sb, 1), 0)) >> CHUNK_SHIFT
        wsc = sm_ref[0, r0:r0 + sb, SM_WI:SM_WI + IDX_HEADS] * (IDX_HEADS ** -0.5 * IDX_DIM ** -0.5)
        wcols = [wsc[:, h:h + 1] for h in range(IDX_HEADS)]
        qi_blk = qi_ref[0, r0:r0 + sb, :]
        qi_all = jnp.concatenate([qi_blk[:, h * IDX_DIM:(h + 1) * IDX_DIM] for h in range(IDX_HEADS)],
                                 axis=0).astype(BF16)

        def score_tile(ki_t, kpos0, jvalid0, col, r0=r0, qchunk=qchunk, wcols=wcols, qi_all=qi_all):
            d_ref[...] = _dot_nt(qi_all, ki_t)
            acc = wcols[0] * jnp.maximum(d_ref[0:sb], 0.0)
            for h in range(1, IDX_HEADS):
                acc = acc + wcols[h] * jnp.maximum(d_ref[h * sb:(h + 1) * sb], 0.0)
            bits = lax.bitcast_convert_type(acc + 0.0, I32)
            key = bits ^ ((bits >> 31) & np.int32(0x7FFFFFFF))
            adm = (((kpos0 + lane_sb) >> CHUNK_SHIFT) <= qchunk) & (jvalid0 + lane_sb < t_valid)
            keys_ref[col, r0:r0 + sb, :] = jnp.where(adm, key, INT_MIN)

        if has_past:
            def p1_past(t, c, score_tile=score_tile):
                ki_t = kip_ref[0, _tile_rows(t, kt), :].astype(BF16)
                score_tile(ki_t, t * kt, -(2 ** 30), t)
                return c
            lax.fori_loop(0, n_past_tiles, p1_past, 0)

        def p1_new(t, c, score_tile=score_tile):
            ki_t = kin_ref[0, _tile_rows(t, kt), 0:IDX_DIM]
            score_tile(ki_t, past_len + t * kt, t * kt, n_past_tiles + t)
            return c
        lax.fori_loop(0, nk_new, p1_new, 0)

    qb = qblk
    lane = lax.broadcasted_iota(I32, (qb, kt), 1)

    def count(fn):
        def body(t, part):
            kall = keys_ref[t]
            for s in range(kt // LANES):
                ks = kall[:, s * LANES:(s + 1) * LANES]
                part = part + jnp.where(fn(ks, t, s), 1.0, 0.0)
            return part
        part = lax.fori_loop(0, n_tiles, body, jnp.zeros((qb, LANES), F32))
        return jnp.sum(part, axis=1, keepdims=True)

    kf = float(topk)

    bits_per_check = 4

    def search(count_ge, shape):
        def sgroup(carry):
            grp, tu, done, _ = carry
            for b in range(bits_per_check):
                shift = jnp.asarray(31 - b, I32) - grp * bits_per_check
                cand = tu | lax.shift_left(np.int32(1), shift)
                cnt = count_ge(cand ^ INT_MIN)
                tu = jnp.where((cnt >= kf) & (done == 0.0), cand, tu)
                done = jnp.where(cnt == kf, 1.0, done)
            return grp + 1, tu, done, jnp.min(done)

        def scond(carry):
            grp, _, _, all_done = carry
            return (grp < 32 // bits_per_check) & (all_done == 0.0)

        _, tu, _, _ = lax.while_loop(
            scond, sgroup, (jnp.int32(0), jnp.zeros(shape, I32), jnp.zeros(shape, F32), jnp.float32(0.0)))
        return tu ^ INT_MIN

    if qb % LANES == 0:
        def xpose(t, c):
            kT_ref[t] = lax.bitcast_convert_type(lax.bitcast_convert_type(keys_ref[t], F32).T, I32)
            return c
        lax.fori_loop(0, n_tiles, xpose, 0)

        acc_rows = 4 * SUBLANES

        def count_t(fn):
            def body(t, part):
                c = jnp.where(fn(kT_ref[t]), 1.0, 0.0)
                return part + jnp.sum(c.reshape(kt // acc_rows, acc_rows, qb), axis=0)
            part = lax.fori_loop(0, n_tiles, body, jnp.zeros((acc_rows, qb), F32))
            return jnp.sum(part, axis=0, keepdims=True)

        thr_t = search(lambda cs: count_t(lambda k: k >= cs), (1, qb))
        cnt_ge_t = count_t(lambda k: k >= thr_t)
        any_tie = jnp.max(jnp.where((cnt_ge_t > kf) & (thr_t > INT_MIN), 1.0, 0.0)) > 0.0
        eye = lax.broadcasted_iota(I32, (qb, qb), 0) == lax.broadcasted_iota(I32, (qb, qb), 1)
        hi = jnp.sum(jnp.where(eye, (thr_t >> 16).astype(F32), 0.0), axis=1, keepdims=True)
        lo = jnp.sum(jnp.where(eye, (thr_t & np.int32(0xFFFF)).astype(F32), 0.0), axis=1, keepdims=True)
        thr = lax.shift_left(hi.astype(I32), np.int32(16)) | lo.astype(I32)
    else:
        thr = search(lambda cs: count(lambda k, t, s: k >= cs), (qb, 1))
        cnt_ge0 = count(lambda k, t, s: k >= thr)
        any_tie = jnp.max(jnp.where((cnt_ge0 > kf) & (thr > INT_MIN), 1.0, 0.0)) > 0.0

    @pl.when(any_tie)
    def _():
        cnt_ge = count(lambda k, t, s: k >= thr)
        cnt_gt = count(lambda k, t, s: k > thr)
        tie = (cnt_ge > kf) & (thr > INT_MIN)
        need = kf - cnt_gt
        lane1 = lax.broadcasted_iota(I32, (qb, LANES), 1)

        def jbody(it, a):
            cand = a | lax.shift_left(np.int32(1), jnp.asarray(15 - it, I32))
            cnt = count(lambda k, t, s: (k == thr) & (t * kt + s * LANES + lane1 < cand))
            return jnp.where(cnt < need, cand, a)
        a = lax.fori_loop(0, 16, jbody, jnp.zeros((qb, 1), I32))

        def drop(t, c):
            k = keys_ref[t]
            keys_ref[t] = jnp.where(tie & (k == thr) & (t * kt + lane > a), INT_MIN, k)
            return c
        lax.fori_loop(0, n_tiles, drop, 0)

    thr_eff = jnp.maximum(thr, INT_MIN + 1)

    def hs(g):
        return slice(g * HEAD_DIM, (g + 1) * HEAD_DIM)

    ones_blk = jnp.ones((kt, HEAD_DIM), BF16)

    for sub in range(nsub):
        r0 = sub * sb
        thr_sub = thr_eff[r0:r0 + sb]
        q = qa_ref[0, r0:r0 + sb, :] * (HEAD_DIM ** -0.5 * LOG2E)
        q2 = []
        for g in range(N_KV_HEADS):
            a = q[:, (2 * g) * HEAD_DIM:(2 * g + 1) * HEAD_DIM]
            b = q[:, (2 * g + 1) * HEAD_DIM:(2 * g + 2) * HEAD_DIM]
            q2.append(jnp.concatenate([a, b], axis=0).astype(BF16))
        for g in range(N_KV_HEADS):
            mx_refs[g][...] = jnp.full(mx_refs[g].shape, -jnp.inf, F32)
            acc_refs[g][...] = jnp.zeros(acc_refs[g].shape, F32)

        def logits(col, k_fn, r0=r0, thr_sub=thr_sub, q2=q2):
            bias = jnp.where(keys_ref[col, r0:r0 + sb, :] >= thr_sub, 0.0, NEG_BIG)
            bias2 = jnp.concatenate([bias, bias], axis=0)
            for g in range(N_KV_HEADS):
                s = _dot_nt(q2[g], k_fn(g)) + bias2
                s_ref[g, col] = s
                mx = mx_refs[g][...]
                for c in range(kt // LANES):
                    mx = jnp.maximum(mx, s[:, c * LANES:(c + 1) * LANES])
                mx_refs[g][...] = mx

        def weighted(col, v_fn, m_rows):
            for g in range(N_KV_HEADS):
                p = jnp.exp2(s_ref[g, col] - m_rows[g]).astype(BF16)
                v_aug = jnp.concatenate([v_fn(g), ones_blk], axis=1)
                acc_refs[g][...] = acc_refs[g][...] + _dot(p, v_aug)

        if has_past:
            def pa_past(t, c, logits=logits):
                rows = _tile_rows(t, kt)
                logits(t, lambda g: kp_ref[0, _head_rows(t, kt, g), :].astype(BF16))
                return c
            lax.fori_loop(0, n_past_tiles, pa_past, 0)

        def pa_new(t, c, logits=logits):
            rows = _tile_rows(t, kt)
            logits(n_past_tiles + t, lambda g: kn_ref[0, rows, hs(g)])
            return c
        lax.fori_loop(0, nk_new, pa_new, 0)

        m_rows = [jnp.max(mx_refs[g][...], axis=1, keepdims=True) for g in range(N_KV_HEADS)]

        if has_past:
            def pb_past(t, c, m_rows=m_rows, weighted=weighted):
                rows = _tile_rows(t, kt)
                weighted(t, lambda g: vp_ref[0, _head_rows(t, kt, g), :].astype(BF16), m_rows)
                return c
            lax.fori_loop(0, n_past_tiles, pb_past, 0)

        def pb_new(t, c, m_rows=m_rows, weighted=weighted):
            rows = _tile_rows(t, kt)
            weighted(n_past_tiles + t, lambda g: vn_ref[0, rows, hs(g)], m_rows)
            return c
        lax.fori_loop(0, nk_new, pb_new, 0)

        for g in range(N_KV_HEADS):
            acc = acc_refs[g][...]
            o = acc[:, 0:HEAD_DIM] / acc[:, HEAD_DIM:2 * HEAD_DIM]
            o_ref[0, r0:r0 + sb, (2 * g) * HEAD_DIM:(2 * g + 1) * HEAD_DIM] = o[0:sb].astype(BF16)
            o_ref[0, r0:r0 + sb, (2 * g + 1) * HEAD_DIM:(2 * g + 2) * HEAD_DIM] = o[sb:2 * sb].astype(BF16)


def _dsa(z3, kv3, past, qb, sb):
    bsz, t, _ = z3.shape
    kt = DSA_KT
    nb = t // qb
    nsub = qb // sb
    has_past = past is not None
    past_len = past[0].shape[1] if has_past else 0
    topk = min(TOPK_MAX, (past_len + t) // 4)
    t_pad = -(-t // kt) * kt
    if t_pad != t:
        kv3 = jnp.pad(kv3, ((0, 0), (0, t_pad - t), (0, 0)))
    n_new_tiles = t_pad // kt
    n_past_tiles = past_len // kt
    qi_width = IDX_HEADS * IDX_DIM
    in_specs = [
        pl.BlockSpec((1, qb, ATT_WIDTH), lambda b, i: (b, i, OFF_QA // ATT_WIDTH)),
        pl.BlockSpec((1, qb, qi_width), lambda b, i: (b, i, OFF_QI // qi_width)),
        pl.BlockSpec((1, qb, LANES), lambda b, i: (b, i, OFF_SM // LANES)),
    ]
    args = [z3, z3, z3]
    if has_past:
        pk, pv, pki = past
        pk = pk.reshape(bsz, past_len * N_KV_HEADS, HEAD_DIM)
        pv = pv.reshape(bsz, past_len * N_KV_HEADS, HEAD_DIM)
        in_specs += [
            pl.BlockSpec((1, past_len, IDX_DIM), lambda b, i: (b, 0, 0)),
            pl.BlockSpec((1, past_len * N_KV_HEADS, HEAD_DIM), lambda b, i: (b, 0, 0)),
            pl.BlockSpec((1, past_len * N_KV_HEADS, HEAD_DIM), lambda b, i: (b, 0, 0)),
        ]
        args += [pki, pk, pv]
    in_specs += [
        pl.BlockSpec((1, t_pad, LANES), lambda b, i: (b, 0, 2 * KV_WIDTH // LANES)),
        pl.BlockSpec((1, t_pad, KV_WIDTH), lambda b, i: (b, 0, 0)),
        pl.BlockSpec((1, t_pad, KV_WIDTH), lambda b, i: (b, 0, 1)),
    ]
    args += [kv3, kv3, kv3]
    kern = functools.partial(_dsa_kernel, has_past=has_past, qblk=qb, sb=sb, n_past_tiles=n_past_tiles,
                             n_new_tiles=n_new_tiles, t_valid=t, past_len=past_len, topk=topk)
    return pl.pallas_call(
        kern,
        grid=(bsz, nb),
        in_specs=in_specs,
        out_specs=pl.BlockSpec((1, qb, ATT_WIDTH), lambda b, i: (b, i, 0)),
        out_shape=jax.ShapeDtypeStruct((bsz, t, ATT_WIDTH), BF16),
        scratch_shapes=[pltpu.VMEM((n_past_tiles + n_new_tiles, qb, kt), I32),
                        pltpu.VMEM((n_past_tiles + n_new_tiles, kt, qb) if qb % LANES == 0
                                   else (1, SUBLANES, LANES), I32),
                        pltpu.VMEM((IDX_HEADS * sb, kt), F32),
                        pltpu.VMEM((N_KV_HEADS, n_past_tiles + n_new_tiles, 2 * sb, kt), F32)] + [
            pltpu.VMEM((2 * sb, LANES), F32), pltpu.VMEM((2 * sb, 2 * HEAD_DIM), F32)] * N_KV_HEADS,
        compiler_params=_cparams(2),
        name="dsa_past" if has_past else "dsa",
    )(*args)


def _mlstm_kernel(qk_ref, vm_ref, og_ref, sm_ref, gt_ref, cprev_ref, c0_ref, n0_ref, m0_ref,
                  cw_ref, cb_ref, gbrow_ref, gbcol_ref, gain_ref,
                  hm_ref, cnew_ref, cout_ref, nout_ref, mout_ref,
                  ubuf, c_s, n_s, m_s, *, ln):
    c = pl.program_id(1)
    nc = pl.num_programs(1)
    pad = SUBLANES

    @pl.when(c == 0)
    def _():
        ubuf[0:pad] = cprev_ref[0]
        c_s[...] = c0_ref[0]
        n_s[...] = n0_ref[0]
        m_s[...] = m0_ref[0]

    @pl.when(c > 0)
    def _():
        ubuf[0:pad] = ubuf[ln:ln + pad]

    ubuf[pad:pad + ln] = qk_ref[0]
    y = cb_ref[...] + ubuf[pad - 3:pad - 3 + ln] * cw_ref[0:1]
    for j in range(1, CONV_W):
        y = y + ubuf[pad - 3 + j:pad - 3 + j + ln] * cw_ref[j:j + 1]
    qkc = y * _sigmoid(y)
    cnew_ref[0] = ubuf[ln:ln + pad]

    pre_c = sm_ref[0] + gbrow_ref[...]
    pre_r = gt_ref[0, 0] + gbcol_ref[...][:, 0:1]

    def log_sigmoid(x):
        return jnp.minimum(x, 0.0) - jnp.log(1.0 + jnp.exp(-jnp.abs(x)))

    lf_c = log_sigmoid(pre_c)
    lf_r = log_sigmoid(pre_r)
    ri = lax.broadcasted_iota(I32, (ln, ln), 0)
    ci = lax.broadcasted_iota(I32, (ln, ln), 1)
    causal = ci <= ri
    tril = jnp.where(causal, 1.0, 0.0).astype(BF16)
    triu = jnp.where(ri <= ci, 1.0, 0.0).astype(BF16)
    b_c = sum(_dot(tril, p) for p in _split3(lf_c))
    b_r = sum(_dot(p, triu) for p in _split3(lf_r))

    vm = vm_ref[0]
    og = og_ref[0]
    for h in range(M_HEADS):
        hs = slice(h * M_HEAD_DIM, (h + 1) * M_HEAD_DIM)
        qf = qkc[:, hs]
        kf = qkc[:, M_WIDTH + h * M_HEAD_DIM:M_WIDTH + (h + 1) * M_HEAD_DIM] * (M_HEAD_DIM ** -0.5)
        qb16 = qf.astype(BF16)
        kb16 = kf.astype(BF16)
        vb16 = vm[:, hs].astype(BF16)
        bcol = b_c[:, SM_FG + h:SM_FG + h + 1]
        igcol = pre_c[:, SM_IG + h:SM_IG + h + 1]
        brow = b_r[M_HEADS + h:M_HEADS + h + 1, :]
        igrow = pre_r[h:h + 1, :]
        blast = bcol[ln - 1:ln, :]
        m_prev = m_s[h:h + 1, 0:1]
        dmat = jnp.where(causal, bcol - brow + igrow, -jnp.inf)
        m_inter = bcol + m_prev
        m_i = jnp.maximum(m_inter, jnp.max(dmat, axis=1, keepdims=True))
        s = _dot_nt(qb16, kb16) * jnp.exp(dmat - m_i)
        scale = jnp.exp(m_inter - m_i)
        c_prev = c_s[h]
        n_prev = n_s[h:h + 1, :]
        num = _dot(s.astype(BF16), vb16) + scale * _dot(qb16, c_prev.astype(BF16))
        den = jnp.sum(s, axis=1, keepdims=True) + scale * jnp.sum(qf * n_prev, axis=1, keepdims=True)
        hh = num / jnp.maximum(jnp.abs(den), jnp.exp(-m_i))
        m_new = m_i[ln - 1:ln, :]
        decay = jnp.exp(blast + m_prev - m_new)
        wcol = jnp.exp(blast - bcol + igcol - m_new)
        kw = kf * wcol
        c_s[h] = decay * c_prev + _dot_tn(kw.astype(BF16), vb16)
        n_s[h:h + 1, :] = decay * n_prev + jnp.sum(kw, axis=0, keepdims=True)
        m_s[h:h + 1, :] = jnp.broadcast_to(m_new, (1, LANES))
        hn = hh * lax.rsqrt(jnp.mean(hh * hh, axis=1, keepdims=True) + EPS) * gain_ref[:, hs]
        hm_ref[0, :, hs] = (hn * _sigmoid(og[:, hs])).astype(BF16)

    @pl.when(c == nc - 1)
    def _():
        cout_ref[0] = c_s[...]
        nout_ref[0] = n_s[...]
        mout_ref[0] = m_s[...]


def _mlstm(z3, conv_prev, c0, n0, m0, conv_w, conv_b, b_igate, b_fgate, mh_gain, ln):
    bsz, t, _ = z3.shape
    nc = t // ln
    pad = SUBLANES
    gt = z3[:, :, OFF_SM + SM_IG:OFF_SM + SM_IG + 2 * M_HEADS]
    gt = gt.reshape(bsz, nc, ln, 2 * M_HEADS).transpose(0, 1, 3, 2)
    cprev = jnp.pad(conv_prev.astype(F32), ((0, 0), (pad - (CONV_W - 1), 0), (0, 0)))
    gbias = jnp.concatenate([b_igate, b_fgate]).astype(F32)
    gbrow = jnp.zeros((1, LANES), F32).at[0, SM_IG:SM_IG + 2 * M_HEADS].set(gbias)
    gbcol = jnp.broadcast_to(gbias[:, None], (2 * M_HEADS, LANES))
    m0b = jnp.broadcast_to(m0.astype(F32)[:, :, None], (bsz, M_HEADS, LANES))
    kern = functools.partial(_mlstm_kernel, ln=ln)
    full = lambda shape: pl.BlockSpec(shape, lambda b, c: (0,) * len(shape))
    hm, cnew, cout, nout, mout = pl.pallas_call(
        kern,
        grid=(bsz, nc),
        in_specs=[
            pl.BlockSpec((1, ln, 2 * M_WIDTH), lambda b, c: (b, c, OFF_QM // (2 * M_WIDTH))),
            pl.BlockSpec((1, ln, M_WIDTH), lambda b, c: (b, c, OFF_VM // M_WIDTH)),
            pl.BlockSpec((1, ln, M_WIDTH), lambda b, c: (b, c, OFF_OG // M_WIDTH)),
            pl.BlockSpec((1, ln, LANES), lambda b, c: (b, c, OFF_SM // LANES)),
            pl.BlockSpec((1, 1, 2 * M_HEADS, ln), lambda b, c: (b, c, 0, 0)),
            pl.BlockSpec((1, pad, 2 * M_WIDTH), lambda b, c: (b, 0, 0)),
            pl.BlockSpec((1, M_HEADS, M_HEAD_DIM, M_HEAD_DIM), lambda b, c: (b, 0, 0, 0)),
            pl.BlockSpec((1, M_HEADS, M_HEAD_DIM), lambda b, c: (b, 0, 0)),
            pl.BlockSpec((1, M_HEADS, LANES), lambda b, c: (b, 0, 0)),
            full((CONV_W, 2 * M_WIDTH)),
            full((1, 2 * M_WIDTH)),
            full((1, LANES)),
            full((2 * M_HEADS, LANES)),
            full((1, M_WIDTH)),
        ],
        out_specs=[
            pl.BlockSpec((1, ln, M_WIDTH), lambda b, c: (b, c, 0)),
            pl.BlockSpec((1, pad, 2 * M_WIDTH), lambda b, c: (b, 0, 0)),
            pl.BlockSpec((1, M_HEADS, M_HEAD_DIM, M_HEAD_DIM), lambda b, c: (b, 0, 0, 0)),
            pl.BlockSpec((1, M_HEADS, M_HEAD_DIM), lambda b, c: (b, 0, 0)),
            pl.BlockSpec((1, M_HEADS, LANES), lambda b, c: (b, 0, 0)),
        ],
        out_shape=[
            jax.ShapeDtypeStruct((bsz, t, M_WIDTH), BF16),
            jax.ShapeDtypeStruct((bsz, pad, 2 * M_WIDTH), F32),
            jax.ShapeDtypeStruct((bsz, M_HEADS, M_HEAD_DIM, M_HEAD_DIM), F32),
            jax.ShapeDtypeStruct((bsz, M_HEADS, M_HEAD_DIM), F32),
            jax.ShapeDtypeStruct((bsz, M_HEADS, LANES), F32),
        ],
        scratch_shapes=[
            pltpu.VMEM((ln + pad, 2 * M_WIDTH), F32),
            pltpu.VMEM((M_HEADS, M_HEAD_DIM, M_HEAD_DIM), F32),
            pltpu.VMEM((M_HEADS, M_HEAD_DIM), F32),
            pltpu.VMEM((M_HEADS, LANES), F32),
        ],
        compiler_params=_cparams(2),
        name="mlstm",
    )(z3, z3, z3, z3, gt, cprev, c0.astype(F32), n0.astype(F32), m0b,
      conv_w.astype(F32), conv_b.astype(F32)[None, :], gbrow, gbcol, mh_gain.astype(F32)[None, :])
    return hm, cnew[:, pad - (CONV_W - 1):, :], cout, nout, mout[:, :, 0]


def _mix_kernel(att_ref, hm_ref, ga_ref, gb_ref, wa_ref, wb_ref, mixed_ref):
    ya = _dot(att_ref[...], wa_ref[...])
    yb = _dot(hm_ref[...], wb_ref[...])
    mixed_ref[...] = (_sigmoid(ga_ref[...]) * ya + _sigmoid(gb_ref[...]) * yb).astype(BF16)


def _mix(att2, hm2, z2, wa, wb):
    n = att2.shape[0]
    tm = min(POST_TM, n)
    row = lambda w: pl.BlockSpec((tm, w), lambda i: (i, 0))
    full = lambda shape: pl.BlockSpec(shape, lambda i: (0,) * len(shape))
    return pl.pallas_call(
        _mix_kernel,
        grid=(n // tm,),
        in_specs=[
            row(ATT_WIDTH), row(M_WIDTH),
            pl.BlockSpec((tm, D_MODEL), lambda i: (i, OFF_GA // D_MODEL)),
            pl.BlockSpec((tm, D_MODEL), lambda i: (i, OFF_GB // D_MODEL)),
            full((ATT_WIDTH, D_MODEL)), full((M_WIDTH, D_MODEL)),
        ],
        out_specs=row(D_MODEL),
        out_shape=jax.ShapeDtypeStruct((n, D_MODEL), BF16),
        compiler_params=_cparams(1),
        name="mix",
    )(att2, hm2, z2, z2, wa, wb)


def _post_kernel(mixed_ref, x_ref, wo_ref, gf_ref, wr_ref, wrl_ref, br_ref, x1_ref, h2_ref, ri_ref):
    x1 = x_ref[...] + _dot(mixed_ref[...], wo_ref[...])
    x1_ref[...] = x1
    h2 = x1 * lax.rsqrt(jnp.mean(x1 * x1, axis=-1, keepdims=True) + EPS) * gf_ref[...]
    h2_ref[...] = h2
    h_hi = h2.astype(BF16)
    h_lo = (h2 - h_hi.astype(F32)).astype(BF16)
    lg = (_dot(h_hi, wr_ref[...]) + _dot(h_lo, wr_ref[...]) + _dot(h_hi, wrl_ref[...])) + br_ref[...]
    lane = lax.broadcasted_iota(I32, lg.shape, 1)
    glog = jnp.where(lane < N_GROUPS, lg, -jnp.inf)
    gmax = jnp.max(glog, axis=1, keepdims=True)
    gstar = jnp.min(jnp.where(glog == gmax, lane, LANES), axis=1, keepdims=True)
    pg = 1.0 / jnp.sum(jnp.exp(glog - gmax), axis=1, keepdims=True)
    lo = N_GROUPS + EXP_PER_GROUP * gstar
    elog = jnp.where((lane >= lo) & (lane < lo + EXP_PER_GROUP), lg, -jnp.inf)
    v0 = jnp.max(elog, axis=1, keepdims=True)
    i0 = jnp.min(jnp.where(elog == v0, lane, LANES), axis=1, keepdims=True)
    elog2 = jnp.where(lane == i0, -jnp.inf, elog)
    v1 = jnp.max(elog2, axis=1, keepdims=True)
    i1 = jnp.min(jnp.where(elog2 == v1, lane, LANES), axis=1, keepdims=True)
    e1 = jnp.exp(v1 - v0)
    w0 = pg / (1.0 + e1)
    w1 = pg * e1 / (1.0 + e1)
    out = jnp.where(lane == 0, (i0 - N_GROUPS).astype(F32), 0.0)
    out = jnp.where(lane == 1, (i1 - N_GROUPS).astype(F32), out)
    out = jnp.where(lane == 2, w0, out)
    out = jnp.where(lane == 3, w1, out)
    ri_ref[...] = out


def _post(mixed, x2, wo, g_ffn, wr, br):
    n = x2.shape[0]
    wr_hi = wr.astype(BF16)
    wr_lo = (wr - wr_hi.astype(F32)).astype(BF16)
    tm = min(POST_TM, n)
    row = lambda w: pl.BlockSpec((tm, w), lambda i: (i, 0))
    full = lambda shape: pl.BlockSpec(shape, lambda i: (0,) * len(shape))
    return pl.pallas_call(
        _post_kernel,
        grid=(n // tm,),
        in_specs=[
            row(D_MODEL), row(D_MODEL), full((D_MODEL, D_MODEL)),
            full((1, D_MODEL)), full((D_MODEL, LANES)), full((D_MODEL, LANES)), full((1, LANES)),
        ],
        out_specs=[row(D_MODEL), row(D_MODEL), row(LANES)],
        out_shape=[jax.ShapeDtypeStruct((n, D_MODEL), F32),
                   jax.ShapeDtypeStruct((n, D_MODEL), F32),
                   jax.ShapeDtypeStruct((n, LANES), F32)],
        compiler_params=_cparams(1),
        name="post",
    )(mixed, x2, wo, g_ffn, wr_hi, wr_lo, br)


def _row_gather_start(idx_ref, n_rows, src_hbm, dst_buf, sem, inline=False):
    def body(r, c):
        tok = idx_ref[0, 0, r]
        pltpu.make_async_copy(src_hbm.at[pl.ds(tok, 1)], dst_buf.at[pl.ds(r, 1)], sem).start()
        return c
    if inline:
        for r in range(n_rows):
            body(r, 0)
    else:
        lax.fori_loop(0, n_rows, body, 0, unroll=8)


def _row_gather_wait(n_rows, src_hbm, dst_buf, sem):
    pltpu.make_async_copy(src_hbm.at[pl.ds(0, n_rows)], dst_buf, sem).wait()


def _moe_kernel(te_ref, nu_ref, tok_ref, tokn_ref, h2_hbm, w1_ref, w3_ref, w2_ref, y_ref,
                xbuf, sem, w1b, w3b, w2b, *, tm):
    t = pl.program_id(0)
    n_used = nu_ref[0]
    slot = t % 2

    @pl.when((t < n_used) & ((t == 0) | (te_ref[t] != te_ref[jnp.maximum(t - 1, 0)])))
    def _():
        w1b[...] = w1_ref[0, 0].astype(BF16)
        w3b[...] = w3_ref[0, 0].astype(BF16)
        w2b[...] = w2_ref[0, 0].astype(BF16)

    @pl.when(t == 0)
    def _():
        _row_gather_start(tok_ref, tm, h2_hbm, xbuf.at[0], sem.at[0])

    @pl.when(t < n_used)
    def _():
        _row_gather_wait(tm, h2_hbm, xbuf.at[slot], sem.at[slot])
        _row_gather_start(tokn_ref, tm, h2_hbm, xbuf.at[1 - slot], sem.at[1 - slot], inline=True)
        xb = xbuf[slot].astype(BF16)
        u = _dot(xb, w1b[...])
        a = (u * _sigmoid(u)) * _dot(xb, w3b[...])
        y_ref[...] = _dot(a.astype(BF16), w2b[...])

        @pl.when(t == pl.num_programs(0) - 1)
        def _():
            _row_gather_wait(tm, h2_hbm, xbuf.at[1 - slot], sem.at[1 - slot])

    @pl.when(t == n_used)
    def _():
        _row_gather_wait(tm, h2_hbm, xbuf.at[slot], sem.at[slot])

    @pl.when(t >= n_used)
    def _():
        y_ref[...] = jnp.zeros(y_ref.shape, F32)


def _moe(h2, tile_expert, n_used, row_token, w1, w3, w2, tm):
    n_tiles = tile_expert.shape[0]
    tok3 = row_token.reshape(n_tiles, 1, tm)
    grid_spec = pltpu.PrefetchScalarGridSpec(
        num_scalar_prefetch=2,
        grid=(n_tiles,),
        in_specs=[
            pl.BlockSpec((1, 1, tm), lambda t, te, nu: (t, 0, 0), memory_space=pltpu.SMEM),
            pl.BlockSpec((1, 1, tm), lambda t, te, nu: (jnp.minimum(t + 1, n_tiles - 1), 0, 0),
                         memory_space=pltpu.SMEM),
            pl.BlockSpec(memory_space=pl.ANY),
            pl.BlockSpec((1, 1, D_MODEL, D_EXPERT), lambda t, te, nu: (0, te[t], 0, 0)),
            pl.BlockSpec((1, 1, D_MODEL, D_EXPERT), lambda t, te, nu: (0, te[t], 0, 0)),
            pl.BlockSpec((1, 1, D_EXPERT, D_MODEL), lambda t, te, nu: (0, te[t], 0, 0)),
        ],
        out_specs=pl.BlockSpec((tm, D_MODEL), lambda t, te, nu: (t, 0)),
        scratch_shapes=[pltpu.VMEM((2, tm, D_MODEL), F32), pltpu.SemaphoreType.DMA((2,)),
                        pltpu.VMEM((D_MODEL, D_EXPERT), BF16), pltpu.VMEM((D_MODEL, D_EXPERT), BF16),
                        pltpu.VMEM((D_EXPERT, D_MODEL), BF16)],
    )
    return pl.pallas_call(
        functools.partial(_moe_kernel, tm=tm),
        grid_spec=grid_spec,
        out_shape=jax.ShapeDtypeStruct((n_tiles * tm, D_MODEL), F32),
        compiler_params=_cparams(1),
        name="moe",
    )(tile_expert, n_used, tok3, tok3, h2, w1, w3, w2)


def _route(rinfo, n_tiles, tm):
    n = rinfo.shape[0]
    eid = rinfo[:, 0:2].astype(I32).reshape(-1)
    order = jnp.argsort(eid, stable=True).astype(I32)
    inv = jnp.argsort(order).astype(I32)
    onehot = eid[:, None] == jnp.arange(N_EXPERTS, dtype=I32)[None, :]
    counts = jnp.sum(onehot.astype(I32), axis=0)
    tiles_e = (counts + tm - 1) // tm
    tile_end = jnp.cumsum(tiles_e)
    row_start = (tile_end - tiles_e) * tm
    grp_start = jnp.cumsum(counts) - counts
    shift = row_start - grp_start
    pos = inv + jnp.sum(jnp.where(onehot, shift[None, :], 0), axis=1)
    n_used = tile_end[-1]
    tile_ids = jnp.arange(n_tiles, dtype=I32)
    te_raw = jnp.sum((tile_end[None, :] <= tile_ids[:, None]).astype(I32), axis=1)
    last_e = jnp.sum((tile_end <= n_used - 1).astype(I32))
    tile_expert = jnp.minimum(te_raw, last_e)
    te_c = jnp.minimum(te_raw, N_EXPERTS - 1)
    src = jnp.arange(n_tiles * tm, dtype=I32) - jnp.repeat(shift[te_c], tm)
    lo = jnp.repeat(grp_start[te_c], tm)
    hi = lo + jnp.repeat(counts[te_c], tm)
    valid = (src >= lo) & (src < hi)
    row_token = jnp.where(valid, order[jnp.clip(src, 0, 2 * n - 1)] // 2, 0)
    return tile_expert, n_used.reshape(1).astype(I32), row_token, pos.reshape(n, 2)


def _final_kernel(pos_ref, posn_ref, y_hbm, x1_ref, ri_ref, pe_ref, gp_ref, wpg_ref, wp_ref, gfin_ref,
                  o_ref, ybuf, sem):
    t = pl.program_id(0)
    nt = pl.num_programs(0)
    slot = t % 2
    tm = TOK_TM

    @pl.when(t == 0)
    def _():
        _row_gather_start(pos_ref, 2 * tm, y_hbm, ybuf.at[0], sem.at[0])

    _row_gather_wait(2 * tm, y_hbm, ybuf.at[slot], sem.at[slot])
    _row_gather_start(posn_ref, 2 * tm, y_hbm, ybuf.at[1 - slot], sem.at[1 - slot], inline=True)
    ri = ri_ref[...]
    moe = ri[:, 2:3] * ybuf[slot, 0:tm] + ri[:, 3:4] * ybuf[slot, tm:2 * tm]
    x2 = x1_ref[...] + moe
    hp = x2 * lax.rsqrt(jnp.mean(x2 * x2, axis=-1, keepdims=True) + EPS) * gp_ref[...]
    gate = _sigmoid(_dot(hp.astype(BF16), wpg_ref[...]))
    x3 = x2 + gate * _dot(pe_ref[...].astype(BF16), wp_ref[...])
    o_ref[...] = x3 * lax.rsqrt(jnp.mean(x3 * x3, axis=-1, keepdims=True) + EPS) * gfin_ref[...]

    @pl.when(t == nt - 1)
    def _():
        _row_gather_wait(2 * tm, y_hbm, ybuf.at[1 - slot], sem.at[1 - slot])


def _final(x1, rinfo, pos, y_sorted, pe2, g_ple, wpg, wp, g_final):
    n = x1.shape[0]
    tm = TOK_TM
    nt = n // tm
    pos3 = pos.reshape(nt, tm, 2).transpose(0, 2, 1).reshape(nt, 1, 2 * tm)
    row = lambda w: pl.BlockSpec((tm, w), lambda i: (i, 0))
    full = lambda shape: pl.BlockSpec(shape, lambda i: (0,) * len(shape))
    return pl.pallas_call(
        _final_kernel,
        grid=(nt,),
        in_specs=[
            pl.BlockSpec((1, 1, 2 * tm), lambda i: (i, 0, 0), memory_space=pltpu.SMEM),
            pl.BlockSpec((1, 1, 2 * tm), lambda i: (jnp.minimum(i + 1, nt - 1), 0, 0),
                         memory_space=pltpu.SMEM),
            pl.BlockSpec(memory_space=pl.ANY),
            row(D_MODEL), row(LANES), row(PLE_DIM),
            full((1, D_MODEL)), full((D_MODEL, D_MODEL)), full((PLE_DIM, D_MODEL)), full((1, D_MODEL)),
        ],
        out_specs=row(D_MODEL),
        out_shape=jax.ShapeDtypeStruct((n, D_MODEL), F32),
        scratch_shapes=[pltpu.VMEM((2, 2 * tm, D_MODEL), F32), pltpu.SemaphoreType.DMA((2,))],
        compiler_params=_cparams(1),
        name="final",
    )(pos3, pos3, y_sorted, x1, rinfo, pe2, g_ple, wpg, wp, g_final)


def _pack_w_in(w):
    offs = np.concatenate([[0], np.cumsum(PROJ_SIZES)])
    qa, ka, va, qi, ki, wi, qm, km, vm, ig, fg, og, ga, gb = [w[:, offs[k]:offs[k + 1]] for k in range(14)]
    small = jnp.concatenate([ki, wi, ig, fg], axis=1)
    small = jnp.pad(small, ((0, 0), (0, SM_WIDTH - small.shape[1])))
    return jnp.concatenate([qa, qi, qm, km, vm, og, ga, gb, ka, va, small], axis=1).astype(BF16)


def _prep_weights(g_mix, w_in, w_att_out, w_mlstm_out, w_out, g_ffn, router_gw, router_gb, router_ew,
                  router_eb, w1, w3, w2, g_ple, w_ple, w_ple_gate, g_final):
    wr = jnp.concatenate([router_gw, router_ew], axis=1).astype(F32)
    wr = jnp.pad(wr, ((0, 0), (0, LANES - wr.shape[1])))
    br = jnp.concatenate([router_gb, router_eb]).astype(F32)
    br = jnp.pad(br, (0, LANES - br.shape[0]))[None, :]
    return dict(
        g_mix=g_mix.astype(F32)[None, :], w_pack=_pack_w_in(w_in),
        wa=w_att_out.astype(BF16), wb=w_mlstm_out.astype(BF16), wo=w_out.astype(BF16),
        g_ffn=g_ffn.astype(F32)[None, :], wr=wr, br=br,
        w1=w1.reshape((1,) + w1.shape[-3:]), w3=w3.reshape((1,) + w3.shape[-3:]),
        w2=w2.reshape((1,) + w2.shape[-3:]),
        g_ple=g_ple.astype(F32)[None, :], wp=w_ple.astype(BF16), wpg=w_ple_gate.astype(BF16),
        g_final=g_final.astype(F32)[None, :])


def _layer(x, pe, past, conv_prev, c0, n0, m0, wts, conv_w, conv_b, b_igate, b_fgate, mh_gain, qb, ln):
    bsz, t, _ = x.shape
    n = bsz * t
    x2 = x.reshape(n, D_MODEL)
    z, kvb, k_new, v_new, ki_new = _inproj(x2, wts["g_mix"], wts["w_pack"])
    z3 = z.reshape(bsz, t, D_PACK)
    kv3 = kvb.reshape(bsz, t, KV_PACK)
    att = _dsa(z3, kv3, past, qb, min(DSA_SB, qb))
    hm, conv_new, c_new, n_new, m_new = _mlstm(z3, conv_prev, c0, n0, m0, conv_w, conv_b,
                                               b_igate, b_fgate, mh_gain, ln)
    mixed = _mix(att.reshape(n, ATT_WIDTH), hm.reshape(n, M_WIDTH), z, wts["wa"], wts["wb"])
    x1, h2, rinfo = _post(mixed, x2, wts["wo"], wts["g_ffn"], wts["wr"], wts["br"])
    tm = MOE_TM if 2 * n >= 4 * MOE_TM * N_EXPERTS else MOE_TM_SMALL
    n_tiles = (2 * n + N_EXPERTS * (tm - 1) + tm - 1) // tm
    tile_expert, n_used, row_token, pos = _route(rinfo, n_tiles, tm)
    y_sorted = _moe(h2, tile_expert, n_used, row_token, wts["w1"], wts["w3"], wts["w2"], tm)
    y = _final(x1, rinfo, pos, y_sorted, pe.reshape(n, PLE_DIM).astype(F32),
               wts["g_ple"], wts["wpg"], wts["wp"], wts["g_final"])
    k_new = k_new.reshape(bsz, t, N_KV_HEADS, HEAD_DIM)
    v_new = v_new.reshape(bsz, t, N_KV_HEADS, HEAD_DIM)
    ki_new = ki_new.reshape(bsz, t, IDX_DIM)
    return y.reshape(bsz, t, D_MODEL), (k_new, v_new, ki_new, conv_new, c_new, n_new, m_new)


def kernel(x_prompt, x_sample, cache_k, cache_v, cache_kidx, state_conv, state_C, state_n, state_m,
           p_prompt, p_sample, g_mix, w_in, conv_w, conv_b, b_igate, b_fgate, mh_gain,
           w_att_out, w_mlstm_out, w_out, g_ffn, router_gw, router_gb, router_ew, router_eb,
           w1, w3, w2, g_ple, w_ple, w_ple_gate, g_final):
    assert g_mix.shape[0] == 1, "single-layer step"
    bp, tp, _ = x_prompt.shape
    bs, ts, _ = x_sample.shape
    sdt = state_C.dtype
    wts = _prep_weights(g_mix[0], w_in[0], w_att_out[0], w_mlstm_out[0], w_out[0], g_ffn[0],
                        router_gw[0], router_gb[0], router_ew[0], router_eb[0], w1, w3, w2,
                        g_ple[0], w_ple[0], w_ple_gate[0], g_final)
    mix = (conv_w[0], conv_b[0], b_igate[0], b_fgate[0], mh_gain[0])
    yp, st_p = _layer(
        x_prompt, p_prompt[0], None,
        jnp.zeros((bp, CONV_W - 1, 2 * M_WIDTH), F32),
        jnp.zeros((bp, M_HEADS, M_HEAD_DIM, M_HEAD_DIM), F32),
        jnp.zeros((bp, M_HEADS, M_HEAD_DIM), F32),
        jnp.zeros((bp, M_HEADS), F32),
        wts, *mix, qb=min(DSA_QB, tp), ln=min(256, tp))
    plen = cache_k.shape[2]
    past = (cache_k[0], cache_v[0], cache_kidx[0])
    ys, st_s = _layer(
        x_sample, p_sample[0], past, state_conv[0], state_C[0], state_n[0], state_m[0],
        wts, *mix, qb=ts, ln=ts)
    outs_p = [s[None] for s in st_p]
    outs_s = [s[None] for s in st_s]
    for lst in (outs_p, outs_s):
        for k in (4, 5, 6):
            lst[k] = lst[k].astype(sdt)
    return (yp, ys, *outs_p, *outs_s)
```

```python
import functools

import numpy as np
import jax
import jax.numpy as jnp
from jax import lax
from jax.experimental import pallas as pl
from jax.experimental.pallas import tpu as pltpu

F32 = jnp.float32
BF16 = jnp.bfloat16
I32 = jnp.int32

D_MODEL = 2048
CHUNK = 64
CHUNK_SHIFT = 6
assert 1 << CHUNK_SHIFT == CHUNK
N_HEADS = 8
N_KV_HEADS = 4
HEAD_DIM = 128
ATT_WIDTH = N_HEADS * HEAD_DIM
KV_WIDTH = N_KV_HEADS * HEAD_DIM
IDX_HEADS = 16
IDX_DIM = 64
TOPK_MAX = 256
M_HEADS = 4
M_HEAD_DIM = 256
M_WIDTH = M_HEADS * M_HEAD_DIM
CONV_W = 4
N_GROUPS = 4
EXP_PER_GROUP = 8
N_EXPERTS = N_GROUPS * EXP_PER_GROUP
D_EXPERT = 512
PLE_DIM = 256
EPS = 1e-6
PROJ_SIZES = (ATT_WIDTH, KV_WIDTH, KV_WIDTH, IDX_HEADS * IDX_DIM, IDX_DIM, IDX_HEADS,
              M_WIDTH, M_WIDTH, M_WIDTH, M_HEADS, M_HEADS, M_WIDTH, D_MODEL, D_MODEL)

LANES = 128
SUBLANES = 8
VMEM_LIMIT = 56 * 1024 * 1024

OFF_QA, OFF_QI, OFF_QM, OFF_KM, OFF_VM, OFF_OG = 0, 1024, 2048, 3072, 4096, 5120
OFF_GA, OFF_GB, OFF_KA, OFF_VA, OFF_SM = 6144, 8192, 10240, 10752, 11264
SM_WIDTH = 512
D_PACK = OFF_SM + SM_WIDTH
SM_KI, SM_WI, SM_IG, SM_FG = 0, 64, 80, 84
PROJ_TN = 512
KV_BLK0 = OFF_KA // PROJ_TN
KV_PACK = D_PACK - OFF_KA

LOG2E = 1.4426950408889634
INT_MIN = np.int32(-2 ** 31)
NEG_BIG = -1e30

DSA_KT = 512
DSA_QB = 128
DSA_SB = 128
MOE_TM = 512
MOE_TM_SMALL = 128
MOE_SLOTS = 3
TOK_TM = 256
POST_TM = 512


def _cparams(n_axes):
    return pltpu.CompilerParams(dimension_semantics=("arbitrary",) * n_axes,
                                vmem_limit_bytes=VMEM_LIMIT)


def _dot(a, b):
    return jnp.dot(a, b, preferred_element_type=F32)


def _dot_nt(a, b):
    return lax.dot_general(a, b, (((1,), (1,)), ((), ())), preferred_element_type=F32)


def _dot_tn(a, b):
    return lax.dot_general(a, b, (((0,), (0,)), ((), ())), preferred_element_type=F32)


def _tile_rows(t, size):
    start = t * size
    return pl.ds(start if isinstance(start, int) else pl.multiple_of(start, size), size)


def _head_rows(t, size, g):
    start = t * (size * N_KV_HEADS)
    if not isinstance(start, int):
        start = pl.multiple_of(start, size * N_KV_HEADS)
    return pl.ds(start + g, size, stride=N_KV_HEADS)


def _sigmoid(x):
    return 1.0 / (1.0 + jnp.exp(-x))


def _split3(x):
    hi = x.astype(BF16)
    r1 = x - hi.astype(F32)
    mid = r1.astype(BF16)
    lo = (r1 - mid.astype(F32)).astype(BF16)
    return hi, mid, lo


def _inproj_kernel(x_ref, g_ref, w_ref, z_ref, kv_ref, k_ref, v_ref, ki_ref, h_ref):
    j = pl.program_id(1)

    @pl.when(j == 0)
    def _():
        x = x_ref[...]
        r = lax.rsqrt(jnp.mean(x * x, axis=-1, keepdims=True) + EPS)
        h_ref[...] = (x * r * g_ref[...]).astype(BF16)

    acc = _dot(h_ref[...], w_ref[...])
    z_ref[...] = acc

    @pl.when(j >= KV_BLK0)
    def _():
        kv_ref[...] = acc.astype(BF16)

    tm = acc.shape[0]

    def store_heads(ref):
        for g in range(N_KV_HEADS):
            ref[pl.ds(g, tm, stride=N_KV_HEADS), :] = acc[:, g * HEAD_DIM:(g + 1) * HEAD_DIM]

    @pl.when(j == KV_BLK0)
    def _():
        store_heads(k_ref)

    @pl.when(j == KV_BLK0 + 1)
    def _():
        store_heads(v_ref)

    @pl.when(j == KV_BLK0 + 2)
    def _():
        ki_ref[...] = acc[:, SM_KI:SM_KI + IDX_DIM]


def _inproj(x2d, g, w_pack):
    n = x2d.shape[0]
    tm = min(1024, n)
    grid = (n // tm, D_PACK // PROJ_TN)
    return pl.pallas_call(
        _inproj_kernel,
        grid=grid,
        in_specs=[
            pl.BlockSpec((tm, D_MODEL), lambda i, j: (i, 0)),
            pl.BlockSpec((1, D_MODEL), lambda i, j: (0, 0)),
            pl.BlockSpec((D_MODEL, PROJ_TN), lambda i, j: (0, j)),
        ],
        out_specs=[
            pl.BlockSpec((tm, PROJ_TN), lambda i, j: (i, j)),
            pl.BlockSpec((tm, PROJ_TN), lambda i, j: (i, jnp.maximum(j - KV_BLK0, 0))),
            pl.BlockSpec((tm * N_KV_HEADS, HEAD_DIM), lambda i, j: (i, 0)),
            pl.BlockSpec((tm * N_KV_HEADS, HEAD_DIM), lambda i, j: (i, 0)),
            pl.BlockSpec((tm, IDX_DIM), lambda i, j: (i, 0)),
        ],
        out_shape=[jax.ShapeDtypeStruct((n, D_PACK), F32),
                   jax.ShapeDtypeStruct((n, KV_PACK), BF16),
                   jax.ShapeDtypeStruct((n * N_KV_HEADS, HEAD_DIM), F32),
                   jax.ShapeDtypeStruct((n * N_KV_HEADS, HEAD_DIM), F32),
                   jax.ShapeDtypeStruct((n, IDX_DIM), F32)],
        scratch_shapes=[pltpu.VMEM((tm, D_MODEL), BF16)],
        compiler_params=_cparams(2),
        name="inproj",
    )(x2d, g, w_pack)


def _dsa_kernel(*refs, has_past, qblk, sb, n_past_tiles, n_new_tiles, t_valid, past_len, topk):
    n_in = 9 if has_past else 6
    if has_past:
        qa_ref, qi_ref, sm_ref, kip_ref, kp_ref, vp_ref, kin_ref, kn_ref, vn_ref = refs[:n_in]
    else:
        qa_ref, qi_ref, sm_ref, kin_ref, kn_ref, vn_ref = refs[:n_in]
    o_ref, keys_ref, kT_ref, d_ref, s_ref = refs[n_in:n_in + 5]
    state = refs[n_in + 5:]
    mx_refs, acc_refs = state[0::2], state[1::2]
    kt = DSA_KT
    nsub = qblk // sb
    i = pl.program_id(1)
    q0 = past_len + i * qblk
    j_end = ((q0 + qblk - 1) // CHUNK + 1) * CHUNK - past_len
    nk_new = jnp.minimum((j_end + kt - 1) // kt, n_new_tiles)
    n_tiles = n_past_tiles + nk_new
    lane_sb = lax.broadcasted_iota(I32, (sb, kt), 1)

    for sub in range(nsub):
        r0 = sub * sb
        qchunk = (q0 + r0 + lax.broadcasted_iota(I32, (sb, 1), 0)) >> CHUNK_SHIFT
        wsc = sm_ref[0, r0:r0 + sb, SM_WI:SM_WI + IDX_HEADS] * (IDX_HEADS ** -0.5 * IDX_DIM ** -0.5)
        wcols = [wsc[:, h:h + 1] for h in range(IDX_HEADS)]
        qi_blk = qi_ref[0, r0:r0 + sb, :]
        qi_all = jnp.concatenate([qi_blk[:, h * IDX_DIM:(h + 1) * IDX_DIM] for h in range(IDX_HEADS)],
                                 axis=0).astype(BF16)

        def score_tile(ki_t, kpos0, jvalid0, col, r0=r0, qchunk=qchunk, wcols=wcols, qi_all=qi_all):
            d_ref[...] = _dot_nt(qi_all, ki_t)
            acc = wcols[0] * jnp.maximum(d_ref[0:sb], 0.0)
            for h in range(1, IDX_HEADS):
                acc = acc + wcols[h] * jnp.maximum(d_ref[h * sb:(h + 1) * sb], 0.0)
            bits = lax.bitcast_convert_type(acc + 0.0, I32)
            key = bits ^ ((bits >> 31) & np.int32(0x7FFFFFFF))
            adm = (((kpos0 + lane_sb) >> CHUNK_SHIFT) <= qchunk) & (jvalid0 + lane_sb < t_valid)
            keys_ref[col, r0:r0 + sb, :] = jnp.where(adm, key, INT_MIN)

        if has_past:
            def p1_past(t, c, score_tile=score_tile):
                ki_t = kip_ref[0, _tile_rows(t, kt), :].astype(BF16)
                score_tile(ki_t, t * kt, -(2 ** 30), t)
                return c
            lax.fori_loop(0, n_past_tiles, p1_past, 0)

        def p1_new(t, c, score_tile=score_tile):
            ki_t = kin_ref[0, _tile_rows(t, kt), 0:IDX_DIM]
            score_tile(ki_t, past_len + t * kt, t * kt, n_past_tiles + t)
            return c
        lax.fori_loop(0, nk_new, p1_new, 0)

    qb = qblk
    lane = lax.broadcasted_iota(I32, (qb, kt), 1)

    def count(fn):
        def body(t, part):
            kall = keys_ref[t]
            for s in range(kt // LANES):
                ks = kall[:, s * LANES:(s + 1) * LANES]
                part = part + jnp.where(fn(ks, t, s), 1.0, 0.0)
            return part
        part = lax.fori_loop(0, n_tiles, body, jnp.zeros((qb, LANES), F32))
        return jnp.sum(part, axis=1, keepdims=True)

    kf = float(topk)

    bits_per_check = 4

    def search(count_ge, shape):
        def sgroup(carry):
            grp, tu, done, _ = carry
            for b in range(bits_per_check):
                shift = jnp.asarray(31 - b, I32) - grp * bits_per_check
                cand = tu | lax.shift_left(np.int32(1), shift)
                cnt = count_ge(cand ^ INT_MIN)
                tu = jnp.where((cnt >= kf) & (done == 0.0), cand, tu)
                done = jnp.where(cnt == kf, 1.0, done)
            return grp + 1, tu, done, jnp.min(done)

        def scond(carry):
            grp, _, _, all_done = carry
            return (grp < 32 // bits_per_check) & (all_done == 0.0)

        _, tu, _, _ = lax.while_loop(
            scond, sgroup, (jnp.int32(0), jnp.zeros(shape, I32), jnp.zeros(shape, F32), jnp.float32(0.0)))
        return tu ^ INT_MIN

    if qb % LANES == 0:
        def xpose(t, c):
            kT_ref[t] = lax.bitcast_convert_type(lax.bitcast_convert_type(keys_ref[t], F32).T, I32)
            return c
        lax.fori_loop(0, n_tiles, xpose, 0)

        acc_rows = 4 * SUBLANES

        def count_t(fn):
            def body(t, part):
                c = jnp.where(fn(kT_ref[t]), 1.0, 0.0)
                return part + jnp.sum(c.reshape(kt // acc_rows, acc_rows, qb), axis=0)
            part = lax.fori_loop(0, n_tiles, body, jnp.zeros((acc_rows, qb), F32))
            return jnp.sum(part, axis=0, keepdims=True)

        thr_t = search(lambda cs: count_t(lambda k: k >= cs), (1, qb))
        cnt_ge_t = count_t(lambda k: k >= thr_t)
        any_tie = jnp.max(jnp.where((cnt_ge_t > kf) & (thr_t > INT_MIN), 1.0, 0.0)) > 0.0
        eye = lax.broadcasted_iota(I32, (qb, qb), 0) == lax.broadcasted_iota(I32, (qb, qb), 1)
        hi = jnp.sum(jnp.where(eye, (thr_t >> 16).astype(F32), 0.0), axis=1, keepdims=True)
        lo = jnp.sum(jnp.where(eye, (thr_t & np.int32(0xFFFF)).astype(F32), 0.0), axis=1, keepdims=True)
        thr = lax.shift_left(hi.astype(I32), np.int32(16)) | lo.astype(I32)
    else:
        thr = search(lambda cs: count(lambda k, t, s: k >= cs), (qb, 1))
        cnt_ge0 = count(lambda k, t, s: k >= thr)
        any_tie = jnp.max(jnp.where((cnt_ge0 > kf) & (thr > INT_MIN), 1.0, 0.0)) > 0.0

    @pl.when(any_tie)
    def _():
        cnt_ge = count(lambda k, t, s: k >= thr)
        cnt_gt = count(lambda k, t, s: k > thr)
        tie = (cnt_ge > kf) & (thr > INT_MIN)
        need = kf - cnt_gt
        lane1 = lax.broadcasted_iota(I32, (qb, LANES), 1)

        def jbody(it, a):
            cand = a | lax.shift_left(np.int32(1), jnp.asarray(15 - it, I32))
            cnt = count(lambda k, t, s: (k == thr) & (t * kt + s * LANES + lane1 < cand))
            return jnp.where(cnt < need, cand, a)
        a = lax.fori_loop(0, 16, jbody, jnp.zeros((qb, 1), I32))

        def drop(t, c):
            k = keys_ref[t]
            keys_ref[t] = jnp.where(tie & (k == thr) & (t * kt + lane > a), INT_MIN, k)
            return c
        lax.fori_loop(0, n_tiles, drop, 0)

    thr_eff = jnp.maximum(thr, INT_MIN + 1)

    def hs(g):
        return slice(g * HEAD_DIM, (g + 1) * HEAD_DIM)

    ones_blk = jnp.ones((kt, HEAD_DIM), BF16)

    for sub in range(nsub):
        r0 = sub * sb
        thr_sub = thr_eff[r0:r0 + sb]
        q = qa_ref[0, r0:r0 + sb, :] * (HEAD_DIM ** -0.5 * LOG2E)
        q2 = []
        for g in range(N_KV_HEADS):
            a = q[:, (2 * g) * HEAD_DIM:(2 * g + 1) * HEAD_DIM]
            b = q[:, (2 * g + 1) * HEAD_DIM:(2 * g + 2) * HEAD_DIM]
            q2.append(jnp.concatenate([a, b], axis=0).astype(BF16))
        for g in range(N_KV_HEADS):
            mx_refs[g][...] = jnp.full(mx_refs[g].shape, -jnp.inf, F32)
            acc_refs[g][...] = jnp.zeros(acc_refs[g].shape, F32)

        def logits(col, k_fn, r0=r0, thr_sub=thr_sub, q2=q2):
            bias = jnp.where(keys_ref[col, r0:r0 + sb, :] >= thr_sub, 0.0, NEG_BIG)
            bias2 = jnp.concatenate([bias, bias], axis=0)
            for g in range(N_KV_HEADS):
                s = _dot_nt(q2[g], k_fn(g)) + bias2
                s_ref[g, col] = s
                mx = mx_refs[g][...]
                for c in range(kt // LANES):
                    mx = jnp.maximum(mx, s[:, c * LANES:(c + 1) * LANES])
                mx_refs[g][...] = mx

        def weighted(col, v_fn, m_rows):
            for g in range(N_KV_HEADS):
                p = jnp.exp2(s_ref[g, col] - m_rows[g]).astype(BF16)
                v_aug = jnp.concatenate([v_fn(g), ones_blk], axis=1)
                acc_refs[g][...] = acc_refs[g][...] + _dot(p, v_aug)

        if has_past:
            def pa_past(t, c, logits=logits):
                rows = _tile_rows(t, kt)
                logits(t, lambda g: kp_ref[0, _head_rows(t, kt, g), :].astype(BF16))
                return c
            lax.fori_loop(0, n_past_tiles, pa_past, 0)

        def pa_new(t, c, logits=logits):
            rows = _tile_rows(t, kt)
            logits(n_past_tiles + t, lambda g: kn_ref[0, rows, hs(g)])
            return c
        lax.fori_loop(0, nk_new, pa_new, 0)

        m_rows = [jnp.max(mx_refs[g][...], axis=1, keepdims=True) for g in range(N_KV_HEADS)]

        if has_past:
            def pb_past(t, c, m_rows=m_rows, weighted=weighted):
                rows = _tile_rows(t, kt)
                weighted(t, lambda g: vp_ref[0, _head_rows(t, kt, g), :].astype(BF16), m_rows)
                return c
            lax.fori_loop(0, n_past_tiles, pb_past, 0)

        def pb_new(t, c, m_rows=m_rows, weighted=weighted):
            rows = _tile_rows(t, kt)
            weighted(n_past_tiles + t, lambda g: vn_ref[0, rows, hs(g)], m_rows)
            return c
        lax.fori_loop(0, nk_new, pb_new, 0)

        for g in range(N_KV_HEADS):
            acc = acc_refs[g][...]
            o = acc[:, 0:HEAD_DIM] / acc[:, HEAD_DIM:2 * HEAD_DIM]
            o_ref[0, r0:r0 + sb, (2 * g) * HEAD_DIM:(2 * g + 1) * HEAD_DIM] = o[0:sb].astype(BF16)
            o_ref[0, r0:r0 + sb, (2 * g + 1) * HEAD_DIM:(2 * g + 2) * HEAD_DIM] = o[sb:2 * sb].astype(BF16)


def _dsa(z3, kv3, past, qb, sb):
    bsz, t, _ = z3.shape
    kt = DSA_KT
    nb = t // qb
    nsub = qb // sb
    has_past = past is not None
    past_len = past[0].shape[1] if has_past else 0
    topk = min(TOPK_MAX, (past_len + t) // 4)
    t_pad = -(-t // kt) * kt
    if t_pad != t:
        kv3 = jnp.pad(kv3, ((0, 0), (0, t_pad - t), (0, 0)))
    n_new_tiles = t_pad // kt
    n_past_tiles = past_len // kt
    qi_width = IDX_HEADS * IDX_DIM
    in_specs = [
        pl.BlockSpec((1, qb, ATT_WIDTH), lambda b, i: (b, i, OFF_QA // ATT_WIDTH)),
        pl.BlockSpec((1, qb, qi_width), lambda b, i: (b, i, OFF_QI // qi_width)),
        pl.BlockSpec((1, qb, LANES), lambda b, i: (b, i, OFF_SM // LANES)),
    ]
    args = [z3, z3, z3]
    if has_past:
        pk, pv, pki = past
        pk = pk.reshape(bsz, past_len * N_KV_HEADS, HEAD_DIM)
        pv = pv.reshape(bsz, past_len * N_KV_HEADS, HEAD_DIM)
        in_specs += [
            pl.BlockSpec((1, past_len, IDX_DIM), lambda b, i: (b, 0, 0)),
            pl.BlockSpec((1, past_len * N_KV_HEADS, HEAD_DIM), lambda b, i: (b, 0, 0)),
            pl.BlockSpec((1, past_len * N_KV_HEADS, HEAD_DIM), lambda b, i: (b, 0, 0)),
        ]
        args += [pki, pk, pv]
    in_specs += [
        pl.BlockSpec((1, t_pad, LANES), lambda b, i: (b, 0, 2 * KV_WIDTH // LANES)),
        pl.BlockSpec((1, t_pad, KV_WIDTH), lambda b, i: (b, 0, 0)),
        pl.BlockSpec((1, t_pad, KV_WIDTH), lambda b, i: (b, 0, 1)),
    ]
    args += [kv3, kv3, kv3]
    kern = functools.partial(_dsa_kernel, has_past=has_past, qblk=qb, sb=sb, n_past_tiles=n_past_tiles,
                             n_new_tiles=n_new_tiles, t_valid=t, past_len=past_len, topk=topk)
    return pl.pallas_call(
        kern,
        grid=(bsz, nb),
        in_specs=in_specs,
        out_specs=pl.BlockSpec((1, qb, ATT_WIDTH), lambda b, i: (b, i, 0)),
        out_shape=jax.ShapeDtypeStruct((bsz, t, ATT_WIDTH), BF16),
        scratch_shapes=[pltpu.VMEM((n_past_tiles + n_new_tiles, qb, kt), I32),
                        pltpu.VMEM((n_past_tiles + n_new_tiles, kt, qb) if qb % LANES == 0
                                   else (1, SUBLANES, LANES), I32),
                        pltpu.VMEM((IDX_HEADS * sb, kt), F32),
                        pltpu.VMEM((N_KV_HEADS, n_past_tiles + n_new_tiles, 2 * sb, kt), F32)] + [
            pltpu.VMEM((2 * sb, LANES), F32), pltpu.VMEM((2 * sb, 2 * HEAD_DIM), F32)] * N_KV_HEADS,
        compiler_params=_cparams(2),
        name="dsa_past" if has_past else "dsa",
    )(*args)


def _mlstm_kernel(qk_ref, vm_ref, og_ref, sm_ref, gt_ref, cprev_ref, c0_ref, n0_ref, m0_ref,
                  cw_ref, cb_ref, gbrow_ref, gbcol_ref, gain_ref,
                  hm_ref, cnew_ref, cout_ref, nout_ref, mout_ref,
                  ubuf, c_s, n_s, m_s, *, ln):
    c = pl.program_id(1)
    nc = pl.num_programs(1)
    pad = SUBLANES

    @pl.when(c == 0)
    def _():
        ubuf[0:pad] = cprev_ref[0]
        c_s[...] = c0_ref[0]
        n_s[...] = n0_ref[0]
        m_s[...] = m0_ref[0]

    @pl.when(c > 0)
    def _():
        ubuf[0:pad] = ubuf[ln:ln + pad]

    ubuf[pad:pad + ln] = qk_ref[0]
    y = cb_ref[...] + ubuf[pad - 3:pad - 3 + ln] * cw_ref[0:1]
    for j in range(1, CONV_W):
        y = y + ubuf[pad - 3 + j:pad - 3 + j + ln] * cw_ref[j:j + 1]
    qkc = y * _sigmoid(y)
    cnew_ref[0] = ubuf[ln:ln + pad]

    pre_c = sm_ref[0] + gbrow_ref[...]
    pre_r = gt_ref[0, 0] + gbcol_ref[...][:, 0:1]

    def log_sigmoid(x):
        return jnp.minimum(x, 0.0) - jnp.log(1.0 + jnp.exp(-jnp.abs(x)))

    lf_c = log_sigmoid(pre_c)
    lf_r = log_sigmoid(pre_r)
    ri = lax.broadcasted_iota(I32, (ln, ln), 0)
    ci = lax.broadcasted_iota(I32, (ln, ln), 1)
    causal = ci <= ri
    tril = jnp.where(causal, 1.0, 0.0).astype(BF16)
    triu = jnp.where(ri <= ci, 1.0, 0.0).astype(BF16)
    b_c = sum(_dot(tril, p) for p in _split3(lf_c))
    b_r = sum(_dot(p, triu) for p in _split3(lf_r))

    vm = vm_ref[0]
    og = og_ref[0]
    for h in range(M_HEADS):
        hs = slice(h * M_HEAD_DIM, (h + 1) * M_HEAD_DIM)
        qf = qkc[:, hs]
        kf = qkc[:, M_WIDTH + h * M_HEAD_DIM:M_WIDTH + (h + 1) * M_HEAD_DIM] * (M_HEAD_DIM ** -0.5)
        qb16 = qf.astype(BF16)
        kb16 = kf.astype(BF16)
        vb16 = vm[:, hs].astype(BF16)
        bcol = b_c[:, SM_FG + h:SM_FG + h + 1]
        igcol = pre_c[:, SM_IG + h:SM_IG + h + 1]
        brow = b_r[M_HEADS + h:M_HEADS + h + 1, :]
        igrow = pre_r[h:h + 1, :]
        blast = bcol[ln - 1:ln, :]
        m_prev = m_s[h:h + 1, 0:1]
        dmat = jnp.where(causal, bcol - brow + igrow, -jnp.inf)
        m_inter = bcol + m_prev
        m_i = jnp.maximum(m_inter, jnp.max(dmat, axis=1, keepdims=True))
        s = _dot_nt(qb16, kb16) * jnp.exp(dmat - m_i)
        scale = jnp.exp(m_inter - m_i)
        c_prev = c_s[h]
        n_prev = n_s[h:h + 1, :]
        num = _dot(s.astype(BF16), vb16) + scale * _dot(qb16, c_prev.astype(BF16))
        den = jnp.sum(s, axis=1, keepdims=True) + scale * jnp.sum(qf * n_prev, axis=1, keepdims=True)
        hh = num / jnp.maximum(jnp.abs(den), jnp.exp(-m_i))
        m_new = m_i[ln - 1:ln, :]
        decay = jnp.exp(blast + m_prev - m_new)
        wcol = jnp.exp(blast - bcol + igcol - m_new)
        kw = kf * wcol
        c_s[h] = decay * c_prev + _dot_tn(kw.astype(BF16), vb16)
        n_s[h:h + 1, :] = decay * n_prev + jnp.sum(kw, axis=0, keepdims=True)
        m_s[h:h + 1, :] = jnp.broadcast_to(m_new, (1, LANES))
        hn = hh * lax.rsqrt(jnp.mean(hh * hh, axis=1, keepdims=True) + EPS) * gain_ref[:, hs]
        hm_ref[0, :, hs] = (hn * _sigmoid(og[:, hs])).astype(BF16)

    @pl.when(c == nc - 1)
    def _():
        cout_ref[0] = c_s[...]
        nout_ref[0] = n_s[...]
        mout_ref[0] = m_s[...]


def _mlstm(z3, conv_prev, c0, n0, m0, conv_w, conv_b, b_igate, b_fgate, mh_gain, ln):
    bsz, t, _ = z3.shape
    nc = t // ln
    pad = SUBLANES
    gt = z3[:, :, OFF_SM + SM_IG:OFF_SM + SM_IG + 2 * M_HEADS]
    gt = gt.reshape(bsz, nc, ln, 2 * M_HEADS).transpose(0, 1, 3, 2)
    cprev = jnp.pad(conv_prev.astype(F32), ((0, 0), (pad - (CONV_W - 1), 0), (0, 0)))
    gbias = jnp.concatenate([b_igate, b_fgate]).astype(F32)
    gbrow = jnp.zeros((1, LANES), F32).at[0, SM_IG:SM_IG + 2 * M_HEADS].set(gbias)
    gbcol = jnp.broadcast_to(gbias[:, None], (2 * M_HEADS, LANES))
    m0b = jnp.broadcast_to(m0.astype(F32)[:, :, None], (bsz, M_HEADS, LANES))
    kern = functools.partial(_mlstm_kernel, ln=ln)
    full = lambda shape: pl.BlockSpec(shape, lambda b, c: (0,) * len(shape))
    hm, cnew, cout, nout, mout = pl.pallas_call(
        kern,
        grid=(bsz, nc),
        in_specs=[
            pl.BlockSpec((1, ln, 2 * M_WIDTH), lambda b, c: (b, c, OFF_QM // (2 * M_WIDTH))),
            pl.BlockSpec((1, ln, M_WIDTH), lambda b, c: (b, c, OFF_VM // M_WIDTH)),
            pl.BlockSpec((1, ln, M_WIDTH), lambda b, c: (b, c, OFF_OG // M_WIDTH)),
            pl.BlockSpec((1, ln, LANES), lambda b, c: (b, c, OFF_SM // LANES)),
            pl.BlockSpec((1, 1, 2 * M_HEADS, ln), lambda b, c: (b, c, 0, 0)),
            pl.BlockSpec((1, pad, 2 * M_WIDTH), lambda b, c: (b, 0, 0)),
            pl.BlockSpec((1, M_HEADS, M_HEAD_DIM, M_HEAD_DIM), lambda b, c: (b, 0, 0, 0)),
            pl.BlockSpec((1, M_HEADS, M_HEAD_DIM), lambda b, c: (b, 0, 0)),
            pl.BlockSpec((1, M_HEADS, LANES), lambda b, c: (b, 0, 0)),
            full((CONV_W, 2 * M_WIDTH)),
            full((1, 2 * M_WIDTH)),
            full((1, LANES)),
            full((2 * M_HEADS, LANES)),
            full((1, M_WIDTH)),
        ],
        out_specs=[
            pl.BlockSpec((1, ln, M_WIDTH), lambda b, c: (b, c, 0)),
            pl.BlockSpec((1, pad, 2 * M_WIDTH), lambda b, c: (b, 0, 0)),
            pl.BlockSpec((1, M_HEADS, M_HEAD_DIM, M_HEAD_DIM), lambda b, c: (b, 0, 0, 0)),
            pl.BlockSpec((1, M_HEADS, M_HEAD_DIM), lambda b, c: (b, 0, 0)),
            pl.BlockSpec((1, M_HEADS, LANES), lambda b, c: (b, 0, 0)),
        ],
        out_shape=[
            jax.ShapeDtypeStruct((bsz, t, M_WIDTH), BF16),
            jax.ShapeDtypeStruct((bsz, pad, 2 * M_WIDTH), F32),
            jax.ShapeDtypeStruct((bsz, M_HEADS, M_HEAD_DIM, M_HEAD_DIM), F32),
            jax.ShapeDtypeStruct((bsz, M_HEADS, M_HEAD_DIM), F32),
            jax.ShapeDtypeStruct((bsz, M_HEADS, LANES), F32),
        ],
        scratch_shapes=[
            pltpu.VMEM((ln + pad, 2 * M_WIDTH), F32),
            pltpu.VMEM((M_HEADS, M_HEAD_DIM, M_HEAD_DIM), F32),
            pltpu.VMEM((M_HEADS, M_HEAD_DIM), F32),
            pltpu.VMEM((M_HEADS, LANES), F32),
        ],
        compiler_params=_cparams(2),
        name="mlstm",
    )(z3, z3, z3, z3, gt, cprev, c0.astype(F32), n0.astype(F32), m0b,
      conv_w.astype(F32), conv_b.astype(F32)[None, :], gbrow, gbcol, mh_gain.astype(F32)[None, :])
    return hm, cnew[:, pad - (CONV_W - 1):, :], cout, nout, mout[:, :, 0]


def _mix_kernel(att_ref, hm_ref, ga_ref, gb_ref, wa_ref, wb_ref, mixed_ref):
    ya = _dot(att_ref[...], wa_ref[...])
    yb = _dot(hm_ref[...], wb_ref[...])
    mixed_ref[...] = (_sigmoid(ga_ref[...]) * ya + _sigmoid(gb_ref[...]) * yb).astype(BF16)


def _mix(att2, hm2, z2, wa, wb):
    n = att2.shape[0]
    tm = min(POST_TM, n)
    row = lambda w: pl.BlockSpec((tm, w), lambda i: (i, 0))
    full = lambda shape: pl.BlockSpec(shape, lambda i: (0,) * len(shape))
    return pl.pallas_call(
        _mix_kernel,
        grid=(n // tm,),
        in_specs=[
            row(ATT_WIDTH), row(M_WIDTH),
            pl.BlockSpec((tm, D_MODEL), lambda i: (i, OFF_GA // D_MODEL)),
            pl.BlockSpec((tm, D_MODEL), lambda i: (i, OFF_GB // D_MODEL)),
            full((ATT_WIDTH, D_MODEL)), full((M_WIDTH, D_MODEL)),
        ],
        out_specs=row(D_MODEL),
        out_shape=jax.ShapeDtypeStruct((n, D_MODEL), BF16),
        compiler_params=_cparams(1),
        name="mix",
    )(att2, hm2, z2, z2, wa, wb)


def _post_kernel(mixed_ref, x_ref, wo_ref, gf_ref, wr_ref, wrl_ref, br_ref, x1_ref, h2_ref, ri_ref):
    x1 = x_ref[...] + _dot(mixed_ref[...], wo_ref[...])
    x1_ref[...] = x1
    h2 = x1 * lax.rsqrt(jnp.mean(x1 * x1, axis=-1, keepdims=True) + EPS) * gf_ref[...]
    h2_ref[...] = h2
    h_hi = h2.astype(BF16)
    h_lo = (h2 - h_hi.astype(F32)).astype(BF16)
    lg = (_dot(h_hi, wr_ref[...]) + _dot(h_lo, wr_ref[...]) + _dot(h_hi, wrl_ref[...])) + br_ref[...]
    lane = lax.broadcasted_iota(I32, lg.shape, 1)
    glog = jnp.where(lane < N_GROUPS, lg, -jnp.inf)
    gmax = jnp.max(glog, axis=1, keepdims=True)
    gstar = jnp.min(jnp.where(glog == gmax, lane, LANES), axis=1, keepdims=True)
    pg = 1.0 / jnp.sum(jnp.exp(glog - gmax), axis=1, keepdims=True)
    lo = N_GROUPS + EXP_PER_GROUP * gstar
    elog = jnp.where((lane >= lo) & (lane < lo + EXP_PER_GROUP), lg, -jnp.inf)
    v0 = jnp.max(elog, axis=1, keepdims=True)
    i0 = jnp.min(jnp.where(elog == v0, lane, LANES), axis=1, keepdims=True)
    elog2 = jnp.where(lane == i0, -jnp.inf, elog)
    v1 = jnp.max(elog2, axis=1, keepdims=True)
    i1 = jnp.min(jnp.where(elog2 == v1, lane, LANES), axis=1, keepdims=True)
    e1 = jnp.exp(v1 - v0)
    w0 = pg / (1.0 + e1)
    w1 = pg * e1 / (1.0 + e1)
    out = jnp.where(lane == 0, (i0 - N_GROUPS).astype(F32), 0.0)
    out = jnp.where(lane == 1, (i1 - N_GROUPS).astype(F32), out)
    out = jnp.where(lane == 2, w0, out)
    out = jnp.where(lane == 3, w1, out)
    ri_ref[...] = out


def _post(mixed, x2, wo, g_ffn, wr, br):
    n = x2.shape[0]
    wr_hi = wr.astype(BF16)
    wr_lo = (wr - wr_hi.astype(F32)).astype(BF16)
    tm = min(POST_TM, n)
    row = lambda w: pl.BlockSpec((tm, w), lambda i: (i, 0))
    full = lambda shape: pl.BlockSpec(shape, lambda i: (0,) * len(shape))
    return pl.pallas_call(
        _post_kernel,
        grid=(n // tm,),
        in_specs=[
            row(D_MODEL), row(D_MODEL), full((D_MODEL, D_MODEL)),
            full((1, D_MODEL)), full((D_MODEL, LANES)), full((D_MODEL, LANES)), full((1, LANES)),
        ],
        out_specs=[row(D_MODEL), row(D_MODEL), row(LANES)],
        out_shape=[jax.ShapeDtypeStruct((n, D_MODEL), F32),
                   jax.ShapeDtypeStruct((n, D_MODEL), F32),
                   jax.ShapeDtypeStruct((n, LANES), F32)],
        compiler_params=_cparams(1),
        name="post",
    )(mixed, x2, wo, g_ffn, wr_hi, wr_lo, br)


def _row_gather_start(idx_ref, n_rows, src_hbm, dst_buf, sem, inline=False):
    def body(r, c):
        tok = idx_ref[0, 0, r]
        pltpu.make_async_copy(src_hbm.at[pl.ds(tok, 1)], dst_buf.at[pl.ds(r, 1)], sem).start()
        return c
    if inline:
        for r in range(n_rows):
            body(r, 0)
    else:
        lax.fori_loop(0, n_rows, body, 0, unroll=8)


def _row_gather_wait(n_rows, src_hbm, dst_buf, sem):
    pltpu.make_async_copy(src_hbm.at[pl.ds(0, n_rows)], dst_buf, sem).wait()


def _moe_kernel(te_ref, nu_ref, tok_ref, tokn_ref, tokn2_ref, h2_hbm, w1_ref, w3_ref, w2_ref, y_ref,
                xbuf, sem, w1b, w3b, w2b, *, tm):
    t = pl.program_id(0)
    n_used = nu_ref[0]
    slot = t % MOE_SLOTS
    slot1 = (t + 1) % MOE_SLOTS
    slot2 = (t + 2) % MOE_SLOTS

    @pl.when((t < n_used) & ((t == 0) | (te_ref[t] != te_ref[jnp.maximum(t - 1, 0)])))
    def _():
        w1b[...] = w1_ref[0, 0].astype(BF16)
        w3b[...] = w3_ref[0, 0].astype(BF16)
        w2b[...] = w2_ref[0, 0].astype(BF16)

    @pl.when(t == 0)
    def _():
        _row_gather_start(tok_ref, tm, h2_hbm, xbuf.at[0], sem.at[0])
        _row_gather_start(tokn_ref, tm, h2_hbm, xbuf.at[1], sem.at[1])

    @pl.when(t < n_used)
    def _():
        _row_gather_wait(tm, h2_hbm, xbuf.at[slot], sem.at[slot])
        _row_gather_start(tokn2_ref, tm, h2_hbm, xbuf.at[slot2], sem.at[slot2], inline=True)
        xb = xbuf[slot].astype(BF16)
        u = _dot(xb, w1b[...])
        a = (u * _sigmoid(u)) * _dot(xb, w3b[...])
        y_ref[...] = _dot(a.astype(BF16), w2b[...])

        @pl.when(t == pl.num_programs(0) - 1)
        def _():
            _row_gather_wait(tm, h2_hbm, xbuf.at[slot1], sem.at[slot1])
            _row_gather_wait(tm, h2_hbm, xbuf.at[slot2], sem.at[slot2])

    @pl.when(t == n_used)
    def _():
        _row_gather_wait(tm, h2_hbm, xbuf.at[slot], sem.at[slot])
        _row_gather_wait(tm, h2_hbm, xbuf.at[slot1], sem.at[slot1])

    @pl.when(t >= n_used)
    def _():
        y_ref[...] = jnp.zeros(y_ref.shape, F32)


def _moe(h2, tile_expert, n_used, row_token, w1, w3, w2, tm):
    n_tiles = tile_expert.shape[0]
    tok3 = row_token.reshape(n_tiles, 1, tm)
    grid_spec = pltpu.PrefetchScalarGridSpec(
        num_scalar_prefetch=2,
        grid=(n_tiles,),
        in_specs=[
            pl.BlockSpec((1, 1, tm), lambda t, te, nu: (t, 0, 0), memory_space=pltpu.SMEM),
            pl.BlockSpec((1, 1, tm), lambda t, te, nu: (jnp.minimum(t + 1, n_tiles - 1), 0, 0),
                         memory_space=pltpu.SMEM),
            pl.BlockSpec((1, 1, tm), lambda t, te, nu: (jnp.minimum(t + 2, n_tiles - 1), 0, 0),
                         memory_space=pltpu.SMEM),
            pl.BlockSpec(memory_space=pl.ANY),
            pl.BlockSpec((1, 1, D_MODEL, D_EXPERT), lambda t, te, nu: (0, te[t], 0, 0)),
            pl.BlockSpec((1, 1, D_MODEL, D_EXPERT), lambda t, te, nu: (0, te[t], 0, 0)),
            pl.BlockSpec((1, 1, D_EXPERT, D_MODEL), lambda t, te, nu: (0, te[t], 0, 0)),
        ],
        out_specs=pl.BlockSpec((tm, D_MODEL), lambda t, te, nu: (t, 0)),
        scratch_shapes=[pltpu.VMEM((MOE_SLOTS, tm, D_MODEL), F32), pltpu.SemaphoreType.DMA((MOE_SLOTS,)),
                        pltpu.VMEM((D_MODEL, D_EXPERT), BF16), pltpu.VMEM((D_MODEL, D_EXPERT), BF16),
                        pltpu.VMEM((D_EXPERT, D_MODEL), BF16)],
    )
    return pl.pallas_call(
        functools.partial(_moe_kernel, tm=tm),
        grid_spec=grid_spec,
        out_shape=jax.ShapeDtypeStruct((n_tiles * tm, D_MODEL), F32),
        compiler_params=_cparams(1),
        name="moe",
    )(tile_expert, n_used, tok3, tok3, tok3, h2, w1, w3, w2)


def _route(rinfo, n_tiles, tm):
    n = rinfo.shape[0]
    eid = rinfo[:, 0:2].astype(I32).reshape(-1)
    order = jnp.argsort(eid, stable=True).astype(I32)
    inv = jnp.argsort(order).astype(I32)
    onehot = eid[:, None] == jnp.arange(N_EXPERTS, dtype=I32)[None, :]
    counts = jnp.sum(onehot.astype(I32), axis=0)
    tiles_e = (counts + tm - 1) // tm
    tile_end = jnp.cumsum(tiles_e)
    row_start = (tile_end - tiles_e) * tm
    grp_start = jnp.cumsum(counts) - counts
    shift = row_start - grp_start
    pos = inv + jnp.sum(jnp.where(onehot, shift[None, :], 0), axis=1)
    n_used = tile_end[-1]
    tile_ids = jnp.arange(n_tiles, dtype=I32)
    te_raw = jnp.sum((tile_end[None, :] <= tile_ids[:, None]).astype(I32), axis=1)
    last_e = jnp.sum((tile_end <= n_used - 1).astype(I32))
    tile_expert = jnp.minimum(te_raw, last_e)
    te_c = jnp.minimum(te_raw, N_EXPERTS - 1)
    src = jnp.arange(n_tiles * tm, dtype=I32) - jnp.repeat(shift[te_c], tm)
    lo = jnp.repeat(grp_start[te_c], tm)
    hi = lo + jnp.repeat(counts[te_c], tm)
    valid = (src >= lo) & (src < hi)
    row_token = jnp.where(valid, order[jnp.clip(src, 0, 2 * n - 1)] // 2, 0)
    return tile_expert, n_used.reshape(1).astype(I32), row_token, pos.reshape(n, 2)


def _final_kernel(pos_ref, posn_ref, y_hbm, x1_ref, ri_ref, pe_ref, gp_ref, wpg_ref, wp_ref, gfin_ref,
                  o_ref, ybuf, sem):
    t = pl.program_id(0)
    nt = pl.num_programs(0)
    slot = t % 2
    tm = TOK_TM

    @pl.when(t == 0)
    def _():
        _row_gather_start(pos_ref, 2 * tm, y_hbm, ybuf.at[0], sem.at[0])

    _row_gather_wait(2 * tm, y_hbm, ybuf.at[slot], sem.at[slot])
    _row_gather_start(posn_ref, 2 * tm, y_hbm, ybuf.at[1 - slot], sem.at[1 - slot], inline=True)
    ri = ri_ref[...]
    moe = ri[:, 2:3] * ybuf[slot, 0:tm] + ri[:, 3:4] * ybuf[slot, tm:2 * tm]
    x2 = x1_ref[...] + moe
    hp = x2 * lax.rsqrt(jnp.mean(x2 * x2, axis=-1, keepdims=True) + EPS) * gp_ref[...]
    gate = _sigmoid(_dot(hp.astype(BF16), wpg_ref[...]))
    x3 = x2 + gate * _dot(pe_ref[...].astype(BF16), wp_ref[...])
    o_ref[...] = x3 * lax.rsqrt(jnp.mean(x3 * x3, axis=-1, keepdims=True) + EPS) * gfin_ref[...]

    @pl.when(t == nt - 1)
    def _():
        _row_gather_wait(2 * tm, y_hbm, ybuf.at[1 - slot], sem.at[1 - slot])


def _final(x1, rinfo, pos, y_sorted, pe2, g_ple, wpg, wp, g_final):
    n = x1.shape[0]
    tm = TOK_TM
    nt = n // tm
    pos3 = pos.reshape(nt, tm, 2).transpose(0, 2, 1).reshape(nt, 1, 2 * tm)
    row = lambda w: pl.BlockSpec((tm, w), lambda i: (i, 0))
    full = lambda shape: pl.BlockSpec(shape, lambda i: (0,) * len(shape))
    return pl.pallas_call(
        _final_kernel,
        grid=(nt,),
        in_specs=[
            pl.BlockSpec((1, 1, 2 * tm), lambda i: (i, 0, 0), memory_space=pltpu.SMEM),
            pl.BlockSpec((1, 1, 2 * tm), lambda i: (jnp.minimum(i + 1, nt - 1), 0, 0),
                         memory_space=pltpu.SMEM),
            pl.BlockSpec(memory_space=pl.ANY),
            row(D_MODEL), row(LANES), row(PLE_DIM),
            full((1, D_MODEL)), full((D_MODEL, D_MODEL)), full((PLE_DIM, D_MODEL)), full((1, D_MODEL)),
        ],
        out_specs=row(D_MODEL),
        out_shape=jax.ShapeDtypeStruct((n, D_MODEL), F32),
        scratch_shapes=[pltpu.VMEM((2, 2 * tm, D_MODEL), F32), pltpu.SemaphoreType.DMA((2,))],
        compiler_params=_cparams(1),
        name="final",
    )(pos3, pos3, y_sorted, x1, rinfo, pe2, g_ple, wpg, wp, g_final)


def _pack_w_in(w):
    offs = np.concatenate([[0], np.cumsum(PROJ_SIZES)])
    qa, ka, va, qi, ki, wi, qm, km, vm, ig, fg, og, ga, gb = [w[:, offs[k]:offs[k + 1]] for k in range(14)]
    small = jnp.concatenate([ki, wi, ig, fg], axis=1)
    small = jnp.pad(small, ((0, 0), (0, SM_WIDTH - small.shape[1])))
    return jnp.concatenate([qa, qi, qm, km, vm, og, ga, gb, ka, va, small], axis=1).astype(BF16)


def _prep_weights(g_mix, w_in, w_att_out, w_mlstm_out, w_out, g_ffn, router_gw, router_gb, router_ew,
                  router_eb, w1, w3, w2, g_ple, w_ple, w_ple_gate, g_final):
    wr = jnp.concatenate([router_gw, router_ew], axis=1).astype(F32)
    wr = jnp.pad(wr, ((0, 0), (0, LANES - wr.shape[1])))
    br = jnp.concatenate([router_gb, router_eb]).astype(F32)
    br = jnp.pad(br, (0, LANES - br.shape[0]))[None, :]
    return dict(
        g_mix=g_mix.astype(F32)[None, :], w_pack=_pack_w_in(w_in),
        wa=w_att_out.astype(BF16), wb=w_mlstm_out.astype(BF16), wo=w_out.astype(BF16),
        g_ffn=g_ffn.astype(F32)[None, :], wr=wr, br=br,
        w1=w1.reshape((1,) + w1.shape[-3:]), w3=w3.reshape((1,) + w3.shape[-3:]),
        w2=w2.reshape((1,) + w2.shape[-3:]),
        g_ple=g_ple.astype(F32)[None, :], wp=w_ple.astype(BF16), wpg=w_ple_gate.astype(BF16),
        g_final=g_final.astype(F32)[None, :])


def _layer(x, pe, past, conv_prev, c0, n0, m0, wts, conv_w, conv_b, b_igate, b_fgate, mh_gain, qb, ln):
    bsz, t, _ = x.shape
    n = bsz * t
    x2 = x.reshape(n, D_MODEL)
    z, kvb, k_new, v_new, ki_new = _inproj(x2, wts["g_mix"], wts["w_pack"])
    z3 = z.reshape(bsz, t, D_PACK)
    kv3 = kvb.reshape(bsz, t, KV_PACK)
    att = _dsa(z3, kv3, past, qb, min(DSA_SB, qb))
    hm, conv_new, c_new, n_new, m_new = _mlstm(z3, conv_prev, c0, n0, m0, conv_w, conv_b,
                                               b_igate, b_fgate, mh_gain, ln)
    mixed = _mix(att.reshape(n, ATT_WIDTH), hm.reshape(n, M_WIDTH), z, wts["wa"], wts["wb"])
    x1, h2, rinfo = _post(mixed, x2, wts["wo"], wts["g_ffn"], wts["wr"], wts["br"])
    tm = MOE_TM if 2 * n >= 4 * MOE_TM * N_EXPERTS else MOE_TM_SMALL
    n_tiles = (2 * n + N_EXPERTS * (tm - 1) + tm - 1) // tm
    tile_expert, n_used, row_token, pos = _route(rinfo, n_tiles, tm)
    y_sorted = _moe(h2, tile_expert, n_used, row_token, wts["w1"], wts["w3"], wts["w2"], tm)
    y = _final(x1, rinfo, pos, y_sorted, pe.reshape(n, PLE_DIM).astype(F32),
               wts["g_ple"], wts["wpg"], wts["wp"], wts["g_final"])
    k_new = k_new.reshape(bsz, t, N_KV_HEADS, HEAD_DIM)
    v_new = v_new.reshape(bsz, t, N_KV_HEADS, HEAD_DIM)
    ki_new = ki_new.reshape(bsz, t, IDX_DIM)
    return y.reshape(bsz, t, D_MODEL), (k_new, v_new, ki_new, conv_new, c_new, n_new, m_new)


def kernel(x_prompt, x_sample, cache_k, cache_v, cache_kidx, state_conv, state_C, state_n, state_m,
           p_prompt, p_sample, g_mix, w_in, conv_w, conv_b, b_igate, b_fgate, mh_gain,
           w_att_out, w_mlstm_out, w_out, g_ffn, router_gw, router_gb, router_ew, router_eb,
           w1, w3, w2, g_ple, w_ple, w_ple_gate, g_final):
    assert g_mix.shape[0] == 1, "single-layer step"
    bp, tp, _ = x_prompt.shape
    bs, ts, _ = x_sample.shape
    sdt = state_C.dtype
    wts = _prep_weights(g_mix[0], w_in[0], w_att_out[0], w_mlstm_out[0], w_out[0], g_ffn[0],
                        router_gw[0], router_gb[0], router_ew[0], router_eb[0], w1, w3, w2,
                        g_ple[0], w_ple[0], w_ple_gate[0], g_final)
    mix = (conv_w[0], conv_b[0], b_igate[0], b_fgate[0], mh_gain[0])
    yp, st_p = _layer(
        x_prompt, p_prompt[0], None,
        jnp.zeros((bp, CONV_W - 1, 2 * M_WIDTH), F32),
        jnp.zeros((bp, M_HEADS, M_HEAD_DIM, M_HEAD_DIM), F32),
        jnp.zeros((bp, M_HEADS, M_HEAD_DIM), F32),
        jnp.zeros((bp, M_HEADS), F32),
        wts, *mix, qb=min(DSA_QB, tp), ln=min(256, tp))
    plen = cache_k.shape[2]
    past = (cache_k[0], cache_v[0], cache_kidx[0])
    ys, st_s = _layer(
        x_sample, p_sample[0], past, state_conv[0], state_C[0], state_n[0], state_m[0],
        wts, *mix, qb=ts, ln=ts)
    outs_p = [s[None] for s in st_p]
    outs_s = [s[None] for s in st_s]
    for lst in (outs_p, outs_s):
        for k in (4, 5, 6):
            lst[k] = lst[k].astype(sdt)
    return (yp, ys, *outs_p, *outs_s)
```
